```python
import jax
import jax.numpy as jnp
from jax import lax
import numpy as np

D_MODEL = 1024
BATCH = 4
SEQ = 4096
DEPTH = 2
DEC_BATCH = 128
DEC_SEQ = 1
PAST_LEN = 2048
PAGE_SIZE = 128

D_MIX = D_MODEL
HEAD_DIM = 64
D_NSA = D_MIX // 2
N_HEADS = D_NSA // HEAD_DIM
N_KV = 2
GROUP = N_HEADS // N_KV
D_KV = 2 * N_KV * HEAD_DIM
CMP_BLOCK = 64
N_SELECT = 16
WINDOW = 512
Q_BLOCK = 64
D_POOL = D_MIX // 4
POOL_WINDOWS = (2, 4, 8, 16)
N_POOL_GROUPS = 4
POOL_GROUP_DIM = D_POOL // N_POOL_GROUPS
POOL_MAX = 16
D_GMLP = D_MIX - D_NSA - D_POOL
N_GMLP_GROUPS = 4
GMLP_GROUP_DIM = D_GMLP // N_GMLP_GROUPS
CHUNK = 128
D_IN = 2 * D_NSA + 3 * D_KV + 3 * N_HEADS + 2 * D_POOL + 3 * D_GMLP
EPS = 1e-6
NEG_INF = -1e30
SEL_FORCED = 1e4
SEL_INVALID = -1e4

kernel_name = 'hymba_nsa_pool_gmlp_step'


def rms_norm(x, g):
    xf = x.astype(jnp.float32)
    y = xf * lax.rsqrt(jnp.mean(xf * xf, axis=-1, keepdims=True) + EPS)
    return (y * g.astype(jnp.float32)).astype(x.dtype)


def masked_softmax(s, mask):
    s = jnp.where(mask, s, NEG_INF)
    p = jnp.exp(s - jnp.max(s, axis=-1, keepdims=True)) * mask
    return p / jnp.maximum(jnp.sum(p, axis=-1, keepdims=True), 1e-30)


def alibi_slopes():
    h = jnp.arange(1, N_HEADS + 1, dtype=jnp.float32)
    return jnp.exp2(-8.0 * h / N_HEADS).reshape(N_KV, GROUP)


def project(h, w):
    p = jnp.einsum('btd,de->bte', h, w)
    sizes = (D_NSA, D_KV, D_KV, D_KV, 3 * N_HEADS, D_NSA, D_POOL, D_POOL, D_GMLP, D_GMLP, D_GMLP)
    return jnp.split(p, np.cumsum(sizes)[:-1].tolist(), axis=-1)


def compress_blocks(kv, pe, w1, w2):
    b, L = kv.shape[0], kv.shape[1]
    n = L // CMP_BLOCK
    blocks = kv[:, :n * CMP_BLOCK].astype(jnp.float32).reshape(b, n, CMP_BLOCK, 2, N_KV, HEAD_DIM)
    blocks = blocks + jnp.swapaxes(pe, 0, 1)[None, None, :, :, None, :]
    hid = jax.nn.silu(jnp.einsum('bncrkd,rde->bncrke', blocks, w1))
    return jnp.einsum('bnrkd,rde->bnrke', jnp.mean(hid, axis=2), w2)


def nsa_attend(q, gates, kv_cmp, kv_sel, kv_win, q_pos0, win_pos0, pe, w1, w2):
    f32 = jnp.float32
    b, tq = q.shape[0], q.shape[1]
    L = kv_sel.shape[1]
    slopes = alibi_slopes()
    sl5 = slopes[None, None, :, :, None]
    qf = q.astype(f32).reshape(b, tq, N_KV, GROUP, HEAD_DIM) * (HEAD_DIM ** -0.5)
    t_pos = q_pos0 + jnp.arange(tq)
    kvc = compress_blocks(kv_cmp, pe.astype(f32), w1.astype(f32), w2.astype(f32))
    n_cmp = kvc.shape[1]
    dist_c = t_pos[:, None] - ((jnp.arange(n_cmp) + 1) * CMP_BLOCK - 1)[None, :]
    s_c = jnp.einsum('bqkgd,bnkd->bqkgn', qf, kvc[:, :, 0]) - sl5 * dist_c.astype(f32)[None, :, None, None, :]
    p_c = masked_softmax(s_c, (dist_c >= 0)[None, :, None, None, :])
    o_c = jnp.einsum('bqkgn,bnkd->bqkgd', p_c, kvc[:, :, 1])
    n_blk = -(-L // CMP_BLOCK)
    k_eff = min(N_SELECT, n_blk)
    imp = jnp.pad(jnp.sum(p_c, axis=3), ((0, 0), (0, 0), (0, 0), (0, n_blk - n_cmp)))
    blk = jnp.arange(n_blk)
    cur = t_pos // CMP_BLOCK
    forced = (blk[None, :] == cur[:, None]) | (blk[None, :] == 0)
    started = blk[None, :] <= cur[:, None]
    score = jnp.where(forced[None, :, None, :], SEL_FORCED,
                      jnp.where(started[None, :, None, :], imp, SEL_INVALID))
    top_val, top_idx = lax.top_k(score, k_eff)
    top_ok = top_val > SEL_INVALID * 0.5
    kb = jnp.pad(kv_sel.astype(f32), ((0, 0), (0, n_blk * CMP_BLOCK - L), (0, 0), (0, 0), (0, 0)))
    kb = kb.reshape(b, n_blk, CMP_BLOCK, 2, N_KV, HEAD_DIM).transpose(0, 4, 1, 2, 3, 5)
    kvw = jnp.pad(kv_win.astype(f32), ((0, 0), (WINDOW, 0), (0, 0), (0, 0), (0, 0)))
    qblk = Q_BLOCK if tq % Q_BLOCK == 0 else tq
    n_qb = tq // qblk

    def to_blocks(a):
        return a.reshape((b, n_qb, qblk) + a.shape[2:]).swapaxes(0, 1)

    def from_blocks(a):
        return a.swapaxes(0, 1).reshape((b, tq) + a.shape[3:])

    bi = jnp.arange(b)[:, None, None, None]
    ki = jnp.arange(N_KV)[None, None, :, None]
    w_off = jnp.arange(WINDOW + qblk)
    in_blk = jnp.arange(CMP_BLOCK)

    def sweep(args):
        qi, idx, ok, start = args
        tqb = q_pos0 + start + jnp.arange(qblk)
        g = kb[bi, ki, idx].reshape(b, qblk, N_KV, k_eff * CMP_BLOCK, 2, HEAD_DIM)
        s_pos = (idx[..., None] * CMP_BLOCK + in_blk).reshape(b, qblk, N_KV, k_eff * CMP_BLOCK)
        dist_s = tqb[None, :, None, None] - s_pos
        mask_s = (dist_s >= 0) & jnp.repeat(ok, CMP_BLOCK, axis=-1)
        s_s = jnp.einsum('bqkgd,bqksd->bqkgs', qi, g[..., 0, :]) - sl5 * dist_s.astype(f32)[:, :, :, None, :]
        o_s = jnp.einsum('bqkgs,bqksd->bqkgd', masked_softmax(s_s, mask_s[:, :, :, None, :]), g[..., 1, :])
        wk = lax.dynamic_slice_in_dim(kvw, q_pos0 + start - win_pos0, WINDOW + qblk, axis=1)
        w_pos = q_pos0 + start - WINDOW + w_off
        dist_w = tqb[:, None] - w_pos[None, :]
        mask_w = (dist_w >= 0) & (dist_w < WINDOW) & (w_pos >= win_pos0)[None, :]
        s_w = jnp.einsum('bqkgd,bskd->bqkgs', qi, wk[:, :, 0]) - sl5 * dist_w.astype(f32)[None, :, None, None, :]
        o_w = jnp.einsum('bqkgs,bskd->bqkgd', masked_softmax(s_w, mask_w[None, :, None, None, :]), wk[:, :, 1])
        return o_s, o_w

    o_s, o_w = lax.map(sweep, (to_blocks(qf), to_blocks(top_idx), to_blocks(top_ok), jnp.arange(n_qb) * qblk))
    o_s = from_blocks(o_s)
    o_w = from_blocks(o_w)
    gf = gates.astype(f32).reshape(b, tq, N_KV, GROUP, 3)
    o = gf[..., 0:1] * o_c + gf[..., 1:2] * o_s + gf[..., 2:3] * o_w
    return o.reshape(b, tq, D_NSA)


def pool_mix(xin, prev, pos0, w_pool, scale):
    b, t = xin.shape[0], xin.shape[1]
    xf = jnp.concatenate([prev, xin], axis=1).astype(jnp.float32)
    cs = jnp.cumsum(jnp.pad(xf, ((0, 0), (1, 0), (0, 0))), axis=1)
    hi = cs[:, POOL_MAX:]
    pos = pos0 + jnp.arange(t)
    pooled = []
    for gi, w in enumerate(POOL_WINDOWS):
        sl = slice(gi * POOL_GROUP_DIM, (gi + 1) * POOL_GROUP_DIM)
        lo = cs[:, POOL_MAX - w:POOL_MAX - w + t, sl]
        cnt = jnp.minimum(w, pos + 1).astype(jnp.float32)[None, :, None]
        pooled.append((hi[:, :, sl] - lo) / cnt)
    diff = (jnp.concatenate(pooled, axis=-1) - xf[:, POOL_MAX - 1:]).reshape(b, t, N_POOL_GROUPS, POOL_GROUP_DIM)
    y = jnp.einsum('btgc,gce->btge', diff, w_pool).reshape(b, t, D_POOL)
    return y * scale


def gmlp_mix(u, v, g_norm, ws, bs, rows):
    b, t = u.shape[0], u.shape[1]
    vg = v.astype(jnp.float32).reshape(b, t, N_GMLP_GROUPS, GMLP_GROUP_DIM)
    vn = vg * lax.rsqrt(jnp.mean(vg * vg, axis=-1, keepdims=True) + EPS) \
        * g_norm.astype(jnp.float32).reshape(N_GMLP_GROUPS, GMLP_GROUP_DIM)
    n = t // rows
    w = ws[:, :rows, :rows] * jnp.tril(jnp.ones((rows, rows), dtype=ws.dtype))
    s = jnp.einsum('gij,bnjgc->bnigc', w, vn.reshape(b, n, rows, N_GMLP_GROUPS, GMLP_GROUP_DIM))
    s = s + jnp.swapaxes(bs[:, :rows], 0, 1)[None, None, :, :, None]
    return u * s.reshape(b, t, D_GMLP), vn.reshape(b, t, D_GMLP).astype(v.dtype)


def merge_groups(x, ya, za, yb, zb, yc, zc, w_out, g_post):
    mix = jnp.concatenate([ya * jax.nn.silu(za), yb * jax.nn.silu(zb), yc * jax.nn.silu(zc)], axis=-1)
    out = jnp.einsum('bte,ed->btd', mix, w_out)
    return x + rms_norm(out, g_post).astype(x.dtype)


def setup_inputs(seed: int = 0) -> dict:
    key = jax.random.key(seed)
    k = jax.random.split(key, 20)
    f32 = jnp.float32
    n_pages = PAST_LEN // PAGE_SIZE
    n_used = DEC_BATCH * n_pages
    n_phys = n_used + max(1, n_used // 4)
    win_keep = min(WINDOW, PAST_LEN)

    def nrm(kk, shape, s=1.0):
        return s * jax.random.normal(kk, shape, f32)

    page_table = jax.random.permutation(k[0], n_phys)[:n_used].reshape(DEC_BATCH, n_pages).astype(jnp.int32)
    return {
        'x_prompt': nrm(k[1], (BATCH, SEQ, D_MODEL)),
        'x_sample': nrm(k[2], (DEC_BATCH, DEC_SEQ, D_MODEL)),
        'cache_kv_cmp': nrm(k[3], (DEPTH, n_phys, PAGE_SIZE, 2, N_KV, HEAD_DIM)),
        'cache_kv_sel': nrm(k[4], (DEPTH, n_phys, PAGE_SIZE, 2, N_KV, HEAD_DIM)),
        'cache_kv_win': nrm(k[5], (DEPTH, DEC_BATCH, win_keep, 2, N_KV, HEAD_DIM)),
        'state_pool': nrm(k[6], (DEPTH, DEC_BATCH, POOL_MAX - 1, D_POOL)),
        'page_table': page_table,
        'norm_pre': 1.0 + nrm(k[7], (DEPTH, D_MODEL), 0.02),
        'w_in': nrm(k[8], (DEPTH, D_MODEL, D_IN), D_MODEL ** -0.5),
        'cmp_pe': nrm(k[9], (DEPTH, 2, CMP_BLOCK, HEAD_DIM), 0.1),
        'cmp_w1': nrm(k[10], (DEPTH, 2, HEAD_DIM, HEAD_DIM), HEAD_DIM ** -0.5),
        'cmp_w2': nrm(k[11], (DEPTH, 2, HEAD_DIM, HEAD_DIM), HEAD_DIM ** -0.5),
        'pool_w': nrm(k[12], (DEPTH, N_POOL_GROUPS, POOL_GROUP_DIM, POOL_GROUP_DIM), POOL_GROUP_DIM ** -0.5),
        'pool_scale': 1.0 + nrm(k[13], (DEPTH, D_POOL), 0.02),
        'gmlp_norm': 1.0 + nrm(k[14], (DEPTH, D_GMLP), 0.02),
        'gmlp_ws': nrm(k[15], (DEPTH, N_GMLP_GROUPS, CHUNK, CHUNK), CHUNK ** -0.5),
        'gmlp_bs': 1.0 + nrm(k[16], (DEPTH, N_GMLP_GROUPS, CHUNK), 0.02),
        'w_out': nrm(k[17], (DEPTH, D_MIX, D_MODEL), D_MIX ** -0.5),
        'norm_post': 1.0 + nrm(k[18], (DEPTH, D_MODEL), 0.02),
    }


def reference(x_prompt, x_sample, cache_kv_cmp, cache_kv_sel, cache_kv_win, state_pool, page_table,
              norm_pre, w_in, cmp_pe, cmp_w1, cmp_w2, pool_w, pool_scale, gmlp_norm, gmlp_ws, gmlp_bs,
              w_out, norm_post):
    bp, tp = x_prompt.shape[0], x_prompt.shape[1]
    bs_, ts = x_sample.shape[0], x_sample.shape[1]
    past_len = page_table.shape[1] * PAGE_SIZE
    win_keep = cache_kv_win.shape[2]
    xp, xs = x_prompt, x_sample
    kvc_p, kvc_s, kvs_p, kvs_s, kvw_p, kvw_s, pool_p, pool_s, gv_p, gv_s = ([] for _ in range(10))
    for l in range(DEPTH):
        q, kc, ks, kw, gl, za, pin, zb, u, v, zc = project(rms_norm(xp, norm_pre[l]), w_in[l])
        kc, ks, kw = [a.reshape(bp, tp, 2, N_KV, HEAD_DIM) for a in (kc, ks, kw)]
        gates = jax.nn.sigmoid(gl.astype(jnp.float32)).reshape(bp, tp, N_HEADS, 3)
        ya = nsa_attend(q.reshape(bp, tp, N_HEADS, HEAD_DIM), gates, kc, ks, kw, 0, 0,
                        cmp_pe[l], cmp_w1[l], cmp_w2[l])
        yb = pool_mix(pin, jnp.zeros((bp, POOL_MAX - 1, D_POOL), pin.dtype), 0, pool_w[l], pool_scale[l])
        yc, vn = gmlp_mix(u, v, gmlp_norm[l], gmlp_ws[l], gmlp_bs[l], CHUNK)
        xp = merge_groups(xp, ya, za, yb, zb, yc, zc, w_out[l], norm_post[l])
        kvc_p.append(kc)
        kvs_p.append(ks)
        kvw_p.append(kw[:, tp - min(WINDOW, tp):])
        pool_p.append(pin[:, tp - (POOL_MAX - 1):])
        gv_p.append(vn[:, tp - CHUNK:])
        q, kc, ks, kw, gl, za, pin, zb, u, v, zc = project(rms_norm(xs, norm_pre[l]), w_in[l])
        kc, ks, kw = [a.reshape(bs_, ts, 2, N_KV, HEAD_DIM) for a in (kc, ks, kw)]
        gates = jax.nn.sigmoid(gl.astype(jnp.float32)).reshape(bs_, ts, N_HEADS, 3)
        kc_full = jnp.concatenate(
            [cache_kv_cmp[l][page_table].reshape(bs_, past_len, 2, N_KV, HEAD_DIM), kc], axis=1)
        ks_full = jnp.concatenate(
            [cache_kv_sel[l][page_table].reshape(bs_, past_len, 2, N_KV, HEAD_DIM), ks], axis=1)
        kw_full = jnp.concatenate([cache_kv_win[l], kw], axis=1)
        ya = nsa_attend(q.reshape(bs_, ts, N_HEADS, HEAD_DIM), gates, kc_full, ks_full, kw_full,
                        past_len, past_len - win_keep, cmp_pe[l], cmp_w1[l], cmp_w2[l])
        yb = pool_mix(pin, state_pool[l], past_len, pool_w[l], pool_scale[l])
        yc, vn = gmlp_mix(u, v, gmlp_norm[l], gmlp_ws[l], gmlp_bs[l], ts)
        xs = merge_groups(xs, ya, za, yb, zb, yc, zc, w_out[l], norm_post[l])
        kvc_s.append(kc)
        kvs_s.append(ks)
        kvw_s.append(kw_full[:, ts:])
        pool_s.append(jnp.concatenate([state_pool[l], pin], axis=1)[:, ts:])
        gv_s.append(vn)
    return (xp, xs,
            jnp.stack(kvc_p), jnp.stack(kvc_s),
            jnp.stack(kvs_p), jnp.stack(kvs_s),
            jnp.stack(kvw_p), jnp.stack(kvw_s),
            jnp.stack(pool_p), jnp.stack(pool_s),
            jnp.stack(gv_p), jnp.stack(gv_s))
```

```python
import functools

import jax
import jax.numpy as jnp
from jax import lax
from jax.experimental import pallas as pl
from jax.experimental.pallas import tpu as pltpu

F32 = jnp.float32
BF16 = jnp.bfloat16

D_MODEL = 1024
HEAD_DIM = 64
N_HEADS = 8
N_KV = 2
GROUP = N_HEADS // N_KV
D_NSA = N_HEADS * HEAD_DIM
D_KV = 2 * N_KV * HEAD_DIM
CMP_BLOCK = 64
N_SELECT = 16
WINDOW = 512
D_POOL = 256
POOL_WINDOWS = (2, 4, 8, 16)
POOL_GROUP_DIM = 64
POOL_MAX = 16
D_GMLP = 256
GMLP_GROUP_DIM = 64
CHUNK = 128
PAGE_SIZE = 128
EPS = 1e-6
NEG_INF = -1e30
SEL_FORCED = 1e4
SEL_INVALID = -1e4
Q_SCALE = HEAD_DIM ** -0.5

LANES = 128
Q_WIDE = N_HEADS * LANES
VMEM_LIMIT = 56 * 1024 * 1024

LANE_ONE_A = 64
LANE_ONE_B = 65
LANE_POS_HI = 66
LANE_POS_LO = 67

_SEG = dict(q=(0, 512), kc=(512, 256), ks=(768, 256), kw=(1024, 256), gl=(1280, 24), za=(1304, 512),
            pin=(1816, 256), zb=(2072, 256), u=(2328, 256), v=(2584, 256), zc=(2840, 256))
_PROJ_OUT = (("q", Q_WIDE), ("kc", 256), ("ks", 256), ("kw", 256), ("za", 512), ("pin", 256),
             ("zb", 256), ("u", 256), ("v", 256), ("zc", 256), ("gl", LANES))


def _nt_dot(a, b):
    return lax.dot_general(a, b, (((1,), (1,)), ((), ())), preferred_element_type=F32)


def _tn_dot(a, b):
    return lax.dot_general(a, b, (((0,), (0,)), ((), ())), preferred_element_type=F32)


def _dot(a, b):
    return jnp.dot(a, b, preferred_element_type=F32)


def _sigmoid(x):
    return 1.0 / (1.0 + jnp.exp(-x))


def _silu(x):
    return x * _sigmoid(x)


def _params(*sem):
    return pltpu.CompilerParams(dimension_semantics=sem, vmem_limit_bytes=VMEM_LIMIT)


def _proj_kernel(x_ref, g_ref, w_ref, *out_refs):
    x = x_ref[...]
    ms = jnp.mean(x * x, axis=-1, keepdims=True)
    xn = (x * lax.rsqrt(ms + EPS) * g_ref[...]).astype(BF16)
    off = 0
    for ref, (_, width) in zip(out_refs, _PROJ_OUT):
        ref[...] = _dot(xn, w_ref[:, off:off + width])
        off += width


def _project(x2, g, w, tm):
    m = x2.shape[0]
    n_tot = w.shape[1]
    outs = [jax.ShapeDtypeStruct((m, width), F32) for _, width in _PROJ_OUT]
    res = pl.pallas_call(
        _proj_kernel,
        grid=(m // tm,),
        in_specs=[pl.BlockSpec((tm, D_MODEL), lambda i: (i, 0)),
                  pl.BlockSpec((1, D_MODEL), lambda i: (0, 0)),
                  pl.BlockSpec((D_MODEL, n_tot), lambda i: (0, 0))],
        out_specs=[pl.BlockSpec((tm, width), lambda i: (i, 0)) for _, width in _PROJ_OUT],
        out_shape=outs,
        compiler_params=_params("parallel"),
        name="norm_project",
    )(x2, g, w)
    return dict(zip([n for n, _ in _PROJ_OUT], res))


def _prep_w_in(w):
    def seg(name):
        o, n = _SEG[name]
        return w[:, o:o + n]
    wq = seg("q").reshape(D_MODEL, N_HEADS, HEAD_DIM)
    zeros = jnp.zeros_like(wq)
    lo = jnp.concatenate([wq, zeros], axis=-1)
    hi = jnp.concatenate([zeros, wq], axis=-1)
    first = (jnp.arange(N_HEADS) < GROUP)[None, :, None]
    q_wide = jnp.where(first, lo, hi).reshape(D_MODEL, Q_WIDE)
    gl = jnp.pad(seg("gl"), ((0, 0), (0, LANES - 3 * N_HEADS)))
    cols = [q_wide] + [seg(n) for n, _ in _PROJ_OUT[1:-1]] + [gl]
    return jnp.concatenate(cols, axis=1).astype(BF16)


def _blockdiag4(w):
    z = jnp.zeros((HEAD_DIM, HEAD_DIM), w.dtype)
    blocks = [w[0], w[0], w[1], w[1]]
    rows = [jnp.concatenate([blocks[i] if j == i else z for j in range(4)], axis=1) for i in range(4)]
    return jnp.concatenate(rows, axis=0)


def _pe_full(pe):
    return jnp.concatenate([pe[0], pe[0], pe[1], pe[1]], axis=1)


def _compress_rows(x, pe, w1):
    n = x.shape[0] // CMP_BLOCK
    xb = (x.reshape(n, CMP_BLOCK, D_KV) + pe[None]).reshape(n * CMP_BLOCK, D_KV)
    hid = _silu(_dot(xb.astype(BF16), w1))
    return jnp.sum(hid.reshape(n, CMP_BLOCK, D_KV), axis=1) * (1.0 / CMP_BLOCK)


def _compress_kernel(x_ref, pe_ref, w1_ref, w2_ref, o_ref):
    hm = _compress_rows(x_ref[...], pe_ref[...], w1_ref[...])
    o_ref[...] = _dot(hm.astype(BF16), w2_ref[...])


def _compress_prompt(kc, pe, w1, w2, tm):
    m = kc.shape[0]
    return pl.pallas_call(
        _compress_kernel,
        grid=(m // tm,),
        in_specs=[pl.BlockSpec((tm, D_KV), lambda i: (i, 0)),
                  pl.BlockSpec((CMP_BLOCK, D_KV), lambda i: (0, 0)),
                  pl.BlockSpec((D_KV, D_KV), lambda i: (0, 0)),
                  pl.BlockSpec((D_KV, D_KV), lambda i: (0, 0))],
        out_specs=pl.BlockSpec((tm // CMP_BLOCK, D_KV), lambda i: (i, 0)),
        out_shape=jax.ShapeDtypeStruct((m // CMP_BLOCK, D_KV), F32),
        compiler_params=_params("parallel"),
        name="compress_prompt",
    )(kc, pe, w1, w2)


def _compress_pages_kernel(pt_ref, *refs, n_pages):
    del pt_ref
    page_refs = refs[:n_pages]
    pe_ref, w1_ref, w2_ref, o_ref = refs[n_pages:]
    x = jnp.concatenate([r[0, 0] for r in page_refs], axis=0)
    hm = _compress_rows(x, pe_ref[...], w1_ref[...])
    o_ref[0] = _dot(hm.astype(BF16), w2_ref[...])


def _page_spec(layer, p, n_pages):
    return pl.BlockSpec((1, 1, PAGE_SIZE, D_KV), lambda b, pt: (layer, pt[b * n_pages + p], 0, 0))


def _compress_sample(cache4, pt_flat, layer, pe, w1, w2, db, n_pages):
    n_cmp = n_pages * (PAGE_SIZE // CMP_BLOCK)
    const = lambda shape: pl.BlockSpec(shape, lambda b, pt: (0,) * len(shape))
    grid_spec = pltpu.PrefetchScalarGridSpec(
        num_scalar_prefetch=1,
        grid=(db,),
        in_specs=[_page_spec(layer, p, n_pages) for p in range(n_pages)]
        + [const((CMP_BLOCK, D_KV)), const((D_KV, D_KV)), const((D_KV, D_KV))],
        out_specs=pl.BlockSpec((1, n_cmp, D_KV), lambda b, pt: (b, 0, 0)),
    )
    return pl.pallas_call(
        functools.partial(_compress_pages_kernel, n_pages=n_pages),
        grid_spec=grid_spec,
        out_shape=jax.ShapeDtypeStruct((db, n_cmp, D_KV), F32),
        compiler_params=_params("parallel"),
        name="compress_sample",
    )(pt_flat, *([cache4] * n_pages), pe, w1, w2)


def _extra_lanes(pos, with_onehot):
    lane = jnp.arange(LANES)[None, :]
    blk = (pos // CMP_BLOCK)[:, None]
    hi = ((pos // CMP_BLOCK) * CMP_BLOCK).astype(F32)[:, None]
    lo = (pos % CMP_BLOCK).astype(F32)[:, None]
    out = jnp.where((lane == LANE_ONE_A) | (lane == LANE_ONE_B), 1.0, 0.0)
    out = jnp.where(lane == LANE_POS_HI, hi, out)
    out = jnp.where(lane == LANE_POS_LO, lo, out)
    if with_onehot:
        out = jnp.where((lane < CMP_BLOCK) & (lane == blk), 1.0, out)
    return out.astype(BF16)


def _alibi_lanes(slope, t_hi, t_lo, shape):
    lane = lax.broadcasted_iota(jnp.int32, shape, 1)
    out = jnp.where(lane == LANE_ONE_A, -slope * t_hi, 0.0)
    out = jnp.where(lane == LANE_ONE_B, -slope * t_lo, out)
    return jnp.where((lane == LANE_POS_HI) | (lane == LANE_POS_LO), slope, out)


def _masked_softmax(s, mask, axis):
    s = jnp.where(mask, s, NEG_INF)
    p = jnp.exp(s - jnp.max(s, axis=axis, keepdims=True)) * mask.astype(F32)
    return p / jnp.maximum(jnp.sum(p, axis=axis, keepdims=True), 1e-30)


def _attn_prompt_kernel(q_ref, gl_ref, kvc_ref, ksk_ref, ksv_ref, kwk_ref, kwv_ref, ext_ref, o_ref,
                        kaug_s, kaug_w, v_s, v_w, *, seq, tq, tk):
    i = pl.program_id(1)
    m = GROUP * tq
    n_cmp = seq // CMP_BLOCK
    fill_rows = 512 if seq % 512 == 0 else seq

    @pl.when(i == 0)
    def _fill():
        def chunk(c, carry):
            rows = pl.ds(pl.multiple_of(c * fill_rows, fill_rows), fill_rows)
            ext = ext_ref[rows, :]
            kaug_s[rows, :] = jnp.concatenate([ksk_ref[rows, :].astype(BF16), ext], axis=1)
            kaug_w[rows, :] = jnp.concatenate([kwk_ref[rows, :].astype(BF16), ext], axis=1)
            v_s[rows, :] = ksv_ref[rows, :].astype(BF16)
            v_w[rows, :] = kwv_ref[rows, :].astype(BF16)
            return carry
        lax.fori_loop(0, seq // fill_rows, chunk, 0)

    q0 = i * tq
    row_t = q0 + lax.broadcasted_iota(jnp.int32, (m, 1), 0) % tq
    t_hi = ((row_t // CMP_BLOCK) * CMP_BLOCK).astype(F32)
    t_lo = (row_t % CMP_BLOCK).astype(F32)
    lane_m = lax.broadcasted_iota(jnp.int32, (m, LANES), 1)
    key_lane = lax.broadcasted_iota(jnp.int32, (1, tk), 1)
    sig = _sigmoid(gl_ref[...])

    lane_t = q0 + lax.broadcasted_iota(jnp.int32, (1, m), 1) % tq
    blk_row = lax.broadcasted_iota(jnp.int32, (n_cmp, m), 0)
    dist_c = lane_t - ((blk_row + 1) * CMP_BLOCK - 1)
    mask_c = dist_c >= 0
    blk2 = lax.broadcasted_iota(jnp.int32, (n_cmp, tq), 0)
    cur = (q0 + lax.broadcasted_iota(jnp.int32, (1, tq), 1)) // CMP_BLOCK
    forced = (blk2 == cur) | (blk2 == 0)
    started = blk2 <= cur

    def flash_step(qaug, kaug, vals, j, carry, mask_fn):
        m_i, l_i, acc = carry
        rows = pl.ds(pl.multiple_of(j * tk, tk), tk)
        s = _nt_dot(qaug, kaug[rows, :])
        if mask_fn is not None:
            s = jnp.where(mask_fn(j * tk + key_lane), s, NEG_INF)
        m_new = jnp.maximum(m_i, jnp.max(s, axis=1, keepdims=True))
        alpha = jnp.exp(m_i - m_new)
        p = jnp.exp(s - m_new)
        l_new = alpha * l_i + jnp.sum(p, axis=1, keepdims=True)
        acc_new = alpha * acc + _dot(p.astype(BF16), vals[rows, :])
        return m_new, l_new, acc_new

    init = (jnp.full((m, 1), NEG_INF, F32), jnp.zeros((m, 1), F32), jnp.zeros((m, LANES), F32))
    heads = []
    for kv in range(N_KV):
        qk = q_ref[:, kv * GROUP * LANES:(kv + 1) * GROUP * LANES]
        q4 = (jnp.concatenate([qk[:, g * LANES:(g + 1) * LANES] for g in range(GROUP)], axis=0)
              * Q_SCALE).astype(BF16)
        slopes = [2.0 ** -(kv * GROUP + g + 1) for g in range(GROUP)]
        slope_col = jnp.concatenate([jnp.full((tq, 1), s, F32) for s in slopes], axis=0)
        slope_lane = jnp.concatenate([jnp.full((1, tq), s, F32) for s in slopes], axis=1)
        ali = _alibi_lanes(slope_col, t_hi, t_lo, (m, LANES))

        kvc = kvc_ref[...]
        s_ct = _nt_dot(kvc[:, :LANES].astype(BF16), q4) - slope_lane * dist_c.astype(F32)
        p_ct = _masked_softmax(s_ct, mask_c, 0)
        o_c = _tn_dot(p_ct.astype(BF16), kvc[:, LANES:].astype(BF16))

        imp = p_ct[:, 0:tq]
        for g in range(1, GROUP):
            imp = imp + p_ct[:, g * tq:(g + 1) * tq]
        score = jnp.where(forced, SEL_FORCED, jnp.where(started, imp, SEL_INVALID))
        rank = jnp.zeros((n_cmp, tq), F32)
        for b in range(n_cmp):
            row = score[b:b + 1, :]
            beats = (row > score) | ((row == score) & (blk2 > b))
            rank = rank + beats.astype(F32)
        sel_t = ((rank < N_SELECT) & (score > SEL_INVALID * 0.5)).astype(F32)
        if n_cmp < LANES:
            sel_t = jnp.concatenate([sel_t, jnp.zeros((LANES - n_cmp, tq), F32)], axis=0)
        sel_bias = (sel_t.T - 1.0) * -NEG_INF
        sel_bias = jnp.concatenate([sel_bias] * GROUP, axis=0)

        qaug_s = jnp.concatenate([q4, jnp.where(lane_m < CMP_BLOCK, sel_bias, ali).astype(BF16)], axis=1)
        qaug_w = jnp.concatenate([q4, ali.astype(BF16)], axis=1)

        n_kt = (q0 + tq + tk - 1) // tk
        carry = lax.fori_loop(
            0, n_kt - 1, lambda j, c: flash_step(qaug_s, kaug_s, v_s, j, c, None), init)
        _, l_s, acc_s = flash_step(qaug_s, kaug_s, v_s, n_kt - 1, carry, lambda key_t: key_t <= row_t)
        o_s = acc_s / l_s

        def win_mask(key_t):
            d = row_t - key_t
            return (d >= 0) & (d < WINDOW)
        j_lo = jnp.maximum(q0 - (WINDOW - 1), 0) // tk
        j_hi = (q0 + tq - 1) // tk
        _, l_w, acc_w = lax.fori_loop(
            j_lo, j_hi + 1, lambda j, c: flash_step(qaug_w, kaug_w, v_w, j, c, win_mask), init)
        o_w = acc_w / l_w

        for g in range(GROUP):
            h = kv * GROUP + g
            rows = slice(g * tq, (g + 1) * tq)
            out = (sig[:, 3 * h:3 * h + 1] * o_c[rows] + sig[:, 3 * h + 1:3 * h + 2] * o_s[rows]
                   + sig[:, 3 * h + 2:3 * h + 3] * o_w[rows])
            if kv != h % 2:
                out = pltpu.roll(out, HEAD_DIM, axis=1)
            heads.append(out)

    lane_q = lax.broadcasted_iota(jnp.int32, (tq, LANES), 1)
    for pair in range(N_HEADS // 2):
        o_ref[:, pair * LANES:(pair + 1) * LANES] = jnp.where(
            lane_q < HEAD_DIM, heads[2 * pair], heads[2 * pair + 1])


def _attn_prompt(qw, gl, kvc, ks, kw, ext, batch, seq, tq, tk):
    nq = seq // tq
    n_cmp = seq // CMP_BLOCK
    kern = functools.partial(_attn_prompt_kernel, seq=seq, tq=tq, tk=tk)
    kv_spec = lambda half: pl.BlockSpec((seq, LANES), lambda b, i: (b, half))
    return pl.pallas_call(
        kern,
        grid=(batch, nq),
        in_specs=[pl.BlockSpec((tq, Q_WIDE), lambda b, i: (b * nq + i, 0)),
                  pl.BlockSpec((tq, LANES), lambda b, i: (b * nq + i, 0)),
                  pl.BlockSpec((n_cmp, D_KV), lambda b, i: (b, 0)),
                  kv_spec(0), kv_spec(1), kv_spec(0), kv_spec(1),
                  pl.BlockSpec((seq, LANES), lambda b, i: (0, 0))],
        out_specs=pl.BlockSpec((tq, D_NSA), lambda b, i: (b * nq + i, 0)),
        out_shape=jax.ShapeDtypeStruct((batch * seq, D_NSA), F32),
        scratch_shapes=[pltpu.VMEM((seq, 2 * LANES), BF16), pltpu.VMEM((seq, 2 * LANES), BF16),
                        pltpu.VMEM((seq, LANES), BF16), pltpu.VMEM((seq, LANES), BF16)],
        compiler_params=_params("arbitrary", "arbitrary"),
        name="attn_prompt",
    )(qw, gl, kvc, ks, ks, kw, kw, ext)


def _attn_sample_kernel(pt_ref, *refs, n_pages, past_len, win_keep):
    del pt_ref
    page_refs = refs[:n_pages]
    (q_ref, gl_ref, ksn_ref, kwn_ref, kvc_ref, win_ref, exts_ref, extw_ref, o_ref) = refs[n_pages:]
    n_cmp = past_len // CMP_BLOCK
    cur = past_len // CMP_BLOCK
    t_hi = float(cur * CMP_BLOCK)
    t_lo = float(past_len % CMP_BLOCK)

    qrow = q_ref[0]
    q8f = jnp.concatenate([qrow[:, h * LANES:(h + 1) * LANES] for h in range(N_HEADS)], axis=0) * Q_SCALE
    q8 = q8f.astype(BF16)
    head = lax.broadcasted_iota(jnp.int32, (N_HEADS, 1), 0)
    slope = jnp.exp2(-(head + 1).astype(F32))
    lane8 = lax.broadcasted_iota(jnp.int32, (N_HEADS, LANES), 1)
    ali = _alibi_lanes(slope, t_hi, t_lo, (N_HEADS, LANES))

    kvc = kvc_ref[0]
    kc = kvc[:, :LANES]
    if n_cmp < LANES:
        kc = jnp.concatenate([kc, jnp.zeros((LANES - n_cmp, LANES), F32)], axis=0)
    blk = lax.broadcasted_iota(jnp.int32, (1, LANES), 1)
    dist_c = past_len - ((blk + 1) * CMP_BLOCK - 1)
    mask_c = (dist_c >= 0) & (blk < n_cmp)
    s_c = _nt_dot(q8, kc.astype(BF16)) - slope * dist_c.astype(F32)
    p_c = _masked_softmax(s_c, mask_c, 1)
    o_c = _dot(p_c[:, :n_cmp].astype(BF16), kvc[:, LANES:].astype(BF16))

    forced = (blk == cur) | (blk == 0)
    started = blk <= cur
    eye = (lax.broadcasted_iota(jnp.int32, (LANES, LANES), 0)
           == lax.broadcasted_iota(jnp.int32, (LANES, LANES), 1))
    below = (lax.broadcasted_iota(jnp.int32, (LANES, LANES), 0)
             < lax.broadcasted_iota(jnp.int32, (LANES, LANES), 1))
    sel_rows = []
    for kv in range(N_KV):
        imp = jnp.sum(p_c[kv * GROUP:(kv + 1) * GROUP], axis=0, keepdims=True)
        score = jnp.where(forced, SEL_FORCED, jnp.where(started, imp, SEL_INVALID))
        col = jnp.sum(jnp.where(eye, score, 0.0), axis=1, keepdims=True)
        beats = (col > score) | ((col == score) & below)
        rank = jnp.sum(beats.astype(F32), axis=0, keepdims=True)
        sel = ((rank < N_SELECT) & (score > SEL_INVALID * 0.5)).astype(F32)
        sel_rows += [sel] * GROUP
    sel_bias = (jnp.concatenate(sel_rows, axis=0) - 1.0) * -NEG_INF
    qaug_s = jnp.concatenate([q8, jnp.where(lane8 < CMP_BLOCK, sel_bias, ali).astype(BF16)], axis=1)
    qaug_w = jnp.concatenate([q8, ali.astype(BF16)], axis=1)

    def new_key_score(row_ref):
        k_new = row_ref[0][:, :LANES].astype(BF16).astype(F32)
        return jnp.sum(q8.astype(F32) * k_new, axis=1, keepdims=True)

    def finish(scores, values, s_new, v_new):
        m = s_new
        for s in scores:
            m = jnp.maximum(m, jnp.max(s, axis=1, keepdims=True))
        p_new = jnp.exp(s_new - m)
        l = p_new
        acc = p_new * v_new.astype(BF16).astype(F32)
        for s, v in zip(scores, values):
            p = jnp.exp(s - m)
            l = l + jnp.sum(p, axis=1, keepdims=True)
            acc = acc + _dot(p.astype(BF16), v)
        return acc / l

    scores, values = [], []
    for p, ref in enumerate(page_refs):
        page = ref[0, 0]
        kaug = jnp.concatenate([page[:, :LANES].astype(BF16), exts_ref[p]], axis=1)
        scores.append(_nt_dot(qaug_s, kaug))
        values.append(page[:, LANES:].astype(BF16))
    o_s = finish(scores, values, new_key_score(ksn_ref), ksn_ref[0][:, LANES:])

    win = win_ref[0, 0]
    kaug = jnp.concatenate([win[:, :LANES].astype(BF16), extw_ref[...]], axis=1)
    s_w = _nt_dot(qaug_w, kaug)
    dist_w = win_keep - lax.broadcasted_iota(jnp.int32, (1, win_keep), 1)
    s_w = jnp.where((dist_w >= 0) & (dist_w < WINDOW), s_w, NEG_INF)
    o_w = finish([s_w], [win[:, LANES:].astype(BF16)], new_key_score(kwn_ref), kwn_ref[0][:, LANES:])

    sig = _sigmoid(gl_ref[0])
    gate = [jnp.sum(jnp.where(lane8 == 3 * head + j, sig, 0.0), axis=1, keepdims=True) for j in range(3)]
    out = gate[0] * o_c + gate[1] * o_s + gate[2] * o_w
    out = jnp.where(head < GROUP, out, pltpu.roll(out, HEAD_DIM, axis=1))
    o_ref[0] = out[:, :HEAD_DIM]


def _attn_sample(cache4, win4, pt_flat, layer, qw, gl, ksn, kwn, kvc, ext_s, ext_w, db, n_pages, win_keep):
    past_len = n_pages * PAGE_SIZE
    n_cmp = past_len // CMP_BLOCK
    row = lambda width: pl.BlockSpec((1, 1, width), lambda b, pt: (b, 0, 0))
    grid_spec = pltpu.PrefetchScalarGridSpec(
        num_scalar_prefetch=1,
        grid=(db,),
        in_specs=[_page_spec(layer, p, n_pages) for p in range(n_pages)]
        + [row(Q_WIDE), row(LANES), row(D_KV), row(D_KV),
           pl.BlockSpec((1, n_cmp, D_KV), lambda b, pt: (b, 0, 0)),
           pl.BlockSpec((1, 1, win_keep, D_KV), lambda b, pt: (layer, b, 0, 0)),
           pl.BlockSpec((n_pages, PAGE_SIZE, LANES), lambda b, pt: (0, 0, 0)),
           pl.BlockSpec((win_keep, LANES), lambda b, pt: (0, 0))],
        out_specs=pl.BlockSpec((1, N_HEADS, HEAD_DIM), lambda b, pt: (b, 0, 0)),
    )
    kern = functools.partial(_attn_sample_kernel, n_pages=n_pages, past_len=past_len, win_keep=win_keep)
    out = pl.pallas_call(
        kern,
        grid_spec=grid_spec,
        out_shape=jax.ShapeDtypeStruct((db, N_HEADS, HEAD_DIM), F32),
        compiler_params=_params("parallel"),
        name="attn_sample",
    )(pt_flat, *([cache4] * n_pages), qw.reshape(db, 1, Q_WIDE), gl.reshape(db, 1, LANES),
      ksn.reshape(db, 1, D_KV), kwn.reshape(db, 1, D_KV), kvc, win4, ext_s, ext_w)
    return out.reshape(db, D_NSA)


def _pool_window_lane(shape):
    lane = lax.broadcasted_iota(jnp.int32, shape, len(shape) - 1)
    w = jnp.full(shape, POOL_WINDOWS[0], jnp.int32)
    for gi in range(1, len(POOL_WINDOWS)):
        w = jnp.where(lane >= gi * POOL_GROUP_DIM, POOL_WINDOWS[gi], w)
    return w


def _group_rms(v, ones_bd):
    sq = v * v
    hi = sq.astype(BF16)
    lo = (sq - hi.astype(F32)).astype(BF16)
    return (_dot(hi, ones_bd) + _dot(lo, ones_bd)) * (1.0 / GMLP_GROUP_DIM)


def _mix_out(x, ya, za, yb, zb, yc, zc, wo_ref, gpost):
    out = _dot((ya * _silu(za)).astype(BF16), wo_ref[0:D_NSA, :])
    out += _dot((yb * _silu(zb)).astype(BF16), wo_ref[D_NSA:D_NSA + D_POOL, :])
    out += _dot((yc * _silu(zc)).astype(BF16), wo_ref[D_NSA + D_POOL:, :])
    ms = jnp.mean(out * out, axis=-1, keepdims=True)
    return x + out * lax.rsqrt(ms + EPS) * gpost


def _merge_prompt_kernel(x_ref, ya_ref, za_ref, pin_ref, prev_ref, zb_ref, u_ref, v_ref, zc_ref,
                         pw_ref, ps_ref, gn_ref, ones_ref, ws_ref, bs_ref, wo_ref, gp_ref,
                         xo_ref, vn_ref, xext, *, tm, tiles_per_seq):
    i = pl.program_id(0)
    first = (i % tiles_per_seq) == 0
    pin = pin_ref[...]
    xext[POOL_MAX:, :] = pin
    xext[0:POOL_MAX, :] = jnp.where(first, 0.0, prev_ref[...])
    w_lane = _pool_window_lane((tm, D_POOL))
    acc = pin
    for k in range(1, POOL_MAX):
        acc = acc + jnp.where(w_lane > k, xext[pl.ds(POOL_MAX - k, tm), :], 0.0)
    pos = (i % tiles_per_seq) * tm + lax.broadcasted_iota(jnp.int32, (tm, D_POOL), 0)
    cnt = jnp.minimum(w_lane, pos + 1).astype(F32)
    diff = acc / cnt - pin
    yb = _dot(diff.astype(BF16), pw_ref[...]) * ps_ref[...]

    v = v_ref[...]
    vn = v * lax.rsqrt(_group_rms(v, ones_ref[...]) + EPS) * gn_ref[...]
    vn_ref[...] = vn
    vnb = vn.astype(BF16)
    lane = lax.broadcasted_iota(jnp.int32, (CHUNK, D_GMLP), 1)
    chunks = []
    for c in range(tm // CHUNK):
        vc = vnb[c * CHUNK:(c + 1) * CHUNK]
        s = _dot(ws_ref[0], vc)
        for g in range(1, D_GMLP // GMLP_GROUP_DIM):
            s = jnp.where(lane >= g * GMLP_GROUP_DIM, _dot(ws_ref[g], vc), s)
        chunks.append(s + bs_ref[...])
    yc = u_ref[...] * jnp.concatenate(chunks, axis=0)

    xo_ref[...] = _mix_out(x_ref[...], ya_ref[...], za_ref[...], yb, zb_ref[...], yc, zc_ref[...],
                           wo_ref, gp_ref[...])


def _merge_prompt(x2, ya, pr, wts, seq, tm):
    m = x2.shape[0]
    tiles_per_seq = seq // tm
    per = tm // POOL_MAX
    rows = lambda width: pl.BlockSpec((tm, width), lambda i: (i, 0))
    const = lambda shape: pl.BlockSpec(shape, lambda i: (0,) * len(shape))
    kern = functools.partial(_merge_prompt_kernel, tm=tm, tiles_per_seq=tiles_per_seq)
    return pl.pallas_call(
        kern,
        grid=(m // tm,),
        in_specs=[rows(D_MODEL), rows(D_NSA), rows(D_NSA), rows(D_POOL),
                  pl.BlockSpec((POOL_MAX, D_POOL), lambda i: (jnp.maximum(i * per - 1, 0), 0)),
                  rows(D_POOL), rows(D_GMLP), rows(D_GMLP), rows(D_GMLP),
                  const((D_POOL, D_POOL)), const((1, D_POOL)), const((1, D_GMLP)),
                  const((D_GMLP, D_GMLP)), const((4, CHUNK, CHUNK)), const((CHUNK, D_GMLP)),
                  const((D_MODEL, D_MODEL)), const((1, D_MODEL))],
        out_specs=[rows(D_MODEL), rows(D_GMLP)],
        out_shape=[jax.ShapeDtypeStruct((m, D_MODEL), F32), jax.ShapeDtypeStruct((m, D_GMLP), F32)],
        scratch_shapes=[pltpu.VMEM((tm + POOL_MAX, D_POOL), F32)],
        compiler_params=_params("parallel"),
        name="merge_prompt",
    )(x2, ya, pr["za"], pr["pin"], pr["pin"], pr["zb"], pr["u"], pr["v"], pr["zc"],
      wts["pool_w"], wts["pool_scale"], wts["gmlp_norm"], wts["ones_bd"], wts["ws_tril"], wts["bs_full"],
      wts["w_out"], wts["norm_post"])


def _merge_sample_kernel(x_ref, ya_ref, za_ref, pin_ref, st_ref, zb_ref, u_ref, v_ref, zc_ref,
                         pw_ref, ps_ref, gn_ref, ones_ref, w0_ref, b0_ref, wo_ref, gp_ref,
                         xo_ref, vn_ref):
    pin = pin_ref[...]
    w_lane = _pool_window_lane(pin.shape)
    acc = pin
    for k in range(1, POOL_MAX):
        acc = acc + jnp.where(w_lane > k, st_ref[POOL_MAX - 1 - k], 0.0)
    diff = acc / w_lane.astype(F32) - pin
    yb = _dot(diff.astype(BF16), pw_ref[...]) * ps_ref[...]

    v = v_ref[...]
    vn = v * lax.rsqrt(_group_rms(v, ones_ref[...]) + EPS) * gn_ref[...]
    vn_ref[...] = vn
    yc = u_ref[...] * (w0_ref[...] * vn + b0_ref[...])

    xo_ref[...] = _mix_out(x_ref[...], ya_ref[...], za_ref[...], yb, zb_ref[...], yc, zc_ref[...],
                           wo_ref, gp_ref[...])


def _merge_sample(x2, ya, pr, state_t, wts):
    db = x2.shape[0]
    full = lambda shape: pl.BlockSpec(shape, lambda i: (0,) * len(shape))
    args = (x2, ya, pr["za"], pr["pin"], state_t, pr["zb"], pr["u"], pr["v"], pr["zc"],
            wts["pool_w"], wts["pool_scale"], wts["gmlp_norm"], wts["ones_bd"], wts["w0"], wts["b0"],
            wts["w_out"], wts["norm_post"])
    return pl.pallas_call(
        _merge_sample_kernel,
        grid=(1,),
        in_specs=[full(a.shape) for a in args],
        out_specs=[full((db, D_MODEL)), full((db, D_GMLP))],
        out_shape=[jax.ShapeDtypeStruct((db, D_MODEL), F32), jax.ShapeDtypeStruct((db, D_GMLP), F32)],
        compiler_params=_params("arbitrary"),
        name="merge_sample",
    )(*args)


def _layer_weights(l, norm_pre, w_in, cmp_pe, cmp_w1, cmp_w2, pool_w, pool_scale, gmlp_norm, gmlp_ws, gmlp_bs,
                   w_out, norm_post):
    n_g = D_GMLP // GMLP_GROUP_DIM
    ones_bd = jnp.kron(jnp.eye(n_g, dtype=F32), jnp.ones((GMLP_GROUP_DIM, GMLP_GROUP_DIM), F32)).astype(BF16)
    pw = pool_w[l]
    zp = jnp.zeros_like(pw[0])
    pool_bd = jnp.concatenate(
        [jnp.concatenate([pw[i] if j == i else zp for j in range(4)], axis=1) for i in range(4)], axis=0)
    tril = jnp.tril(jnp.ones((CHUNK, CHUNK), F32))
    return dict(
        norm_pre=norm_pre[l].reshape(1, D_MODEL),
        w_in=_prep_w_in(w_in[l]),
        pe=_pe_full(cmp_pe[l]),
        w1=_blockdiag4(cmp_w1[l]).astype(BF16),
        w2=_blockdiag4(cmp_w2[l]).astype(BF16),
        pool_w=pool_bd.astype(BF16),
        pool_scale=pool_scale[l].reshape(1, D_POOL),
        gmlp_norm=gmlp_norm[l].reshape(1, D_GMLP),
        ones_bd=ones_bd,
        ws_tril=(gmlp_ws[l] * tril).astype(BF16),
        bs_full=jnp.repeat(gmlp_bs[l].T, GMLP_GROUP_DIM, axis=1),
        w0=jnp.repeat(gmlp_ws[l][:, 0, 0], GMLP_GROUP_DIM).reshape(1, D_GMLP),
        b0=jnp.repeat(gmlp_bs[l][:, 0], GMLP_GROUP_DIM).reshape(1, D_GMLP),
        w_out=w_out[l].astype(BF16),
        norm_post=norm_post[l].reshape(1, D_MODEL),
    )


def _largest_tile(n, cap):
    t = cap
    while n % t:
        t //= 2
    return t


def kernel(x_prompt, x_sample, cache_kv_cmp, cache_kv_sel, cache_kv_win, state_pool, page_table, norm_pre, w_in,
           cmp_pe, cmp_w1, cmp_w2, pool_w, pool_scale, gmlp_norm, gmlp_ws, gmlp_bs, w_out, norm_post):
    bp, tp, _ = x_prompt.shape
    db, ts, _ = x_sample.shape
    depth = w_in.shape[0]
    n_pages = page_table.shape[1]
    past_len = n_pages * PAGE_SIZE
    win_keep = cache_kv_win.shape[2]
    n_phys = cache_kv_cmp.shape[1]
    assert ts == 1 and tp % CHUNK == 0 and tp <= CMP_BLOCK * CMP_BLOCK and tp >= WINDOW
    assert past_len // CMP_BLOCK + 1 <= CMP_BLOCK and win_keep == WINDOW

    mp = bp * tp
    tm_proj = _largest_tile(mp, 512)
    tm_cmp = _largest_tile(tp, 1024)
    tm_merge = _largest_tile(tp, 512)
    tq, tk = 128, 256

    cmp4 = cache_kv_cmp.reshape(depth, n_phys, PAGE_SIZE, D_KV)
    sel4 = cache_kv_sel.reshape(depth, n_phys, PAGE_SIZE, D_KV)
    win4 = cache_kv_win.reshape(depth, db, win_keep, D_KV)
    pt_flat = page_table.reshape(-1).astype(jnp.int32)

    ext_prompt = _extra_lanes(jnp.arange(tp, dtype=jnp.int32), True)
    ext_s = _extra_lanes(jnp.arange(past_len, dtype=jnp.int32), True).reshape(n_pages, PAGE_SIZE, LANES)
    ext_w = _extra_lanes(past_len - win_keep + jnp.arange(win_keep, dtype=jnp.int32), False)

    xp = x_prompt.reshape(mp, D_MODEL)
    xs = x_sample.reshape(db, D_MODEL)
    outs = {k: [] for k in ("kvc_p", "kvc_s", "kvs_p", "kvs_s", "kvw_p", "kvw_s", "pool_p", "pool_s", "gv_p", "gv_s")}
    kv_shape = lambda b, t: (b, t, 2, N_KV, HEAD_DIM)
    for l in range(depth):
        wts = _layer_weights(l, norm_pre, w_in, cmp_pe, cmp_w1, cmp_w2, pool_w, pool_scale, gmlp_norm, gmlp_ws,
                             gmlp_bs, w_out, norm_post)
        pr = _project(xp, wts["norm_pre"], wts["w_in"], tm_proj)
        kvc = _compress_prompt(pr["kc"], wts["pe"], wts["w1"], wts["w2"], tm_cmp)
        ya = _attn_prompt(pr["q"], pr["gl"], kvc, pr["ks"], pr["kw"], ext_prompt, bp, tp, tq, tk)
        xp, vn = _merge_prompt(xp, ya, pr, wts, tp, tm_merge)
        outs["kvc_p"].append(pr["kc"].reshape(kv_shape(bp, tp)))
        outs["kvs_p"].append(pr["ks"].reshape(kv_shape(bp, tp)))
        outs["kvw_p"].append(pr["kw"].reshape(kv_shape(bp, tp))[:, tp - WINDOW:])
        outs["pool_p"].append(pr["pin"].reshape(bp, tp, D_POOL)[:, tp - (POOL_MAX - 1):])
        outs["gv_p"].append(vn.reshape(bp, tp, D_GMLP)[:, tp - CHUNK:])
        pr = _project(xs, wts["norm_pre"], wts["w_in"], db)
        kvc = _compress_sample(cmp4, pt_flat, l, wts["pe"], wts["w1"], wts["w2"], db, n_pages)
        ya = _attn_sample(sel4, win4, pt_flat, l, pr["q"], pr["gl"], pr["ks"], pr["kw"], kvc, ext_s, ext_w,
                          db, n_pages, win_keep)
        state_t = jnp.swapaxes(state_pool[l], 0, 1)
        xs, vn = _merge_sample(xs, ya, pr, state_t, wts)
        outs["kvc_s"].append(pr["kc"].reshape(kv_shape(db, 1)))
        outs["kvs_s"].append(pr["ks"].reshape(kv_shape(db, 1)))
        outs["kvw_s"].append(jnp.concatenate([cache_kv_win[l][:, 1:], pr["kw"].reshape(kv_shape(db, 1))], axis=1))
        outs["pool_s"].append(jnp.concatenate([state_pool[l][:, 1:], pr["pin"].reshape(db, 1, D_POOL)], axis=1))
        outs["gv_s"].append(vn.reshape(db, 1, D_GMLP))
    st = {k: jnp.stack(v) for k, v in outs.items()}
    return (xp.reshape(bp, tp, D_MODEL), xs.reshape(db, ts, D_MODEL),
            st["kvc_p"], st["kvc_s"], st["kvs_p"], st["kvs_s"], st["kvw_p"], st["kvw_s"],
            st["pool_p"], st["pool_s"], st["gv_p"], st["gv_s"])
```

```python
import functools

import jax
import jax.numpy as jnp
from jax import lax
from jax.experimental import pallas as pl
from jax.experimental.pallas import tpu as pltpu

F32 = jnp.float32
BF16 = jnp.bfloat16

D_MODEL = 1024
HEAD_DIM = 64
N_HEADS = 8
N_KV = 2
GROUP = N_HEADS // N_KV
D_NSA = N_HEADS * HEAD_DIM
D_KV = 2 * N_KV * HEAD_DIM
CMP_BLOCK = 64
N_SELECT = 16
WINDOW = 512
D_POOL = 256
POOL_WINDOWS = (2, 4, 8, 16)
POOL_GROUP_DIM = 64
POOL_MAX = 16
D_GMLP = 256
GMLP_GROUP_DIM = 64
CHUNK = 128
PAGE_SIZE = 128
EPS = 1e-6
NEG_INF = -1e30
SEL_FORCED = 1e4
SEL_INVALID = -1e4
Q_SCALE = HEAD_DIM ** -0.5

LANES = 128
Q_WIDE = N_HEADS * LANES
VMEM_LIMIT = 56 * 1024 * 1024

ROW_ONE_A = 64
ROW_ONE_B = 65
ROW_POS_HI = 66
ROW_POS_LO = 67

_SEG = dict(q=(0, 512), kc=(512, 256), ks=(768, 256), kw=(1024, 256), gl=(1280, 24), za=(1304, 512),
            pin=(1816, 256), zb=(2072, 256), u=(2328, 256), v=(2584, 256), zc=(2840, 256))
_PROJ_OUT = (("q", Q_WIDE), ("kc", 256), ("ks", 256), ("kw", 256), ("za", 512), ("pin", 256),
             ("zb", 256), ("u", 256), ("v", 256), ("zc", 256), ("gl", LANES))
_KV_NAMES = ("kc", "ks", "kw")


def _nt_dot(a, b):
    return lax.dot_general(a, b, (((1,), (1,)), ((), ())), preferred_element_type=F32)


def _tn_dot(a, b):
    return lax.dot_general(a, b, (((0,), (0,)), ((), ())), preferred_element_type=F32)


def _dot(a, b):
    return jnp.dot(a, b, preferred_element_type=F32)


def _sigmoid(x):
    return 1.0 / (1.0 + jnp.exp(-x))


def _silu(x):
    return x * _sigmoid(x)


def _params(*sem):
    return pltpu.CompilerParams(dimension_semantics=sem, vmem_limit_bytes=VMEM_LIMIT)


def _row_to_col(row):
    eye = (lax.broadcasted_iota(jnp.int32, (LANES, LANES), 0)
           == lax.broadcasted_iota(jnp.int32, (LANES, LANES), 1))
    return jnp.sum(jnp.where(eye, row, 0.0), axis=1, keepdims=True)


def _proj_kernel(x_ref, g_ref, w_ref, *out_refs, emit_rows):
    x = x_ref[...]
    ms = jnp.mean(x * x, axis=-1, keepdims=True)
    xn = (x * lax.rsqrt(ms + EPS) * g_ref[...]).astype(BF16)
    refs = iter(out_refs)
    off = 0
    for name, width in _PROJ_OUT:
        res = _dot(xn, w_ref[:, off:off + width])
        off += width
        if name in _KV_NAMES:
            next(refs)[0] = res.T
            if not emit_rows:
                continue
        next(refs)[...] = res


def _project(x2, g, w, batch, seq, tm, emit_rows):
    m = x2.shape[0]
    n_tot = w.shape[1]
    tiles = seq // tm
    names, shapes, specs = [], [], []
    for name, width in _PROJ_OUT:
        if name in _KV_NAMES:
            names.append(name + "_t")
            shapes.append(jax.ShapeDtypeStruct((batch, width, seq), F32))
            specs.append(pl.BlockSpec((1, width, tm), lambda i: (i // tiles, 0, i % tiles)))
            if not emit_rows:
                continue
        names.append(name)
        shapes.append(jax.ShapeDtypeStruct((m, width), F32))
        specs.append(pl.BlockSpec((tm, width), lambda i: (i, 0)))
    res = pl.pallas_call(
        functools.partial(_proj_kernel, emit_rows=emit_rows),
        grid=(m // tm,),
        in_specs=[pl.BlockSpec((tm, D_MODEL), lambda i: (i, 0)),
                  pl.BlockSpec((1, D_MODEL), lambda i: (0, 0)),
                  pl.BlockSpec((D_MODEL, n_tot), lambda i: (0, 0))],
        out_specs=specs,
        out_shape=shapes,
        compiler_params=_params("parallel"),
        name="norm_project",
    )(x2, g, w)
    return dict(zip(names, res))


def _prep_w_in(w):
    def seg(name):
        o, n = _SEG[name]
        return w[:, o:o + n]
    wq = seg("q").reshape(D_MODEL, N_HEADS, HEAD_DIM)
    zeros = jnp.zeros_like(wq)
    lo = jnp.concatenate([wq, zeros], axis=-1)
    hi = jnp.concatenate([zeros, wq], axis=-1)
    first = (jnp.arange(N_HEADS) < GROUP)[None, :, None]
    q_wide = jnp.where(first, lo, hi).reshape(D_MODEL, Q_WIDE)
    gl = jnp.pad(seg("gl"), ((0, 0), (0, LANES - 3 * N_HEADS)))
    cols = [q_wide] + [seg(n) for n, _ in _PROJ_OUT[1:-1]] + [gl]
    return jnp.concatenate(cols, axis=1).astype(BF16)


def _blockdiag4(w):
    z = jnp.zeros((HEAD_DIM, HEAD_DIM), w.dtype)
    blocks = [w[0], w[0], w[1], w[1]]
    rows = [jnp.concatenate([blocks[i] if j == i else z for j in range(4)], axis=1) for i in range(4)]
    return jnp.concatenate(rows, axis=0)


def _pe_t(pe):
    one = jnp.concatenate([pe[0].T, pe[0].T, pe[1].T, pe[1].T], axis=0)
    return jnp.concatenate([one, one], axis=1)


def _block_avg(n_tok):
    t = jnp.arange(n_tok)[:, None] // CMP_BLOCK
    return (t == jnp.arange(LANES)[None, :]).astype(BF16)


def _compress_t(xt, pe128, w1t, w2, avg):
    n = xt.shape[1]
    pet = jnp.concatenate([pe128] * (n // LANES), axis=1)
    hid = _silu(_dot(w1t, (xt + pet).astype(BF16)))
    hi = hid.astype(BF16)
    lo = (hid - hi.astype(F32)).astype(BF16)
    mean_t = (_dot(hi, avg) + _dot(lo, avg)) * (1.0 / CMP_BLOCK)
    return _tn_dot(mean_t.astype(BF16), w2)


def _compress_kernel(x_ref, pe_ref, w1_ref, w2_ref, avg_ref, o_ref):
    res = _compress_t(x_ref[0], pe_ref[...], w1_ref[...], w2_ref[...], avg_ref[...])
    o_ref[...] = res[:o_ref.shape[0]]


def _compress_prompt(kc_t, wts, tile):
    batch, _, seq = kc_t.shape
    tiles = seq // tile
    const = lambda shape: pl.BlockSpec(shape, lambda b, j: (0,) * len(shape))
    return pl.pallas_call(
        _compress_kernel,
        grid=(batch, tiles),
        in_specs=[pl.BlockSpec((1, D_KV, tile), lambda b, j: (b, 0, j)),
                  const((D_KV, LANES)), const((D_KV, D_KV)), const((D_KV, D_KV)), const((tile, LANES))],
        out_specs=pl.BlockSpec((tile // CMP_BLOCK, D_KV), lambda b, j: (b * tiles + j, 0)),
        out_shape=jax.ShapeDtypeStruct((batch * seq // CMP_BLOCK, D_KV), F32),
        compiler_params=_params("parallel", "parallel"),
        name="compress_prompt",
    )(kc_t, wts["pe_t"], wts["w1_t"], wts["w2"], _block_avg(tile))


def _compress_pages_kernel(pt_ref, *refs, n_pages):
    del pt_ref
    page_refs = refs[:n_pages]
    pe_ref, w1_ref, w2_ref, avg_ref, o_ref = refs[n_pages:]
    xt = jnp.concatenate([r[0, 0] for r in page_refs], axis=1)
    res = _compress_t(xt, pe_ref[...], w1_ref[...], w2_ref[...], avg_ref[...])
    o_ref[0] = res[:o_ref.shape[1]]


def _page_spec(layer, p, n_pages):
    return pl.BlockSpec((1, 1, D_KV, PAGE_SIZE), lambda b, pt: (layer, pt[b * n_pages + p], 0, 0))


def _compress_sample(cache_t, pt_flat, layer, wts, db, n_pages):
    past_len = n_pages * PAGE_SIZE
    n_cmp = past_len // CMP_BLOCK
    const = lambda shape: pl.BlockSpec(shape, lambda b, pt: (0,) * len(shape))
    grid_spec = pltpu.PrefetchScalarGridSpec(
        num_scalar_prefetch=1,
        grid=(db,),
        in_specs=[_page_spec(layer, p, n_pages) for p in range(n_pages)]
        + [const((D_KV, LANES)), const((D_KV, D_KV)), const((D_KV, D_KV)), const((past_len, LANES))],
        out_specs=pl.BlockSpec((1, n_cmp, D_KV), lambda b, pt: (b, 0, 0)),
    )
    return pl.pallas_call(
        functools.partial(_compress_pages_kernel, n_pages=n_pages),
        grid_spec=grid_spec,
        out_shape=jax.ShapeDtypeStruct((db, n_cmp, D_KV), F32),
        compiler_params=_params("parallel"),
        name="compress_sample",
    )(pt_flat, *([cache_t] * n_pages), wts["pe_t"], wts["w1_t"], wts["w2"], _block_avg(past_len))


def _extra_rows(pos, with_onehot):
    row = jnp.arange(LANES)[:, None]
    blk = (pos // CMP_BLOCK)[None, :]
    hi = ((pos // CMP_BLOCK) * CMP_BLOCK).astype(F32)[None, :]
    lo = (pos % CMP_BLOCK).astype(F32)[None, :]
    out = jnp.where((row == ROW_ONE_A) | (row == ROW_ONE_B), 1.0, 0.0) + jnp.zeros_like(hi)
    out = jnp.where(row == ROW_POS_HI, hi, out)
    out = jnp.where(row == ROW_POS_LO, lo, out)
    if with_onehot:
        out = jnp.where((row < CMP_BLOCK) & (row == blk), 1.0, out)
    return out.astype(BF16)


def _alibi_lanes(slope, t_hi, t_lo, shape):
    lane = lax.broadcasted_iota(jnp.int32, shape, 1)
    out = jnp.where(lane == ROW_ONE_A, -slope * t_hi, 0.0)
    out = jnp.where(lane == ROW_ONE_B, -slope * t_lo, out)
    return jnp.where((lane == ROW_POS_HI) | (lane == ROW_POS_LO), slope, out)


def _masked_softmax(s, mask, axis):
    s = jnp.where(mask, s, NEG_INF)
    p = jnp.exp(s - jnp.max(s, axis=axis, keepdims=True)) * mask.astype(F32)
    return p / jnp.maximum(jnp.sum(p, axis=axis, keepdims=True), 1e-30)


def _attn_prompt_kernel(q_ref, gl_ref, kvc_ref, ksk_ref, ksv_ref, kwk_ref, kwv_ref, ext_ref, o_ref,
                        kaug_s, kaug_w, v_s, v_w, *, seq, tq, tk):
    i = pl.program_id(1)
    m = GROUP * tq
    n_cmp = seq // CMP_BLOCK

    @pl.when(i == 0)
    def _fill():
        for c in range(seq // tk):
            cols = slice(c * tk, (c + 1) * tk)
            ext = ext_ref[:, cols]
            kaug_s[c] = jnp.concatenate([ksk_ref[0, :, cols].astype(BF16), ext], axis=0)
            kaug_w[c] = jnp.concatenate([kwk_ref[0, :, cols].astype(BF16), ext], axis=0)
            v_s[c] = ksv_ref[0, :, cols].astype(BF16)
            v_w[c] = kwv_ref[0, :, cols].astype(BF16)

    q0 = i * tq
    row_t = q0 + lax.broadcasted_iota(jnp.int32, (m, 1), 0) % tq
    t_hi = ((row_t // CMP_BLOCK) * CMP_BLOCK).astype(F32)
    t_lo = (row_t % CMP_BLOCK).astype(F32)
    lane_m = lax.broadcasted_iota(jnp.int32, (m, LANES), 1)
    key_lane = lax.broadcasted_iota(jnp.int32, (1, tk), 1)
    sig = _sigmoid(gl_ref[...])

    lane_t = q0 + lax.broadcasted_iota(jnp.int32, (1, m), 1) % tq
    blk_row = lax.broadcasted_iota(jnp.int32, (n_cmp, m), 0)
    dist_c = lane_t - ((blk_row + 1) * CMP_BLOCK - 1)
    mask_c = dist_c >= 0
    blk2 = lax.broadcasted_iota(jnp.int32, (n_cmp, tq), 0)
    cur = (q0 + lax.broadcasted_iota(jnp.int32, (1, tq), 1)) // CMP_BLOCK
    forced = (blk2 == cur) | (blk2 == 0)
    started = blk2 <= cur

    def flash_step(qaug, kaug, vals, j, carry, mask_fn):
        m_i, l_i, acc = carry
        s = _dot(qaug, kaug[j])
        if mask_fn is not None:
            s = jnp.where(mask_fn(j * tk + key_lane), s, NEG_INF)
        m_new = jnp.maximum(m_i, jnp.max(s, axis=1, keepdims=True))
        alpha = jnp.exp(m_i - m_new)
        p = jnp.exp(s - m_new)
        l_new = alpha * l_i + jnp.sum(p, axis=1, keepdims=True)
        acc_new = alpha * acc + _nt_dot(p.astype(BF16), vals[j])
        return m_new, l_new, acc_new

    init = (jnp.full((m, 1), NEG_INF, F32), jnp.zeros((m, 1), F32), jnp.zeros((m, LANES), F32))
    heads = []
    for kv in range(N_KV):
        qk = q_ref[:, kv * GROUP * LANES:(kv + 1) * GROUP * LANES]
        q4 = (jnp.concatenate([qk[:, g * LANES:(g + 1) * LANES] for g in range(GROUP)], axis=0)
              * Q_SCALE).astype(BF16)
        slopes = [2.0 ** -(kv * GROUP + g + 1) for g in range(GROUP)]
        slope_col = jnp.concatenate([jnp.full((tq, 1), s, F32) for s in slopes], axis=0)
        slope_lane = jnp.concatenate([jnp.full((1, tq), s, F32) for s in slopes], axis=1)
        ali = _alibi_lanes(slope_col, t_hi, t_lo, (m, LANES))

        kvc = kvc_ref[...]
        s_ct = _nt_dot(kvc[:, :LANES].astype(BF16), q4) - slope_lane * dist_c.astype(F32)
        p_ct = _masked_softmax(s_ct, mask_c, 0)
        o_c = _tn_dot(p_ct.astype(BF16), kvc[:, LANES:].astype(BF16))

        imp = p_ct[:, 0:tq]
        for g in range(1, GROUP):
            imp = imp + p_ct[:, g * tq:(g + 1) * tq]
        score = jnp.where(forced, SEL_FORCED, jnp.where(started, imp, SEL_INVALID))
        rank = jnp.zeros((n_cmp, tq), F32)
        for b in range(n_cmp):
            row = score[b:b + 1, :]
            beats = (row > score) | ((row == score) & (blk2 > b))
            rank = rank + beats.astype(F32)
        sel_t = ((rank < N_SELECT) & (score > SEL_INVALID * 0.5)).astype(F32)
        if n_cmp < LANES:
            sel_t = jnp.concatenate([sel_t, jnp.zeros((LANES - n_cmp, tq), F32)], axis=0)
        sel_bias = (sel_t.T - 1.0) * -NEG_INF
        sel_bias = jnp.concatenate([sel_bias] * GROUP, axis=0)

        qaug_s = jnp.concatenate([q4, jnp.where(lane_m < CMP_BLOCK, sel_bias, ali).astype(BF16)], axis=1)
        qaug_w = jnp.concatenate([q4, ali.astype(BF16)], axis=1)

        n_kt = (q0 + tq + tk - 1) // tk
        carry = lax.fori_loop(
            0, n_kt - 1, lambda j, c: flash_step(qaug_s, kaug_s, v_s, j, c, None), init)
        _, l_s, acc_s = flash_step(qaug_s, kaug_s, v_s, n_kt - 1, carry, lambda key_t: key_t <= row_t)
        o_s = acc_s / l_s

        def win_mask(key_t):
            d = row_t - key_t
            return (d >= 0) & (d < WINDOW)
        j_lo = jnp.maximum(q0 - (WINDOW - 1), 0) // tk
        j_hi = (q0 + tq - 1) // tk
        _, l_w, acc_w = lax.fori_loop(
            j_lo, j_hi + 1, lambda j, c: flash_step(qaug_w, kaug_w, v_w, j, c, win_mask), init)
        o_w = acc_w / l_w

        for g in range(GROUP):
            h = kv * GROUP + g
            rows = slice(g * tq, (g + 1) * tq)
            out = (sig[:, 3 * h:3 * h + 1] * o_c[rows] + sig[:, 3 * h + 1:3 * h + 2] * o_s[rows]
                   + sig[:, 3 * h + 2:3 * h + 3] * o_w[rows])
            if kv != h % 2:
                out = pltpu.roll(out, HEAD_DIM, axis=1)
            heads.append(out)

    lane_q = lax.broadcasted_iota(jnp.int32, (tq, LANES), 1)
    for pair in range(N_HEADS // 2):
        o_ref[:, pair * LANES:(pair + 1) * LANES] = jnp.where(
            lane_q < HEAD_DIM, heads[2 * pair], heads[2 * pair + 1])


def _attn_prompt(qw, gl, kvc, ks_t, kw_t, ext, batch, seq, tq, tk):
    nq = seq // tq
    n_cmp = seq // CMP_BLOCK
    n_kt = seq // tk
    kern = functools.partial(_attn_prompt_kernel, seq=seq, tq=tq, tk=tk)
    kv_spec = lambda half: pl.BlockSpec((1, LANES, seq), lambda b, i: (b, half, 0))
    return pl.pallas_call(
        kern,
        grid=(batch, nq),
        in_specs=[pl.BlockSpec((tq, Q_WIDE), lambda b, i: (b * nq + i, 0)),
                  pl.BlockSpec((tq, LANES), lambda b, i: (b * nq + i, 0)),
                  pl.BlockSpec((n_cmp, D_KV), lambda b, i: (b, 0)),
                  kv_spec(0), kv_spec(1), kv_spec(0), kv_spec(1),
                  pl.BlockSpec((LANES, seq), lambda b, i: (0, 0))],
        out_specs=pl.BlockSpec((tq, D_NSA), lambda b, i: (b * nq + i, 0)),
        out_shape=jax.ShapeDtypeStruct((batch * seq, D_NSA), F32),
        scratch_shapes=[pltpu.VMEM((n_kt, 2 * LANES, tk), BF16), pltpu.VMEM((n_kt, 2 * LANES, tk), BF16),
                        pltpu.VMEM((n_kt, LANES, tk), BF16), pltpu.VMEM((n_kt, LANES, tk), BF16)],
        compiler_params=_params("arbitrary", "arbitrary"),
        name="attn_prompt",
    )(qw, gl, kvc, ks_t, ks_t, kw_t, kw_t, ext)


def _attn_sample_kernel(pt_ref, *refs, n_pages, past_len, win_keep):
    del pt_ref
    page_refs = refs[:n_pages]
    (q_ref, gl_ref, ksn_ref, kwn_ref, kvc_ref, win_ref, exts_ref, extw_ref, o_ref) = refs[n_pages:]
    n_cmp = past_len // CMP_BLOCK
    cur = past_len // CMP_BLOCK
    t_hi = float(cur * CMP_BLOCK)
    t_lo = float(past_len % CMP_BLOCK)

    qrow = q_ref[0]
    q8f = jnp.concatenate([qrow[:, h * LANES:(h + 1) * LANES] for h in range(N_HEADS)], axis=0) * Q_SCALE
    q8 = q8f.astype(BF16)
    head = lax.broadcasted_iota(jnp.int32, (N_HEADS, 1), 0)
    slope = jnp.exp2(-(head + 1).astype(F32))
    lane8 = lax.broadcasted_iota(jnp.int32, (N_HEADS, LANES), 1)
    ali = _alibi_lanes(slope, t_hi, t_lo, (N_HEADS, LANES))

    kvc = kvc_ref[0]
    kc = kvc[:, :LANES]
    if n_cmp < LANES:
        kc = jnp.concatenate([kc, jnp.zeros((LANES - n_cmp, LANES), F32)], axis=0)
    blk = lax.broadcasted_iota(jnp.int32, (1, LANES), 1)
    dist_c = past_len - ((blk + 1) * CMP_BLOCK - 1)
    mask_c = (dist_c >= 0) & (blk < n_cmp)
    s_c = _nt_dot(q8, kc.astype(BF16)) - slope * dist_c.astype(F32)
    p_c = _masked_softmax(s_c, mask_c, 1)
    o_c = _dot(p_c[:, :n_cmp].astype(BF16), kvc[:, LANES:].astype(BF16))

    forced = (blk == cur) | (blk == 0)
    started = blk <= cur
    below = (lax.broadcasted_iota(jnp.int32, (LANES, LANES), 0)
             < lax.broadcasted_iota(jnp.int32, (LANES, LANES), 1))
    sel_rows = []
    for kv in range(N_KV):
        imp = jnp.sum(p_c[kv * GROUP:(kv + 1) * GROUP], axis=0, keepdims=True)
        score = jnp.where(forced, SEL_FORCED, jnp.where(started, imp, SEL_INVALID))
        col = _row_to_col(score)
        beats = (col > score) | ((col == score) & below)
        rank = jnp.sum(beats.astype(F32), axis=0, keepdims=True)
        sel = ((rank < N_SELECT) & (score > SEL_INVALID * 0.5)).astype(F32)
        sel_rows += [sel] * GROUP
    sel_bias = (jnp.concatenate(sel_rows, axis=0) - 1.0) * -NEG_INF
    qaug_s = jnp.concatenate([q8, jnp.where(lane8 < CMP_BLOCK, sel_bias, ali).astype(BF16)], axis=1)
    qaug_w = jnp.concatenate([q8, ali.astype(BF16)], axis=1)

    def new_key_score(row_ref):
        k_new = row_ref[0][:, :LANES].astype(BF16).astype(F32)
        return jnp.sum(q8.astype(F32) * k_new, axis=1, keepdims=True)

    def finish(scores, values, s_new, v_new):
        m = s_new
        for s in scores:
            m = jnp.maximum(m, jnp.max(s, axis=1, keepdims=True))
        p_new = jnp.exp(s_new - m)
        l = p_new
        acc = p_new * v_new.astype(BF16).astype(F32)
        for s, v in zip(scores, values):
            p = jnp.exp(s - m)
            l = l + jnp.sum(p, axis=1, keepdims=True)
            acc = acc + _nt_dot(p.astype(BF16), v)
        return acc / l

    scores, values = [], []
    for p, ref in enumerate(page_refs):
        page = ref[0, 0]
        kaug = jnp.concatenate([page[:LANES].astype(BF16), exts_ref[p]], axis=0)
        scores.append(_dot(qaug_s, kaug))
        values.append(page[LANES:].astype(BF16))
    o_s = finish(scores, values, new_key_score(ksn_ref), ksn_ref[0][:, LANES:])

    win = win_ref[0, 0]
    kaug = jnp.concatenate([win[:LANES].astype(BF16), extw_ref[...]], axis=0)
    s_w = _dot(qaug_w, kaug)
    dist_w = win_keep - lax.broadcasted_iota(jnp.int32, (1, win_keep), 1)
    s_w = jnp.where((dist_w >= 0) & (dist_w < WINDOW), s_w, NEG_INF)
    o_w = finish([s_w], [win[LANES:].astype(BF16)], new_key_score(kwn_ref), kwn_ref[0][:, LANES:])

    sig = _sigmoid(gl_ref[0])
    gate = [jnp.sum(jnp.where(lane8 == 3 * head + j, sig, 0.0), axis=1, keepdims=True) for j in range(3)]
    out = gate[0] * o_c + gate[1] * o_s + gate[2] * o_w
    out = jnp.where(head < GROUP, out, pltpu.roll(out, HEAD_DIM, axis=1))
    o_ref[0] = out[:, :HEAD_DIM]


def _attn_sample(cache_t, win_t, pt_flat, layer, pr, kvc, ext_s, ext_w, db, n_pages, win_keep):
    past_len = n_pages * PAGE_SIZE
    n_cmp = past_len // CMP_BLOCK
    row = lambda width: pl.BlockSpec((1, 1, width), lambda b, pt: (b, 0, 0))
    grid_spec = pltpu.PrefetchScalarGridSpec(
        num_scalar_prefetch=1,
        grid=(db,),
        in_specs=[_page_spec(layer, p, n_pages) for p in range(n_pages)]
        + [row(Q_WIDE), row(LANES), row(D_KV), row(D_KV),
           pl.BlockSpec((1, n_cmp, D_KV), lambda b, pt: (b, 0, 0)),
           pl.BlockSpec((1, 1, D_KV, win_keep), lambda b, pt: (layer, b, 0, 0)),
           pl.BlockSpec((n_pages, LANES, PAGE_SIZE), lambda b, pt: (0, 0, 0)),
           pl.BlockSpec((LANES, win_keep), lambda b, pt: (0, 0))],
        out_specs=pl.BlockSpec((1, N_HEADS, HEAD_DIM), lambda b, pt: (b, 0, 0)),
    )
    kern = functools.partial(_attn_sample_kernel, n_pages=n_pages, past_len=past_len, win_keep=win_keep)
    out = pl.pallas_call(
        kern,
        grid_spec=grid_spec,
        out_shape=jax.ShapeDtypeStruct((db, N_HEADS, HEAD_DIM), F32),
        compiler_params=_params("parallel"),
        name="attn_sample",
    )(pt_flat, *([cache_t] * n_pages), pr["q"].reshape(db, 1, Q_WIDE), pr["gl"].reshape(db, 1, LANES),
      pr["ks"].reshape(db, 1, D_KV), pr["kw"].reshape(db, 1, D_KV), kvc, win_t, ext_s, ext_w)
    return out.reshape(db, D_NSA)


def _win_update_kernel(win_ref, new_ref, o_ref, *, bb, win_keep):
    lane = lax.broadcasted_iota(jnp.int32, (D_KV, win_keep), 1)
    for k in range(bb):
        row = new_ref[0, k]
        col = jnp.concatenate([_row_to_col(row[:, h * LANES:(h + 1) * LANES]) for h in range(D_KV // LANES)],
                              axis=0)
        shifted = pltpu.roll(win_ref[0, k], win_keep - 1, axis=1)
        o_ref[0, k] = jnp.where(lane == win_keep - 1, col, shifted)


def _win_update(win_t, kw_new, bb):
    depth, db, _, win_keep = win_t.shape
    return pl.pallas_call(
        functools.partial(_win_update_kernel, bb=bb, win_keep=win_keep),
        grid=(depth, db // bb),
        in_specs=[pl.BlockSpec((1, bb, D_KV, win_keep), lambda l, b: (l, b, 0, 0)),
                  pl.BlockSpec((1, bb, 1, D_KV), lambda l, b: (l, b, 0, 0))],
        out_specs=pl.BlockSpec((1, bb, D_KV, win_keep), lambda l, b: (l, b, 0, 0)),
        out_shape=jax.ShapeDtypeStruct(win_t.shape, F32),
        compiler_params=_params("parallel", "parallel"),
        name="win_update",
    )(win_t, kw_new)


def _pool_window_lane(shape):
    lane = lax.broadcasted_iota(jnp.int32, shape, len(shape) - 1)
    w = jnp.full(shape, POOL_WINDOWS[0], jnp.int32)
    for gi in range(1, len(POOL_WINDOWS)):
        w = jnp.where(lane >= gi * POOL_GROUP_DIM, POOL_WINDOWS[gi], w)
    return w


def _group_rms(v, ones_bd):
    sq = v * v
    hi = sq.astype(BF16)
    lo = (sq - hi.astype(F32)).astype(BF16)
    return (_dot(hi, ones_bd) + _dot(lo, ones_bd)) * (1.0 / GMLP_GROUP_DIM)


def _mix_out(x, ya, za, yb, zb, yc, zc, wo_ref, gpost):
    out = _dot((ya * _silu(za)).astype(BF16), wo_ref[0:D_NSA, :])
    out += _dot((yb * _silu(zb)).astype(BF16), wo_ref[D_NSA:D_NSA + D_POOL, :])
    out += _dot((yc * _silu(zc)).astype(BF16), wo_ref[D_NSA + D_POOL:, :])
    ms = jnp.mean(out * out, axis=-1, keepdims=True)
    return x + out * lax.rsqrt(ms + EPS) * gpost


def _merge_prompt_kernel(x_ref, ya_ref, za_ref, pin_ref, prev_ref, zb_ref, u_ref, v_ref, zc_ref,
                         pw_ref, ps_ref, gn_ref, ones_ref, ws_ref, bs_ref, wo_ref, gp_ref,
                         xo_ref, vn_ref, xext, *, tm, tiles_per_seq):
    i = pl.program_id(0)
    first = (i % tiles_per_seq) == 0
    pin = pin_ref[...]
    xext[POOL_MAX:, :] = pin
    xext[0:POOL_MAX, :] = jnp.where(first, 0.0, prev_ref[...])
    w_lane = _pool_window_lane((tm, D_POOL))
    acc = pin
    for k in range(1, POOL_MAX):
        acc = acc + jnp.where(w_lane > k, xext[pl.ds(POOL_MAX - k, tm), :], 0.0)
    pos = (i % tiles_per_seq) * tm + lax.broadcasted_iota(jnp.int32, (tm, D_POOL), 0)
    cnt = jnp.minimum(w_lane, pos + 1).astype(F32)
    diff = acc / cnt - pin
    yb = _dot(diff.astype(BF16), pw_ref[...]) * ps_ref[...]

    v = v_ref[...]
    vn = v * lax.rsqrt(_group_rms(v, ones_ref[...]) + EPS) * gn_ref[...]
    vn_ref[...] = vn
    vnb = vn.astype(BF16)
    lane = lax.broadcasted_iota(jnp.int32, (CHUNK, D_GMLP), 1)
    chunks = []
    for c in range(tm // CHUNK):
        vc = vnb[c * CHUNK:(c + 1) * CHUNK]
        s = _dot(ws_ref[0], vc)
        for g in range(1, D_GMLP // GMLP_GROUP_DIM):
            s = jnp.where(lane >= g * GMLP_GROUP_DIM, _dot(ws_ref[g], vc), s)
        chunks.append(s + bs_ref[...])
    yc = u_ref[...] * jnp.concatenate(chunks, axis=0)

    xo_ref[...] = _mix_out(x_ref[...], ya_ref[...], za_ref[...], yb, zb_ref[...], yc, zc_ref[...],
                           wo_ref, gp_ref[...])


def _merge_prompt(x2, ya, pr, wts, seq, tm):
    m = x2.shape[0]
    tiles_per_seq = seq // tm
    per = tm // POOL_MAX
    rows = lambda width: pl.BlockSpec((tm, width), lambda i: (i, 0))
    const = lambda shape: pl.BlockSpec(shape, lambda i: (0,) * len(shape))
    kern = functools.partial(_merge_prompt_kernel, tm=tm, tiles_per_seq=tiles_per_seq)
    return pl.pallas_call(
        kern,
        grid=(m // tm,),
        in_specs=[rows(D_MODEL), rows(D_NSA), rows(D_NSA), rows(D_POOL),
                  pl.BlockSpec((POOL_MAX, D_POOL), lambda i: (jnp.maximum(i * per - 1, 0), 0)),
                  rows(D_POOL), rows(D_GMLP), rows(D_GMLP), rows(D_GMLP),
                  const((D_POOL, D_POOL)), const((1, D_POOL)), const((1, D_GMLP)),
                  const((D_GMLP, D_GMLP)), const((4, CHUNK, CHUNK)), const((CHUNK, D_GMLP)),
                  const((D_MODEL, D_MODEL)), const((1, D_MODEL))],
        out_specs=[rows(D_MODEL), rows(D_GMLP)],
        out_shape=[jax.ShapeDtypeStruct((m, D_MODEL), F32), jax.ShapeDtypeStruct((m, D_GMLP), F32)],
        scratch_shapes=[pltpu.VMEM((tm + POOL_MAX, D_POOL), F32)],
        compiler_params=_params("parallel"),
        name="merge_prompt",
    )(x2, ya, pr["za"], pr["pin"], pr["pin"], pr["zb"], pr["u"], pr["v"], pr["zc"],
      wts["pool_w"], wts["pool_scale"], wts["gmlp_norm"], wts["ones_bd"], wts["ws_tril"], wts["bs_full"],
      wts["w_out"], wts["norm_post"])


def _merge_sample_kernel(x_ref, ya_ref, za_ref, pin_ref, st_ref, zb_ref, u_ref, v_ref, zc_ref,
                         pw_ref, ps_ref, gn_ref, ones_ref, w0_ref, b0_ref, wo_ref, gp_ref,
                         xo_ref, vn_ref):
    pin = pin_ref[...]
    w_lane = _pool_window_lane(pin.shape)
    acc = pin
    for k in range(1, POOL_MAX):
        acc = acc + jnp.where(w_lane > k, st_ref[POOL_MAX - 1 - k], 0.0)
    diff = acc / w_lane.astype(F32) - pin
    yb = _dot(diff.astype(BF16), pw_ref[...]) * ps_ref[...]

    v = v_ref[...]
    vn = v * lax.rsqrt(_group_rms(v, ones_ref[...]) + EPS) * gn_ref[...]
    vn_ref[...] = vn
    yc = u_ref[...] * (w0_ref[...] * vn + b0_ref[...])

    xo_ref[...] = _mix_out(x_ref[...], ya_ref[...], za_ref[...], yb, zb_ref[...], yc, zc_ref[...],
                           wo_ref, gp_ref[...])


def _merge_sample(x2, ya, pr, state_t, wts):
    db = x2.shape[0]
    full = lambda shape: pl.BlockSpec(shape, lambda i: (0,) * len(shape))
    args = (x2, ya, pr["za"], pr["pin"], state_t, pr["zb"], pr["u"], pr["v"], pr["zc"],
            wts["pool_w"], wts["pool_scale"], wts["gmlp_norm"], wts["ones_bd"], wts["w0"], wts["b0"],
            wts["w_out"], wts["norm_post"])
    return pl.pallas_call(
        _merge_sample_kernel,
        grid=(1,),
        in_specs=[full(a.shape) for a in args],
        out_specs=[full((db, D_MODEL)), full((db, D_GMLP))],
        out_shape=[jax.ShapeDtypeStruct((db, D_MODEL), F32), jax.ShapeDtypeStruct((db, D_GMLP), F32)],
        compiler_params=_params("arbitrary"),
        name="merge_sample",
    )(*args)


def _layer_weights(l, norm_pre, w_in, cmp_pe, cmp_w1, cmp_w2, pool_w, pool_scale, gmlp_norm, gmlp_ws, gmlp_bs,
                   w_out, norm_post):
    n_g = D_GMLP // GMLP_GROUP_DIM
    ones_bd = jnp.kron(jnp.eye(n_g, dtype=F32), jnp.ones((GMLP_GROUP_DIM, GMLP_GROUP_DIM), F32)).astype(BF16)
    pw = pool_w[l]
    zp = jnp.zeros_like(pw[0])
    pool_bd = jnp.concatenate(
        [jnp.concatenate([pw[i] if j == i else zp for j in range(4)], axis=1) for i in range(4)], axis=0)
    tril = jnp.tril(jnp.ones((CHUNK, CHUNK), F32))
    return dict(
        norm_pre=norm_pre[l].reshape(1, D_MODEL),
        w_in=_prep_w_in(w_in[l]),
        pe_t=_pe_t(cmp_pe[l]),
        w1_t=_blockdiag4(cmp_w1[l]).T.astype(BF16),
        w2=_blockdiag4(cmp_w2[l]).astype(BF16),
        pool_w=pool_bd.astype(BF16),
        pool_scale=pool_scale[l].reshape(1, D_POOL),
        gmlp_norm=gmlp_norm[l].reshape(1, D_GMLP),
        ones_bd=ones_bd,
        ws_tril=(gmlp_ws[l] * tril).astype(BF16),
        bs_full=jnp.repeat(gmlp_bs[l].T, GMLP_GROUP_DIM, axis=1),
        w0=jnp.repeat(gmlp_ws[l][:, 0, 0], GMLP_GROUP_DIM).reshape(1, D_GMLP),
        b0=jnp.repeat(gmlp_bs[l][:, 0], GMLP_GROUP_DIM).reshape(1, D_GMLP),
        w_out=w_out[l].astype(BF16),
        norm_post=norm_post[l].reshape(1, D_MODEL),
    )


def _largest_tile(n, cap):
    t = cap
    while n % t:
        t //= 2
    return t


def _feature_major(cache):
    d, n, t = cache.shape[:3]
    return jnp.transpose(cache, (0, 1, 3, 4, 5, 2)).reshape(d, n, D_KV, t)


def _token_major(x_t):
    d, n, _, t = x_t.shape
    return jnp.transpose(x_t.reshape(d, n, 2, N_KV, HEAD_DIM, t), (0, 1, 5, 2, 3, 4))


def kernel(x_prompt, x_sample, cache_kv_cmp, cache_kv_sel, cache_kv_win, state_pool, page_table, norm_pre, w_in,
           cmp_pe, cmp_w1, cmp_w2, pool_w, pool_scale, gmlp_norm, gmlp_ws, gmlp_bs, w_out, norm_post):
    bp, tp, _ = x_prompt.shape
    db, ts, _ = x_sample.shape
    depth = w_in.shape[0]
    n_pages = page_table.shape[1]
    past_len = n_pages * PAGE_SIZE
    win_keep = cache_kv_win.shape[2]
    assert ts == 1 and tp % CHUNK == 0 and tp <= CMP_BLOCK * CMP_BLOCK and tp >= WINDOW
    assert past_len // CMP_BLOCK + 1 <= CMP_BLOCK and win_keep == WINDOW and db % LANES == 0

    mp = bp * tp
    tm_proj = _largest_tile(tp, 512)
    tile_cmp = _largest_tile(tp, 1024)
    tm_merge = _largest_tile(tp, 512)
    tq, tk = 128, 256

    cmp_t = _feature_major(cache_kv_cmp)
    sel_t = _feature_major(cache_kv_sel)
    win_t = _feature_major(cache_kv_win)
    pt_flat = page_table.reshape(-1).astype(jnp.int32)

    ext_prompt = _extra_rows(jnp.arange(tp, dtype=jnp.int32), True)
    ext_s = jnp.transpose(
        _extra_rows(jnp.arange(past_len, dtype=jnp.int32), True).reshape(LANES, n_pages, PAGE_SIZE), (1, 0, 2))
    ext_w = _extra_rows(past_len - win_keep + jnp.arange(win_keep, dtype=jnp.int32), False)

    xp = x_prompt.reshape(mp, D_MODEL)
    xs = x_sample.reshape(db, D_MODEL)
    names = ("kvc_p", "kvc_s", "kvs_p", "kvs_s", "kvw_p", "kw_new", "pool_p", "pool_s", "gv_p", "gv_s")
    outs = {k: [] for k in names}
    for l in range(depth):
        wts = _layer_weights(l, norm_pre, w_in, cmp_pe, cmp_w1, cmp_w2, pool_w, pool_scale, gmlp_norm, gmlp_ws,
                             gmlp_bs, w_out, norm_post)
        pr = _project(xp, wts["norm_pre"], wts["w_in"], bp, tp, tm_proj, False)
        kvc = _compress_prompt(pr["kc_t"], wts, tile_cmp)
        ya = _attn_prompt(pr["q"], pr["gl"], kvc, pr["ks_t"], pr["kw_t"], ext_prompt, bp, tp, tq, tk)
        xp, vn = _merge_prompt(xp, ya, pr, wts, tp, tm_merge)
        outs["kvc_p"].append(pr["kc_t"])
        outs["kvs_p"].append(pr["ks_t"])
        outs["kvw_p"].append(pr["kw_t"][:, :, tp - WINDOW:])
        outs["pool_p"].append(pr["pin"].reshape(bp, tp, D_POOL)[:, tp - (POOL_MAX - 1):])
        outs["gv_p"].append(vn.reshape(bp, tp, D_GMLP)[:, tp - CHUNK:])
        pr = _project(xs, wts["norm_pre"], wts["w_in"], 1, db, db, True)
        kvc = _compress_sample(cmp_t, pt_flat, l, wts, db, n_pages)
        ya = _attn_sample(sel_t, win_t, pt_flat, l, pr, kvc, ext_s, ext_w, db, n_pages, win_keep)
        state_t = jnp.swapaxes(state_pool[l], 0, 1)
        xs, vn = _merge_sample(xs, ya, pr, state_t, wts)
        outs["kvc_s"].append(pr["kc_t"])
        outs["kvs_s"].append(pr["ks_t"])
        outs["kw_new"].append(pr["kw"].reshape(db, 1, D_KV))
        outs["pool_s"].append(jnp.concatenate([state_t[1:], pr["pin"][None]], axis=0))
        outs["gv_s"].append(vn.reshape(db, 1, D_GMLP))
    st = {k: jnp.stack(v) for k, v in outs.items()}
    kvw_s = _win_update(win_t, st["kw_new"], 4)
    sample_kv = lambda x_t: jnp.transpose(_token_major(x_t), (0, 2, 1, 3, 4, 5))
    return (xp.reshape(bp, tp, D_MODEL), xs.reshape(db, ts, D_MODEL),
            _token_major(st["kvc_p"]), sample_kv(st["kvc_s"]),
            _token_major(st["kvs_p"]), sample_kv(st["kvs_s"]),
            _token_major(st["kvw_p"]), _token_major(kvw_s),
            st["pool_p"], jnp.swapaxes(st["pool_s"], 1, 2), st["gv_p"], st["gv_s"])
```

```python
import functools

import jax
import jax.numpy as jnp
from jax import lax
from jax.experimental import pallas as pl
from jax.experimental.pallas import tpu as pltpu

F32 = jnp.float32
BF16 = jnp.bfloat16

D_MODEL = 1024
HEAD_DIM = 64
N_HEADS = 8
N_KV = 2
GROUP = N_HEADS // N_KV
D_NSA = N_HEADS * HEAD_DIM
D_KV = 2 * N_KV * HEAD_DIM
CMP_BLOCK = 64
N_SELECT = 16
WINDOW = 512
D_POOL = 256
POOL_WINDOWS = (2, 4, 8, 16)
POOL_GROUP_DIM = 64
POOL_MAX = 16
D_GMLP = 256
GMLP_GROUP_DIM = 64
CHUNK = 128
PAGE_SIZE = 128
EPS = 1e-6
NEG_INF = -1e30
SEL_FORCED = 1e4
SEL_INVALID = -1e4
Q_SCALE = HEAD_DIM ** -0.5

WIN_BLOCK = 128
LANES = 128
Q_WIDE = N_HEADS * LANES
VMEM_LIMIT = 56 * 1024 * 1024

ROW_ONE_A = 64
ROW_ONE_B = 65
ROW_POS_HI = 66
ROW_POS_LO = 67

_SEG = dict(q=(0, 512), kc=(512, 256), ks=(768, 256), kw=(1024, 256), gl=(1280, 24), za=(1304, 512),
            pin=(1816, 256), zb=(2072, 256), u=(2328, 256), v=(2584, 256), zc=(2840, 256))
_PROJ_OUT = (("q", Q_WIDE), ("kc", 256), ("ks", 256), ("kw", 256), ("za", 512), ("pin", 256),
             ("zb", 256), ("u", 256), ("v", 256), ("zc", 256), ("gl", LANES))
_KV_NAMES = ("kc", "ks", "kw")


def _nt_dot(a, b):
    return lax.dot_general(a, b, (((1,), (1,)), ((), ())), preferred_element_type=F32)


def _tn_dot(a, b):
    return lax.dot_general(a, b, (((0,), (0,)), ((), ())), preferred_element_type=F32)


def _dot(a, b):
    return jnp.dot(a, b, preferred_element_type=F32)


def _sigmoid(x):
    return 1.0 / (1.0 + jnp.exp(-x))


def _silu(x):
    return x * _sigmoid(x)


def _params(*sem):
    return pltpu.CompilerParams(dimension_semantics=sem, vmem_limit_bytes=VMEM_LIMIT)


def _row_to_col(row):
    eye = (lax.broadcasted_iota(jnp.int32, (LANES, LANES), 0)
           == lax.broadcasted_iota(jnp.int32, (LANES, LANES), 1))
    return jnp.sum(jnp.where(eye, row, 0.0), axis=1, keepdims=True)


def _proj_kernel(x_ref, g_ref, w_ref, *out_refs, emit_rows):
    x = x_ref[...]
    ms = jnp.mean(x * x, axis=-1, keepdims=True)
    xn = (x * lax.rsqrt(ms + EPS) * g_ref[...]).astype(BF16)
    refs = iter(out_refs)
    off = 0
    for name, width in _PROJ_OUT:
        res = _dot(xn, w_ref[:, off:off + width])
        off += width
        if name in _KV_NAMES:
            next(refs)[0] = res.T
            if not emit_rows:
                continue
        next(refs)[...] = res


def _project(x2, g, w, batch, seq, tm, emit_rows):
    m = x2.shape[0]
    n_tot = w.shape[1]
    tiles = seq // tm
    names, shapes, specs = [], [], []
    for name, width in _PROJ_OUT:
        if name in _KV_NAMES:
            names.append(name + "_t")
            shapes.append(jax.ShapeDtypeStruct((batch, width, seq), F32))
            specs.append(pl.BlockSpec((1, width, tm), lambda i: (i // tiles, 0, i % tiles)))
            if not emit_rows:
                continue
        names.append(name)
        shapes.append(jax.ShapeDtypeStruct((m, width), F32))
        specs.append(pl.BlockSpec((tm, width), lambda i: (i, 0)))
    res = pl.pallas_call(
        functools.partial(_proj_kernel, emit_rows=emit_rows),
        grid=(m // tm,),
        in_specs=[pl.BlockSpec((tm, D_MODEL), lambda i: (i, 0)),
                  pl.BlockSpec((1, D_MODEL), lambda i: (0, 0)),
                  pl.BlockSpec((D_MODEL, n_tot), lambda i: (0, 0))],
        out_specs=specs,
        out_shape=shapes,
        compiler_params=_params("parallel"),
        name="norm_project",
    )(x2, g, w)
    return dict(zip(names, res))


def _prep_w_in(w):
    def seg(name):
        o, n = _SEG[name]
        return w[:, o:o + n]
    wq = seg("q").reshape(D_MODEL, N_HEADS, HEAD_DIM)
    zeros = jnp.zeros_like(wq)
    lo = jnp.concatenate([wq, zeros], axis=-1)
    hi = jnp.concatenate([zeros, wq], axis=-1)
    first = (jnp.arange(N_HEADS) < GROUP)[None, :, None]
    q_wide = jnp.where(first, lo, hi).reshape(D_MODEL, Q_WIDE)
    gl = jnp.pad(seg("gl"), ((0, 0), (0, LANES - 3 * N_HEADS)))
    cols = [q_wide] + [seg(n) for n, _ in _PROJ_OUT[1:-1]] + [gl]
    return jnp.concatenate(cols, axis=1).astype(BF16)


def _blockdiag4(w):
    z = jnp.zeros((HEAD_DIM, HEAD_DIM), w.dtype)
    blocks = [w[0], w[0], w[1], w[1]]
    rows = [jnp.concatenate([blocks[i] if j == i else z for j in range(4)], axis=1) for i in range(4)]
    return jnp.concatenate(rows, axis=0)


def _pe_t(pe):
    one = jnp.concatenate([pe[0].T, pe[0].T, pe[1].T, pe[1].T], axis=0)
    return jnp.concatenate([one, one], axis=1)


def _block_avg(n_tok):
    t = jnp.arange(n_tok)[:, None] // CMP_BLOCK
    return (t == jnp.arange(LANES)[None, :]).astype(BF16)


def _compress_t(xt, pe128, w1t, w2, avg):
    n = xt.shape[1]
    pet = jnp.concatenate([pe128] * (n // LANES), axis=1)
    hid = _silu(_dot(w1t, (xt + pet).astype(BF16)))
    hi = hid.astype(BF16)
    lo = (hid - hi.astype(F32)).astype(BF16)
    mean_t = (_dot(hi, avg) + _dot(lo, avg)) * (1.0 / CMP_BLOCK)
    return _tn_dot(mean_t.astype(BF16), w2)


def _compress_kernel(x_ref, pe_ref, w1_ref, w2_ref, avg_ref, o_ref):
    res = _compress_t(x_ref[0], pe_ref[...], w1_ref[...], w2_ref[...], avg_ref[...])
    o_ref[...] = res[:o_ref.shape[0]]


def _compress_prompt(kc_t, wts, tile):
    batch, _, seq = kc_t.shape
    tiles = seq // tile
    const = lambda shape: pl.BlockSpec(shape, lambda b, j: (0,) * len(shape))
    return pl.pallas_call(
        _compress_kernel,
        grid=(batch, tiles),
        in_specs=[pl.BlockSpec((1, D_KV, tile), lambda b, j: (b, 0, j)),
                  const((D_KV, LANES)), const((D_KV, D_KV)), const((D_KV, D_KV)), const((tile, LANES))],
        out_specs=pl.BlockSpec((tile // CMP_BLOCK, D_KV), lambda b, j: (b * tiles + j, 0)),
        out_shape=jax.ShapeDtypeStruct((batch * seq // CMP_BLOCK, D_KV), F32),
        compiler_params=_params("parallel", "parallel"),
        name="compress_prompt",
    )(kc_t, wts["pe_t"], wts["w1_t"], wts["w2"], _block_avg(tile))


def _compress_pages_kernel(pt_ref, *refs, n_pages):
    del pt_ref
    page_refs = refs[:n_pages]
    pe_ref, w1_ref, w2_ref, avg_ref, o_ref = refs[n_pages:]
    xt = jnp.concatenate([r[0, 0] for r in page_refs], axis=1)
    res = _compress_t(xt, pe_ref[...], w1_ref[...], w2_ref[...], avg_ref[...])
    o_ref[0] = res[:o_ref.shape[1]]


def _page_spec(layer, p, n_pages):
    return pl.BlockSpec((1, 1, D_KV, PAGE_SIZE), lambda b, pt: (layer, pt[b * n_pages + p], 0, 0))


def _compress_sample(cache_t, pt_flat, layer, wts, db, n_pages):
    past_len = n_pages * PAGE_SIZE
    n_cmp = past_len // CMP_BLOCK
    const = lambda shape: pl.BlockSpec(shape, lambda b, pt: (0,) * len(shape))
    grid_spec = pltpu.PrefetchScalarGridSpec(
        num_scalar_prefetch=1,
        grid=(db,),
        in_specs=[_page_spec(layer, p, n_pages) for p in range(n_pages)]
        + [const((D_KV, LANES)), const((D_KV, D_KV)), const((D_KV, D_KV)), const((past_len, LANES))],
        out_specs=pl.BlockSpec((1, n_cmp, D_KV), lambda b, pt: (b, 0, 0)),
    )
    return pl.pallas_call(
        functools.partial(_compress_pages_kernel, n_pages=n_pages),
        grid_spec=grid_spec,
        out_shape=jax.ShapeDtypeStruct((db, n_cmp, D_KV), F32),
        compiler_params=_params("parallel"),
        name="compress_sample",
    )(pt_flat, *([cache_t] * n_pages), wts["pe_t"], wts["w1_t"], wts["w2"], _block_avg(past_len))


def _extra_rows(pos, with_onehot):
    row = jnp.arange(LANES)[:, None]
    blk = (pos // CMP_BLOCK)[None, :]
    hi = ((pos // CMP_BLOCK) * CMP_BLOCK).astype(F32)[None, :]
    lo = (pos % CMP_BLOCK).astype(F32)[None, :]
    out = jnp.where((row == ROW_ONE_A) | (row == ROW_ONE_B), 1.0, 0.0) + jnp.zeros_like(hi)
    out = jnp.where(row == ROW_POS_HI, hi, out)
    out = jnp.where(row == ROW_POS_LO, lo, out)
    if with_onehot:
        out = jnp.where((row < CMP_BLOCK) & (row == blk), 1.0, out)
    return out.astype(BF16)


def _alibi_lanes(slope, t_hi, t_lo, shape):
    lane = lax.broadcasted_iota(jnp.int32, shape, 1)
    out = jnp.where(lane == ROW_ONE_A, -slope * t_hi, 0.0)
    out = jnp.where(lane == ROW_ONE_B, -slope * t_lo, out)
    return jnp.where((lane == ROW_POS_HI) | (lane == ROW_POS_LO), slope, out)


def _masked_softmax(s, mask, axis):
    s = jnp.where(mask, s, NEG_INF)
    p = jnp.exp(s - jnp.max(s, axis=axis, keepdims=True)) * mask.astype(F32)
    return p / jnp.maximum(jnp.sum(p, axis=axis, keepdims=True), 1e-30)


def _attn_prompt_kernel(q_ref, gl_ref, kvc_ref, ksk_ref, ksv_ref, kwk_ref, kwv_ref, ext_ref, o_ref,
                        kaug_s, kaug_w, v_s, v_w, *, seq, tq, tk):
    i = pl.program_id(1)
    m = GROUP * tq
    n_cmp = seq // CMP_BLOCK

    n_wblk = WINDOW // WIN_BLOCK + tq // WIN_BLOCK

    @pl.when(i == 0)
    def _fill():
        for c in range(seq // tk):
            cols = slice(c * tk, (c + 1) * tk)
            kaug_s[c] = jnp.concatenate([ksk_ref[0, :, cols].T.astype(BF16), ext_ref[cols, :]], axis=1)
            v_s[c] = ksv_ref[0, :, cols].astype(BF16)
        for c in range(seq // WIN_BLOCK):
            cols = slice(c * WIN_BLOCK, (c + 1) * WIN_BLOCK)
            kaug_w[c] = jnp.concatenate([kwk_ref[0, :, cols].T.astype(BF16), ext_ref[cols, :]], axis=1)
            v_w[c] = kwv_ref[0, :, cols].astype(BF16)

    q0 = i * tq
    lane_t = q0 + lax.broadcasted_iota(jnp.int32, (1, m), 1) % tq
    t_hi = ((lane_t // CMP_BLOCK) * CMP_BLOCK).astype(F32)
    t_lo = (lane_t % CMP_BLOCK).astype(F32)
    ext_row = lax.broadcasted_iota(jnp.int32, (LANES, m), 0)
    key_row = lax.broadcasted_iota(jnp.int32, (tk, 1), 0)
    sig_t = _sigmoid(gl_ref[...]).T

    blk_row = lax.broadcasted_iota(jnp.int32, (n_cmp, m), 0)
    dist_c = lane_t - ((blk_row + 1) * CMP_BLOCK - 1)
    mask_c = dist_c >= 0
    blk2 = lax.broadcasted_iota(jnp.int32, (n_cmp, tq), 0)
    cur = (q0 + lax.broadcasted_iota(jnp.int32, (1, tq), 1)) // CMP_BLOCK
    forced = (blk2 == cur) | (blk2 == 0)
    started = blk2 <= cur

    def flash_step(qaugs, k_tile, v_tile, carries, mask):
        out = []
        scores = [_dot(k_tile, qaug_t) for qaug_t in qaugs]
        for s, (m_i, l_i, acc) in zip(scores, carries):
            if mask is not None:
                s = jnp.where(mask, s, NEG_INF)
            m_new = jnp.maximum(m_i, jnp.max(s, axis=0, keepdims=True))
            alpha = jnp.exp(m_i - m_new)
            p = jnp.exp(s - m_new)
            l_new = alpha * l_i + jnp.sum(p, axis=0, keepdims=True)
            out.append((m_new, l_new, alpha * acc + _dot(v_tile, p.astype(BF16))))
        return tuple(out)

    init = (jnp.full((1, m), NEG_INF, F32), jnp.zeros((1, m), F32), jnp.zeros((LANES, m), F32))
    n_chunk = n_cmp // 8
    o_cs, qaugs_s, qaugs_w = [], [], []
    for kv in range(N_KV):
        qk = q_ref[:, kv * GROUP * LANES:(kv + 1) * GROUP * LANES]
        q4_t = jnp.concatenate([(qk[:, g * LANES:(g + 1) * LANES] * Q_SCALE).T for g in range(GROUP)],
                               axis=1).astype(BF16)
        slopes = [2.0 ** -(kv * GROUP + g + 1) for g in range(GROUP)]
        slope = jnp.concatenate([jnp.full((1, tq), s, F32) for s in slopes], axis=1)
        ali_t = jnp.where(ext_row == ROW_ONE_A, -slope * t_hi, 0.0)
        ali_t = jnp.where(ext_row == ROW_ONE_B, -slope * t_lo, ali_t)
        ali_t = jnp.where((ext_row == ROW_POS_HI) | (ext_row == ROW_POS_LO), slope, ali_t)

        kvc = kvc_ref[...]
        s_ct = _dot(kvc[:, :LANES].astype(BF16), q4_t) - slope * dist_c.astype(F32)
        p_ct = _masked_softmax(s_ct, mask_c, 0)
        o_c = _tn_dot(kvc[:, LANES:].astype(BF16), p_ct.astype(BF16))

        imp = p_ct[:, 0:tq]
        for g in range(1, GROUP):
            imp = imp + p_ct[:, g * tq:(g + 1) * tq]
        score = jnp.where(forced, SEL_FORCED, jnp.where(started, imp, SEL_INVALID))
        chunks = [score[8 * c:8 * c + 8] for c in range(n_chunk)]
        blk8 = lax.broadcasted_iota(jnp.int32, (8, tq), 0)
        ranks = [jnp.zeros((8, tq), F32) for _ in range(n_chunk)]
        for b in range(n_cmp):
            row = score[b:b + 1, :]
            for c in range(n_chunk):
                if 8 * c > b:
                    beats = row >= chunks[c]
                elif 8 * c + 7 < b:
                    beats = row > chunks[c]
                else:
                    beats = (row > chunks[c]) | ((row == chunks[c]) & (blk8 > b - 8 * c))
                ranks[c] = ranks[c] + jnp.where(beats, 1.0, 0.0)
        rank = jnp.concatenate(ranks, axis=0)
        sel_t = ((rank < N_SELECT) & (score > SEL_INVALID * 0.5)).astype(F32)
        sel_bias = (sel_t - 1.0) * -NEG_INF
        if n_cmp < LANES:
            sel_bias = jnp.concatenate([sel_bias, jnp.zeros((LANES - n_cmp, tq), F32)], axis=0)
        sel_bias = jnp.concatenate([sel_bias] * GROUP, axis=1)

        o_cs.append(o_c)
        qaugs_s.append(
            jnp.concatenate([q4_t, jnp.where(ext_row < CMP_BLOCK, sel_bias, ali_t).astype(BF16)], axis=0))
        qaugs_w.append(jnp.concatenate([q4_t, ali_t.astype(BF16)], axis=0))

    n_kt = (q0 + tq + tk - 1) // tk
    carries = lax.fori_loop(
        0, n_kt - 1, lambda j, c: flash_step(qaugs_s, kaug_s[j], v_s[j], c, None), (init,) * N_KV)
    diag = (n_kt - 1) * tk + key_row <= lane_t
    carries = flash_step(qaugs_s, kaug_s[n_kt - 1], v_s[n_kt - 1], carries, diag)
    o_ss = [acc / l for _, l, acc in carries]

    c0 = jnp.maximum(q0 // WIN_BLOCK - WINDOW // WIN_BLOCK, 0)
    k_win = jnp.concatenate([kaug_w[c0 + t] for t in range(n_wblk)], axis=0)
    v_win = jnp.concatenate([v_w[c0 + t] for t in range(n_wblk)], axis=1)
    dist_w = lane_t - (c0 * WIN_BLOCK + lax.broadcasted_iota(jnp.int32, (n_wblk * WIN_BLOCK, 1), 0))
    carries = flash_step(qaugs_w, k_win, v_win, (init,) * N_KV, (dist_w >= 0) & (dist_w < WINDOW))
    o_ws = [acc / l for _, l, acc in carries]

    heads = []
    for kv in range(N_KV):
        rows = slice(kv * HEAD_DIM, (kv + 1) * HEAD_DIM)
        for g in range(GROUP):
            h = kv * GROUP + g
            cols = slice(g * tq, (g + 1) * tq)
            heads.append(sig_t[3 * h:3 * h + 1] * o_cs[kv][rows, cols]
                         + sig_t[3 * h + 1:3 * h + 2] * o_ss[kv][rows, cols]
                         + sig_t[3 * h + 2:3 * h + 3] * o_ws[kv][rows, cols])
    o_ref[...] = jnp.concatenate(heads, axis=0).T


def _attn_prompt(qw, gl, kvc, ks_t, kw_t, ext, batch, seq, tq, tk):
    nq = seq // tq
    n_cmp = seq // CMP_BLOCK
    n_kt = seq // tk
    kern = functools.partial(_attn_prompt_kernel, seq=seq, tq=tq, tk=tk)
    kv_spec = lambda half: pl.BlockSpec((1, LANES, seq), lambda b, i: (b, half, 0))
    return pl.pallas_call(
        kern,
        grid=(batch, nq),
        in_specs=[pl.BlockSpec((tq, Q_WIDE), lambda b, i: (b * nq + i, 0)),
                  pl.BlockSpec((tq, LANES), lambda b, i: (b * nq + i, 0)),
                  pl.BlockSpec((n_cmp, D_KV), lambda b, i: (b, 0)),
                  kv_spec(0), kv_spec(1), kv_spec(0), kv_spec(1),
                  pl.BlockSpec((seq, LANES), lambda b, i: (0, 0))],
        out_specs=pl.BlockSpec((tq, D_NSA), lambda b, i: (b * nq + i, 0)),
        out_shape=jax.ShapeDtypeStruct((batch * seq, D_NSA), F32),
        scratch_shapes=[pltpu.VMEM((n_kt, tk, 2 * LANES), BF16),
                        pltpu.VMEM((seq // WIN_BLOCK, WIN_BLOCK, 2 * LANES), BF16),
                        pltpu.VMEM((n_kt, LANES, tk), BF16),
                        pltpu.VMEM((seq // WIN_BLOCK, LANES, WIN_BLOCK), BF16)],
        compiler_params=_params("arbitrary", "arbitrary"),
        name="attn_prompt",
    )(qw, gl, kvc, ks_t, ks_t, kw_t, kw_t, ext)


def _attn_sample_kernel(pt_ref, *refs, n_pages, past_len, win_keep):
    del pt_ref
    page_refs = refs[:n_pages]
    (q_ref, gl_ref, ksn_ref, kwn_ref, kvc_ref, win_ref, exts_ref, extw_ref, o_ref) = refs[n_pages:]
    n_cmp = past_len // CMP_BLOCK
    cur = past_len // CMP_BLOCK
    t_hi = float(cur * CMP_BLOCK)
    t_lo = float(past_len % CMP_BLOCK)

    qrow = q_ref[0]
    q8f = jnp.concatenate([qrow[:, h * LANES:(h + 1) * LANES] for h in range(N_HEADS)], axis=0) * Q_SCALE
    q8 = q8f.astype(BF16)
    head = lax.broadcasted_iota(jnp.int32, (N_HEADS, 1), 0)
    slope = jnp.exp2(-(head + 1).astype(F32))
    lane8 = lax.broadcasted_iota(jnp.int32, (N_HEADS, LANES), 1)
    ali = _alibi_lanes(slope, t_hi, t_lo, (N_HEADS, LANES))

    kvc = kvc_ref[0]
    kc = kvc[:, :LANES]
    if n_cmp < LANES:
        kc = jnp.concatenate([kc, jnp.zeros((LANES - n_cmp, LANES), F32)], axis=0)
    blk = lax.broadcasted_iota(jnp.int32, (1, LANES), 1)
    dist_c = past_len - ((blk + 1) * CMP_BLOCK - 1)
    mask_c = (dist_c >= 0) & (blk < n_cmp)
    s_c = _nt_dot(q8, kc.astype(BF16)) - slope * dist_c.astype(F32)
    p_c = _masked_softmax(s_c, mask_c, 1)
    o_c = _dot(p_c[:, :n_cmp].astype(BF16), kvc[:, LANES:].astype(BF16))

    forced = (blk == cur) | (blk == 0)
    started = blk <= cur
    below = (lax.broadcasted_iota(jnp.int32, (LANES, LANES), 0)
             < lax.broadcasted_iota(jnp.int32, (LANES, LANES), 1))
    sel_rows = []
    for kv in range(N_KV):
        imp = jnp.sum(p_c[kv * GROUP:(kv + 1) * GROUP], axis=0, keepdims=True)
        score = jnp.where(forced, SEL_FORCED, jnp.where(started, imp, SEL_INVALID))
        col = _row_to_col(score)
        beats = (col > score) | ((col == score) & below)
        rank = jnp.sum(beats.astype(F32), axis=0, keepdims=True)
        sel = ((rank < N_SELECT) & (score > SEL_INVALID * 0.5)).astype(F32)
        sel_rows += [sel] * GROUP
    sel_bias = (jnp.concatenate(sel_rows, axis=0) - 1.0) * -NEG_INF
    qaug_s = jnp.concatenate([q8, jnp.where(lane8 < CMP_BLOCK, sel_bias, ali).astype(BF16)], axis=1)
    qaug_w = jnp.concatenate([q8, ali.astype(BF16)], axis=1)

    def new_key_score(row_ref):
        k_new = row_ref[0][:, :LANES].astype(BF16).astype(F32)
        return jnp.sum(q8.astype(F32) * k_new, axis=1, keepdims=True)

    def finish(scores, values, s_new, v_new):
        m = s_new
        for s in scores:
            m = jnp.maximum(m, jnp.max(s, axis=1, keepdims=True))
        p_new = jnp.exp(s_new - m)
        l = p_new
        acc = p_new * v_new.astype(BF16).astype(F32)
        for s, v in zip(scores, values):
            p = jnp.exp(s - m)
            l = l + jnp.sum(p, axis=1, keepdims=True)
            acc = acc + _nt_dot(p.astype(BF16), v)
        return acc / l

    scores, values = [], []
    for p, ref in enumerate(page_refs):
        page = ref[0, 0]
        kaug = jnp.concatenate([page[:LANES].astype(BF16), exts_ref[p]], axis=0)
        scores.append(_dot(qaug_s, kaug))
        values.append(page[LANES:].astype(BF16))
    o_s = finish(scores, values, new_key_score(ksn_ref), ksn_ref[0][:, LANES:])

    win = win_ref[0, 0]
    kaug = jnp.concatenate([win[:LANES].astype(BF16), extw_ref[...]], axis=0)
    s_w = _dot(qaug_w, kaug)
    dist_w = win_keep - lax.broadcasted_iota(jnp.int32, (1, win_keep), 1)
    s_w = jnp.where((dist_w >= 0) & (dist_w < WINDOW), s_w, NEG_INF)
    o_w = finish([s_w], [win[LANES:].astype(BF16)], new_key_score(kwn_ref), kwn_ref[0][:, LANES:])

    sig = _sigmoid(gl_ref[0])
    gate = [jnp.sum(jnp.where(lane8 == 3 * head + j, sig, 0.0), axis=1, keepdims=True) for j in range(3)]
    out = gate[0] * o_c + gate[1] * o_s + gate[2] * o_w
    out = jnp.where(head < GROUP, out, pltpu.roll(out, HEAD_DIM, axis=1))
    o_ref[0] = out[:, :HEAD_DIM]


def _attn_sample(cache_t, win_t, pt_flat, layer, pr, kvc, ext_s, ext_w, db, n_pages, win_keep):
    past_len = n_pages * PAGE_SIZE
    n_cmp = past_len // CMP_BLOCK
    row = lambda width: pl.BlockSpec((1, 1, width), lambda b, pt: (b, 0, 0))
    grid_spec = pltpu.PrefetchScalarGridSpec(
        num_scalar_prefetch=1,
        grid=(db,),
        in_specs=[_page_spec(layer, p, n_pages) for p in range(n_pages)]
        + [row(Q_WIDE), row(LANES), row(D_KV), row(D_KV),
           pl.BlockSpec((1, n_cmp, D_KV), lambda b, pt: (b, 0, 0)),
           pl.BlockSpec((1, 1, D_KV, win_keep), lambda b, pt: (layer, b, 0, 0)),
           pl.BlockSpec((n_pages, LANES, PAGE_SIZE), lambda b, pt: (0, 0, 0)),
           pl.BlockSpec((LANES, win_keep), lambda b, pt: (0, 0))],
        out_specs=pl.BlockSpec((1, N_HEADS, HEAD_DIM), lambda b, pt: (b, 0, 0)),
    )
    kern = functools.partial(_attn_sample_kernel, n_pages=n_pages, past_len=past_len, win_keep=win_keep)
    out = pl.pallas_call(
        kern,
        grid_spec=grid_spec,
        out_shape=jax.ShapeDtypeStruct((db, N_HEADS, HEAD_DIM), F32),
        compiler_params=_params("parallel"),
        name="attn_sample",
    )(pt_flat, *([cache_t] * n_pages), pr["q"].reshape(db, 1, Q_WIDE), pr["gl"].reshape(db, 1, LANES),
      pr["ks"].reshape(db, 1, D_KV), pr["kw"].reshape(db, 1, D_KV), kvc, win_t, ext_s, ext_w)
    return out.reshape(db, D_NSA)


def _win_update_kernel(win_ref, new_ref, o_ref, *, bb, win_keep):
    lane = lax.broadcasted_iota(jnp.int32, (D_KV, win_keep), 1)
    for k in range(bb):
        row = new_ref[0, k]
        col = jnp.concatenate([_row_to_col(row[:, h * LANES:(h + 1) * LANES]) for h in range(D_KV // LANES)],
                              axis=0)
        shifted = pltpu.roll(win_ref[0, k], win_keep - 1, axis=1)
        o_ref[0, k] = jnp.where(lane == win_keep - 1, col, shifted)


def _win_update(win_t, kw_new, bb):
    depth, db, _, win_keep = win_t.shape
    return pl.pallas_call(
        functools.partial(_win_update_kernel, bb=bb, win_keep=win_keep),
        grid=(depth, db // bb),
        in_specs=[pl.BlockSpec((1, bb, D_KV, win_keep), lambda l, b: (l, b, 0, 0)),
                  pl.BlockSpec((1, bb, 1, D_KV), lambda l, b: (l, b, 0, 0))],
        out_specs=pl.BlockSpec((1, bb, D_KV, win_keep), lambda l, b: (l, b, 0, 0)),
        out_shape=jax.ShapeDtypeStruct(win_t.shape, F32),
        compiler_params=_params("parallel", "parallel"),
        name="win_update",
    )(win_t, kw_new)


def _pool_window_lane(shape):
    lane = lax.broadcasted_iota(jnp.int32, shape, len(shape) - 1)
    w = jnp.full(shape, POOL_WINDOWS[0], jnp.int32)
    for gi in range(1, len(POOL_WINDOWS)):
        w = jnp.where(lane >= gi * POOL_GROUP_DIM, POOL_WINDOWS[gi], w)
    return w


def _group_rms(v, ones_bd):
    sq = v * v
    hi = sq.astype(BF16)
    lo = (sq - hi.astype(F32)).astype(BF16)
    return (_dot(hi, ones_bd) + _dot(lo, ones_bd)) * (1.0 / GMLP_GROUP_DIM)


def _mix_out(x, ya, za, yb, zb, yc, zc, wo_ref, gpost):
    out = _dot((ya * _silu(za)).astype(BF16), wo_ref[0:D_NSA, :])
    out += _dot((yb * _silu(zb)).astype(BF16), wo_ref[D_NSA:D_NSA + D_POOL, :])
    out += _dot((yc * _silu(zc)).astype(BF16), wo_ref[D_NSA + D_POOL:, :])
    ms = jnp.mean(out * out, axis=-1, keepdims=True)
    return x + out * lax.rsqrt(ms + EPS) * gpost


def _merge_prompt_kernel(x_ref, ya_ref, za_ref, pin_ref, prev_ref, zb_ref, u_ref, v_ref, zc_ref,
                         pw_ref, ps_ref, gn_ref, ones_ref, ws_ref, bs_ref, wo_ref, gp_ref,
                         xo_ref, vn_ref, xext, *, tm, tiles_per_seq):
    i = pl.program_id(0)
    first = (i % tiles_per_seq) == 0
    pin = pin_ref[...]
    xext[POOL_MAX:, :] = pin
    xext[0:POOL_MAX, :] = jnp.where(first, 0.0, prev_ref[...])
    w_lane = _pool_window_lane((tm, D_POOL))
    acc = pin
    for k in range(1, POOL_MAX):
        acc = acc + jnp.where(w_lane > k, xext[pl.ds(POOL_MAX - k, tm), :], 0.0)
    pos = (i % tiles_per_seq) * tm + lax.broadcasted_iota(jnp.int32, (tm, D_POOL), 0)
    cnt = jnp.minimum(w_lane, pos + 1).astype(F32)
    diff = acc / cnt - pin
    yb = _dot(diff.astype(BF16), pw_ref[...]) * ps_ref[...]

    v = v_ref[...]
    vn = v * lax.rsqrt(_group_rms(v, ones_ref[...]) + EPS) * gn_ref[...]
    vn_ref[...] = vn
    vnb = vn.astype(BF16)
    lane = lax.broadcasted_iota(jnp.int32, (CHUNK, D_GMLP), 1)
    chunks = []
    for c in range(tm // CHUNK):
        vc = vnb[c * CHUNK:(c + 1) * CHUNK]
        s = _dot(ws_ref[0], vc)
        for g in range(1, D_GMLP // GMLP_GROUP_DIM):
            s = jnp.where(lane >= g * GMLP_GROUP_DIM, _dot(ws_ref[g], vc), s)
        chunks.append(s + bs_ref[...])
    yc = u_ref[...] * jnp.concatenate(chunks, axis=0)

    xo_ref[...] = _mix_out(x_ref[...], ya_ref[...], za_ref[...], yb, zb_ref[...], yc, zc_ref[...],
                           wo_ref, gp_ref[...])


def _merge_prompt(x2, ya, pr, wts, seq, tm):
    m = x2.shape[0]
    tiles_per_seq = seq // tm
    per = tm // POOL_MAX
    rows = lambda width: pl.BlockSpec((tm, width), lambda i: (i, 0))
    const = lambda shape: pl.BlockSpec(shape, lambda i: (0,) * len(shape))
    kern = functools.partial(_merge_prompt_kernel, tm=tm, tiles_per_seq=tiles_per_seq)
    return pl.pallas_call(
        kern,
        grid=(m // tm,),
        in_specs=[rows(D_MODEL), rows(D_NSA), rows(D_NSA), rows(D_POOL),
                  pl.BlockSpec((POOL_MAX, D_POOL), lambda i: (jnp.maximum(i * per - 1, 0), 0)),
                  rows(D_POOL), rows(D_GMLP), rows(D_GMLP), rows(D_GMLP),
                  const((D_POOL, D_POOL)), const((1, D_POOL)), const((1, D_GMLP)),
                  const((D_GMLP, D_GMLP)), const((4, CHUNK, CHUNK)), const((CHUNK, D_GMLP)),
                  const((D_MODEL, D_MODEL)), const((1, D_MODEL))],
        out_specs=[rows(D_MODEL), rows(D_GMLP)],
        out_shape=[jax.ShapeDtypeStruct((m, D_MODEL), F32), jax.ShapeDtypeStruct((m, D_GMLP), F32)],
        scratch_shapes=[pltpu.VMEM((tm + POOL_MAX, D_POOL), F32)],
        compiler_params=_params("parallel"),
        name="merge_prompt",
    )(x2, ya, pr["za"], pr["pin"], pr["pin"], pr["zb"], pr["u"], pr["v"], pr["zc"],
      wts["pool_w"], wts["pool_scale"], wts["gmlp_norm"], wts["ones_bd"], wts["ws_tril"], wts["bs_full"],
      wts["w_out"], wts["norm_post"])


def _merge_sample_kernel(x_ref, ya_ref, za_ref, pin_ref, st_ref, zb_ref, u_ref, v_ref, zc_ref,
                         pw_ref, ps_ref, gn_ref, ones_ref, w0_ref, b0_ref, wo_ref, gp_ref,
                         xo_ref, vn_ref):
    pin = pin_ref[...]
    w_lane = _pool_window_lane(pin.shape)
    acc = pin
    for k in range(1, POOL_MAX):
        acc = acc + jnp.where(w_lane > k, st_ref[POOL_MAX - 1 - k], 0.0)
    diff = acc / w_lane.astype(F32) - pin
    yb = _dot(diff.astype(BF16), pw_ref[...]) * ps_ref[...]

    v = v_ref[...]
    vn = v * lax.rsqrt(_group_rms(v, ones_ref[...]) + EPS) * gn_ref[...]
    vn_ref[...] = vn
    yc = u_ref[...] * (w0_ref[...] * vn + b0_ref[...])

    xo_ref[...] = _mix_out(x_ref[...], ya_ref[...], za_ref[...], yb, zb_ref[...], yc, zc_ref[...],
                           wo_ref, gp_ref[...])


def _merge_sample(x2, ya, pr, state_t, wts):
    db = x2.shape[0]
    full = lambda shape: pl.BlockSpec(shape, lambda i: (0,) * len(shape))
    args = (x2, ya, pr["za"], pr["pin"], state_t, pr["zb"], pr["u"], pr["v"], pr["zc"],
            wts["pool_w"], wts["pool_scale"], wts["gmlp_norm"], wts["ones_bd"], wts["w0"], wts["b0"],
            wts["w_out"], wts["norm_post"])
    return pl.pallas_call(
        _merge_sample_kernel,
        grid=(1,),
        in_specs=[full(a.shape) for a in args],
        out_specs=[full((db, D_MODEL)), full((db, D_GMLP))],
        out_shape=[jax.ShapeDtypeStruct((db, D_MODEL), F32), jax.ShapeDtypeStruct((db, D_GMLP), F32)],
        compiler_params=_params("arbitrary"),
        name="merge_sample",
    )(*args)


def _layer_weights(l, norm_pre, w_in, cmp_pe, cmp_w1, cmp_w2, pool_w, pool_scale, gmlp_norm, gmlp_ws, gmlp_bs,
                   w_out, norm_post):
    n_g = D_GMLP // GMLP_GROUP_DIM
    ones_bd = jnp.kron(jnp.eye(n_g, dtype=F32), jnp.ones((GMLP_GROUP_DIM, GMLP_GROUP_DIM), F32)).astype(BF16)
    pw = pool_w[l]
    zp = jnp.zeros_like(pw[0])
    pool_bd = jnp.concatenate(
        [jnp.concatenate([pw[i] if j == i else zp for j in range(4)], axis=1) for i in range(4)], axis=0)
    tril = jnp.tril(jnp.ones((CHUNK, CHUNK), F32))
    return dict(
        norm_pre=norm_pre[l].reshape(1, D_MODEL),
        w_in=_prep_w_in(w_in[l]),
        pe_t=_pe_t(cmp_pe[l]),
        w1_t=_blockdiag4(cmp_w1[l]).T.astype(BF16),
        w2=_blockdiag4(cmp_w2[l]).astype(BF16),
        pool_w=pool_bd.astype(BF16),
        pool_scale=pool_scale[l].reshape(1, D_POOL),
        gmlp_norm=gmlp_norm[l].reshape(1, D_GMLP),
        ones_bd=ones_bd,
        ws_tril=(gmlp_ws[l] * tril).astype(BF16),
        bs_full=jnp.repeat(gmlp_bs[l].T, GMLP_GROUP_DIM, axis=1),
        w0=jnp.repeat(gmlp_ws[l][:, 0, 0], GMLP_GROUP_DIM).reshape(1, D_GMLP),
        b0=jnp.repeat(gmlp_bs[l][:, 0], GMLP_GROUP_DIM).reshape(1, D_GMLP),
        w_out=w_out[l].astype(BF16),
        norm_post=norm_post[l].reshape(1, D_MODEL),
    )


def _largest_tile(n, cap):
    t = cap
    while n % t:
        t //= 2
    return t


def _feature_major(cache):
    d, n, t = cache.shape[:3]
    return jnp.transpose(cache, (0, 1, 3, 4, 5, 2)).reshape(d, n, D_KV, t)


def _token_major(x_t):
    d, n, _, t = x_t.shape
    return jnp.transpose(x_t.reshape(d, n, 2, N_KV, HEAD_DIM, t), (0, 1, 5, 2, 3, 4))


def kernel(x_prompt, x_sample, cache_kv_cmp, cache_kv_sel, cache_kv_win, state_pool, page_table, norm_pre, w_in,
           cmp_pe, cmp_w1, cmp_w2, pool_w, pool_scale, gmlp_norm, gmlp_ws, gmlp_bs, w_out, norm_post):
    bp, tp, _ = x_prompt.shape
    db, ts, _ = x_sample.shape
    depth = w_in.shape[0]
    n_pages = page_table.shape[1]
    past_len = n_pages * PAGE_SIZE
    win_keep = cache_kv_win.shape[2]
    assert ts == 1 and tp % CHUNK == 0 and tp <= CMP_BLOCK * CMP_BLOCK and tp >= WINDOW
    assert past_len // CMP_BLOCK + 1 <= CMP_BLOCK and win_keep == WINDOW and db % LANES == 0

    mp = bp * tp
    tm_proj = _largest_tile(tp, 512)
    tile_cmp = _largest_tile(tp, 1024)
    tm_merge = _largest_tile(tp, 512)
    tq, tk = 128, _largest_tile(tp, 512)
    assert tp >= WINDOW + tq

    cmp_t = _feature_major(cache_kv_cmp)
    sel_t = _feature_major(cache_kv_sel)
    win_t = _feature_major(cache_kv_win)
    pt_flat = page_table.reshape(-1).astype(jnp.int32)

    ext_prompt = _extra_rows(jnp.arange(tp, dtype=jnp.int32), True).T
    ext_s = jnp.transpose(
        _extra_rows(jnp.arange(past_len, dtype=jnp.int32), True).reshape(LANES, n_pages, PAGE_SIZE), (1, 0, 2))
    ext_w = _extra_rows(past_len - win_keep + jnp.arange(win_keep, dtype=jnp.int32), False)

    xp = x_prompt.reshape(mp, D_MODEL)
    xs = x_sample.reshape(db, D_MODEL)
    names = ("kvc_p", "kvc_s", "kvs_p", "kvs_s", "kvw_p", "kw_new", "pool_p", "pool_s", "gv_p", "gv_s")
    outs = {k: [] for k in names}
    for l in range(depth):
        wts = _layer_weights(l, norm_pre, w_in, cmp_pe, cmp_w1, cmp_w2, pool_w, pool_scale, gmlp_norm, gmlp_ws,
                             gmlp_bs, w_out, norm_post)
        pr = _project(xp, wts["norm_pre"], wts["w_in"], bp, tp, tm_proj, False)
        kvc = _compress_prompt(pr["kc_t"], wts, tile_cmp)
        ya = _attn_prompt(pr["q"], pr["gl"], kvc, pr["ks_t"], pr["kw_t"], ext_prompt, bp, tp, tq, tk)
        xp, vn = _merge_prompt(xp, ya, pr, wts, tp, tm_merge)
        outs["kvc_p"].append(pr["kc_t"])
        outs["kvs_p"].append(pr["ks_t"])
        outs["kvw_p"].append(pr["kw_t"][:, :, tp - WINDOW:])
        outs["pool_p"].append(pr["pin"].reshape(bp, tp, D_POOL)[:, tp - (POOL_MAX - 1):])
        outs["gv_p"].append(vn.reshape(bp, tp, D_GMLP)[:, tp - CHUNK:])
        pr = _project(xs, wts["norm_pre"], wts["w_in"], 1, db, db, True)
        kvc = _compress_sample(cmp_t, pt_flat, l, wts, db, n_pages)
        ya = _attn_sample(sel_t, win_t, pt_flat, l, pr, kvc, ext_s, ext_w, db, n_pages, win_keep)
        state_t = jnp.swapaxes(state_pool[l], 0, 1)
        xs, vn = _merge_sample(xs, ya, pr, state_t, wts)
        outs["kvc_s"].append(pr["kc_t"])
        outs["kvs_s"].append(pr["ks_t"])
        outs["kw_new"].append(pr["kw"].reshape(db, 1, D_KV))
        outs["pool_s"].append(jnp.concatenate([state_t[1:], pr["pin"][None]], axis=0))
        outs["gv_s"].append(vn.reshape(db, 1, D_GMLP))
    st = {k: jnp.stack(v) for k, v in outs.items()}
    kvw_s = _win_update(win_t, st["kw_new"], 4)
    sample_kv = lambda x_t: jnp.transpose(_token_major(x_t), (0, 2, 1, 3, 4, 5))
    return (xp.reshape(bp, tp, D_MODEL), xs.reshape(db, ts, D_MODEL),
            _token_major(st["kvc_p"]), sample_kv(st["kvc_s"]),
            _token_major(st["kvs_p"]), sample_kv(st["kvs_s"]),
            _token_major(st["kvw_p"]), _token_major(kvw_s),
            st["pool_p"], jnp.swapaxes(st["pool_s"], 1, 2), st["gv_p"], st["gv_s"])
```

```python
import functools

import jax
import jax.numpy as jnp
from jax import lax
from jax.experimental import pallas as pl
from jax.experimental.pallas import tpu as pltpu

F32 = jnp.float32
BF16 = jnp.bfloat16

D_MODEL = 1024
HEAD_DIM = 64
N_HEADS = 8
N_KV = 2
GROUP = N_HEADS // N_KV
D_NSA = N_HEADS * HEAD_DIM
D_KV = 2 * N_KV * HEAD_DIM
CMP_BLOCK = 64
N_SELECT = 16
WINDOW = 512
D_POOL = 256
POOL_WINDOWS = (2, 4, 8, 16)
POOL_GROUP_DIM = 64
POOL_MAX = 16
D_GMLP = 256
GMLP_GROUP_DIM = 64
CHUNK = 128
PAGE_SIZE = 128
EPS = 1e-6
NEG_INF = -1e30
SEL_FORCED = 1e4
SEL_INVALID = -1e4
Q_SCALE = HEAD_DIM ** -0.5

WIN_BLOCK = 128
SUB_KEYS = 256
LANES = 128
Q_WIDE = N_HEADS * LANES
VMEM_LIMIT = 56 * 1024 * 1024

ROW_ONE_A = 64
ROW_ONE_B = 65
ROW_POS_HI = 66
ROW_POS_LO = 67

_SEG = dict(q=(0, 512), kc=(512, 256), ks=(768, 256), kw=(1024, 256), gl=(1280, 24), za=(1304, 512),
            pin=(1816, 256), zb=(2072, 256), u=(2328, 256), v=(2584, 256), zc=(2840, 256))
_PROJ_OUT = (("q", Q_WIDE), ("kc", 256), ("ks", 256), ("kw", 256), ("za", 512), ("pin", 256),
             ("zb", 256), ("u", 256), ("v", 256), ("zc", 256), ("gl", LANES))
_KV_NAMES = ("kc", "ks", "kw")


def _nt_dot(a, b):
    return lax.dot_general(a, b, (((1,), (1,)), ((), ())), preferred_element_type=F32)


def _tn_dot(a, b):
    return lax.dot_general(a, b, (((0,), (0,)), ((), ())), preferred_element_type=F32)


def _dot(a, b):
    return jnp.dot(a, b, preferred_element_type=F32)


def _sigmoid(x):
    return 0.5 * jnp.tanh(0.5 * x) + 0.5


def _silu(x):
    return x * _sigmoid(x)


def _params(*sem):
    return pltpu.CompilerParams(dimension_semantics=sem, vmem_limit_bytes=VMEM_LIMIT)


def _row_to_col(row):
    eye = (lax.broadcasted_iota(jnp.int32, (LANES, LANES), 0)
           == lax.broadcasted_iota(jnp.int32, (LANES, LANES), 1))
    return jnp.sum(jnp.where(eye, row, 0.0), axis=1, keepdims=True)


def _proj_kernel(x_ref, g_ref, w_ref, *out_refs, emit_rows):
    x = x_ref[...]
    ms = jnp.mean(x * x, axis=-1, keepdims=True)
    xn = (x * lax.rsqrt(ms + EPS) * g_ref[...]).astype(BF16)
    refs = iter(out_refs)
    off = 0
    for name, width in _PROJ_OUT:
        res = _dot(xn, w_ref[:, off:off + width])
        off += width
        if name in _KV_NAMES:
            next(refs)[0] = res.T
            if not emit_rows:
                continue
        next(refs)[...] = res


def _project(x2, g, w, batch, seq, tm, emit_rows):
    m = x2.shape[0]
    n_tot = w.shape[1]
    tiles = seq // tm
    names, shapes, specs = [], [], []
    for name, width in _PROJ_OUT:
        if name in _KV_NAMES:
            names.append(name + "_t")
            shapes.append(jax.ShapeDtypeStruct((batch, width, seq), F32))
            specs.append(pl.BlockSpec((1, width, tm), lambda i: (i // tiles, 0, i % tiles)))
            if not emit_rows:
                continue
        names.append(name)
        shapes.append(jax.ShapeDtypeStruct((m, width), F32))
        specs.append(pl.BlockSpec((tm, width), lambda i: (i, 0)))
    res = pl.pallas_call(
        functools.partial(_proj_kernel, emit_rows=emit_rows),
        grid=(m // tm,),
        in_specs=[pl.BlockSpec((tm, D_MODEL), lambda i: (i, 0)),
                  pl.BlockSpec((1, D_MODEL), lambda i: (0, 0)),
                  pl.BlockSpec((D_MODEL, n_tot), lambda i: (0, 0))],
        out_specs=specs,
        out_shape=shapes,
        compiler_params=_params("parallel"),
        name="norm_project",
    )(x2, g, w)
    return dict(zip(names, res))


def _prep_w_in(w):
    def seg(name):
        o, n = _SEG[name]
        return w[:, o:o + n]
    wq = seg("q").reshape(D_MODEL, N_HEADS, HEAD_DIM)
    zeros = jnp.zeros_like(wq)
    lo = jnp.concatenate([wq, zeros], axis=-1)
    hi = jnp.concatenate([zeros, wq], axis=-1)
    first = (jnp.arange(N_HEADS) < GROUP)[None, :, None]
    q_wide = jnp.where(first, lo, hi).reshape(D_MODEL, Q_WIDE)
    gl = jnp.pad(seg("gl"), ((0, 0), (0, LANES - 3 * N_HEADS)))
    cols = [q_wide] + [seg(n) for n, _ in _PROJ_OUT[1:-1]] + [gl]
    return jnp.concatenate(cols, axis=1).astype(BF16)


def _blockdiag4(w):
    z = jnp.zeros((HEAD_DIM, HEAD_DIM), w.dtype)
    blocks = [w[0], w[0], w[1], w[1]]
    rows = [jnp.concatenate([blocks[i] if j == i else z for j in range(4)], axis=1) for i in range(4)]
    return jnp.concatenate(rows, axis=0)


def _pe_t(pe):
    one = jnp.concatenate([pe[0].T, pe[0].T, pe[1].T, pe[1].T], axis=0)
    return jnp.concatenate([one, one], axis=1)


def _compress_t(xt, pe128, w1, w2):
    n = xt.shape[1]
    pet = jnp.concatenate([pe128] * (n // LANES), axis=1)
    xb = (xt + pet).T.astype(BF16)
    hid = _silu(_dot(xb, w1))
    mean = jnp.sum(hid.reshape(n // CMP_BLOCK, CMP_BLOCK, D_KV), axis=1) * (1.0 / CMP_BLOCK)
    return _dot(mean.astype(BF16), w2)


def _compress_kernel(x_ref, pe_ref, w1_ref, w2_ref, o_ref):
    o_ref[...] = _compress_t(x_ref[0], pe_ref[...], w1_ref[...], w2_ref[...])


def _compress_prompt(kc_t, wts, tile):
    batch, _, seq = kc_t.shape
    tiles = seq // tile
    const = lambda shape: pl.BlockSpec(shape, lambda b, j: (0,) * len(shape))
    return pl.pallas_call(
        _compress_kernel,
        grid=(batch, tiles),
        in_specs=[pl.BlockSpec((1, D_KV, tile), lambda b, j: (b, 0, j)),
                  const((D_KV, LANES)), const((D_KV, D_KV)), const((D_KV, D_KV))],
        out_specs=pl.BlockSpec((tile // CMP_BLOCK, D_KV), lambda b, j: (b * tiles + j, 0)),
        out_shape=jax.ShapeDtypeStruct((batch * seq // CMP_BLOCK, D_KV), F32),
        compiler_params=_params("parallel", "parallel"),
        name="compress_prompt",
    )(kc_t, wts["pe_t"], wts["w1"], wts["w2"])


def _compress_pages_kernel(pt_ref, *refs, n_pages):
    del pt_ref
    page_refs = refs[:n_pages]
    pe_ref, w1_ref, w2_ref, o_ref = refs[n_pages:]
    xt = jnp.concatenate([r[0, 0] for r in page_refs], axis=1)
    o_ref[0] = _compress_t(xt, pe_ref[...], w1_ref[...], w2_ref[...])


def _page_spec(layer, p, n_pages):
    return pl.BlockSpec((1, 1, D_KV, PAGE_SIZE), lambda b, pt: (layer, pt[b * n_pages + p], 0, 0))


def _compress_sample(cache_t, pt_flat, layer, wts, db, n_pages):
    past_len = n_pages * PAGE_SIZE
    n_cmp = past_len // CMP_BLOCK
    const = lambda shape: pl.BlockSpec(shape, lambda b, pt: (0,) * len(shape))
    grid_spec = pltpu.PrefetchScalarGridSpec(
        num_scalar_prefetch=1,
        grid=(db,),
        in_specs=[_page_spec(layer, p, n_pages) for p in range(n_pages)]
        + [const((D_KV, LANES)), const((D_KV, D_KV)), const((D_KV, D_KV))],
        out_specs=pl.BlockSpec((1, n_cmp, D_KV), lambda b, pt: (b, 0, 0)),
    )
    return pl.pallas_call(
        functools.partial(_compress_pages_kernel, n_pages=n_pages),
        grid_spec=grid_spec,
        out_shape=jax.ShapeDtypeStruct((db, n_cmp, D_KV), F32),
        compiler_params=_params("parallel"),
        name="compress_sample",
    )(pt_flat, *([cache_t] * n_pages), wts["pe_t"], wts["w1"], wts["w2"])


def _extra_rows(pos, with_onehot):
    row = jnp.arange(LANES)[:, None]
    blk = (pos // CMP_BLOCK)[None, :]
    hi = ((pos // CMP_BLOCK) * CMP_BLOCK).astype(F32)[None, :]
    lo = (pos % CMP_BLOCK).astype(F32)[None, :]
    out = jnp.where((row == ROW_ONE_A) | (row == ROW_ONE_B), 1.0, 0.0) + jnp.zeros_like(hi)
    out = jnp.where(row == ROW_POS_HI, hi, out)
    out = jnp.where(row == ROW_POS_LO, lo, out)
    if with_onehot:
        out = jnp.where((row < CMP_BLOCK) & (row == blk), 1.0, out)
    return out.astype(BF16)


def _alibi_lanes(slope, t_hi, t_lo, shape):
    lane = lax.broadcasted_iota(jnp.int32, shape, 1)
    out = jnp.where(lane == ROW_ONE_A, -slope * t_hi, 0.0)
    out = jnp.where(lane == ROW_ONE_B, -slope * t_lo, out)
    return jnp.where((lane == ROW_POS_HI) | (lane == ROW_POS_LO), slope, out)


def _masked_softmax(s, mask, axis):
    s = jnp.where(mask, s, NEG_INF)
    p = jnp.exp(s - jnp.max(s, axis=axis, keepdims=True)) * mask.astype(F32)
    return p / jnp.maximum(jnp.sum(p, axis=axis, keepdims=True), 1e-30)


def _attn_prompt_kernel(q_ref, gl_ref, kvc_ref, ksk_ref, ksv_ref, kwk_ref, kwv_ref, ext_ref, o_ref,
                        kaug_s, kaug_w, v_s, v_w, s_buf, *, seq, tq, tk):
    i = pl.program_id(1)
    m = GROUP * tq
    n_cmp = seq // CMP_BLOCK

    n_wblk = WINDOW // WIN_BLOCK + tq // WIN_BLOCK

    @pl.when(i == 0)
    def _fill():
        for c in range(seq // tk):
            cols = slice(c * tk, (c + 1) * tk)
            kaug_s[c] = jnp.concatenate([ksk_ref[0, :, cols].T.astype(BF16), ext_ref[cols, :]], axis=1)
            v_s[c] = ksv_ref[0, :, cols].astype(BF16)
        for c in range(seq // WIN_BLOCK):
            cols = slice(c * WIN_BLOCK, (c + 1) * WIN_BLOCK)
            kaug_w[c] = jnp.concatenate([kwk_ref[0, :, cols].T.astype(BF16), ext_ref[cols, :]], axis=1)
            v_w[c] = kwv_ref[0, :, cols].astype(BF16)

    q0 = i * tq
    lane_t = q0 + lax.broadcasted_iota(jnp.int32, (1, m), 1) % tq
    t_hi = ((lane_t // CMP_BLOCK) * CMP_BLOCK).astype(F32)
    t_lo = (lane_t % CMP_BLOCK).astype(F32)
    ext_row = lax.broadcasted_iota(jnp.int32, (LANES, m), 0)
    key_row = lax.broadcasted_iota(jnp.int32, (tk, 1), 0)
    sig_t = _sigmoid(gl_ref[...]).T

    blk_row = lax.broadcasted_iota(jnp.int32, (n_cmp, m), 0)
    dist_c = lane_t - ((blk_row + 1) * CMP_BLOCK - 1)
    mask_c = dist_c >= 0
    blk2 = lax.broadcasted_iota(jnp.int32, (n_cmp, tq), 0)
    cur = (q0 + lax.broadcasted_iota(jnp.int32, (1, tq), 1)) // CMP_BLOCK
    forced = (blk2 == cur) | (blk2 == 0)
    started = blk2 <= cur

    def pieces_of(n):
        return [(a, min(a + SUB_KEYS, n)) for a in range(0, n, SUB_KEYS)]

    def qk_scores(qaugs, k_tile):
        return [[_dot(k_tile[a:b], qaug_t) for qaug_t in qaugs] for a, b in pieces_of(k_tile.shape[0])]

    def softmax_update(s, v_piece, carry, mask):
        m_i, l_i, acc = carry
        if mask is not None:
            s = jnp.where(mask, s, NEG_INF)
        m_new = jnp.maximum(m_i, jnp.max(s, axis=0, keepdims=True))
        alpha = jnp.exp(m_i - m_new)
        p = jnp.exp(s - m_new)
        l_new = alpha * l_i + jnp.sum(p, axis=0, keepdims=True)
        return m_new, l_new, alpha * acc + _dot(v_piece, p.astype(BF16))

    def softmax_pv(scores, v_tile, carries, mask_fn):
        carries = list(carries)
        for (a, b), piece_scores in zip(pieces_of(v_tile.shape[1]), scores):
            for kv, s in enumerate(piece_scores):
                carries[kv] = softmax_update(s, v_tile[:, a:b], carries[kv], mask_fn(a, b))
        return tuple(carries)

    def sel_tile(j, carries, mask_fn, prefetch):
        carries = list(carries)
        for p, (a, b) in enumerate(pieces_of(tk)):
            for kv in range(N_KV):
                mask = None if mask_fn is None else mask_fn(a, b)
                carries[kv] = softmax_update(s_buf[p * N_KV + kv], v_s[j, :, a:b], carries[kv], mask)
            if prefetch:
                for kv in range(N_KV):
                    s_buf[p * N_KV + kv] = _dot(kaug_s[j + 1, a:b, :], qaugs_s[kv])
        return tuple(carries)

    init = (jnp.full((1, m), NEG_INF, F32), jnp.zeros((1, m), F32), jnp.zeros((LANES, m), F32))
    n_chunk = n_cmp // 8
    q4_ts, slope_rows, ali_ts, qaugs_w = [], [], [], []
    for kv in range(N_KV):
        qk = q_ref[:, kv * GROUP * LANES:(kv + 1) * GROUP * LANES]
        q4_t = jnp.concatenate([(qk[:, g * LANES:(g + 1) * LANES] * Q_SCALE).T for g in range(GROUP)],
                               axis=1).astype(BF16)
        slopes = [2.0 ** -(kv * GROUP + g + 1) for g in range(GROUP)]
        slope = jnp.concatenate([jnp.full((1, tq), s, F32) for s in slopes], axis=1)
        ali_t = jnp.where(ext_row == ROW_ONE_A, -slope * t_hi, 0.0)
        ali_t = jnp.where(ext_row == ROW_ONE_B, -slope * t_lo, ali_t)
        ali_t = jnp.where((ext_row == ROW_POS_HI) | (ext_row == ROW_POS_LO), slope, ali_t)
        q4_ts.append(q4_t)
        slope_rows.append(slope)
        ali_ts.append(ali_t)
        qaugs_w.append(jnp.concatenate([q4_t, ali_t.astype(BF16)], axis=0))

    c0 = jnp.maximum(q0 // WIN_BLOCK - WINDOW // WIN_BLOCK, 0)
    k_win = jnp.concatenate([kaug_w[c0 + t] for t in range(n_wblk)], axis=0)
    v_win = jnp.concatenate([v_w[c0 + t] for t in range(n_wblk)], axis=1)
    scores_w = qk_scores(qaugs_w, k_win)

    o_cs, qaugs_s = [], []
    for kv in range(N_KV):
        q4_t, slope, ali_t = q4_ts[kv], slope_rows[kv], ali_ts[kv]
        kvc = kvc_ref[...]
        s_ct = _dot(kvc[:, :LANES].astype(BF16), q4_t) - slope * dist_c.astype(F32)
        p_ct = _masked_softmax(s_ct, mask_c, 0)
        o_c = _tn_dot(kvc[:, LANES:].astype(BF16), p_ct.astype(BF16))

        imp = p_ct[:, 0:tq]
        for g in range(1, GROUP):
            imp = imp + p_ct[:, g * tq:(g + 1) * tq]
        score = jnp.where(forced, SEL_FORCED, jnp.where(started, imp, SEL_INVALID))
        chunks = [score[8 * c:8 * c + 8] for c in range(n_chunk)]
        blk8 = lax.broadcasted_iota(jnp.int32, (8, tq), 0)
        ranks = [jnp.zeros((8, tq), F32) for _ in range(n_chunk)]
        for b in range(n_cmp):
            row = score[b:b + 1, :]
            for c in range(n_chunk):
                if 8 * c > b:
                    beats = row >= chunks[c]
                elif 8 * c + 7 < b:
                    beats = row > chunks[c]
                else:
                    beats = (row > chunks[c]) | ((row == chunks[c]) & (blk8 > b - 8 * c))
                ranks[c] = ranks[c] + jnp.where(beats, 1.0, 0.0)
        rank = jnp.concatenate(ranks, axis=0)
        sel_t = ((rank < N_SELECT) & (score > SEL_INVALID * 0.5)).astype(F32)
        sel_bias = (sel_t - 1.0) * -NEG_INF
        if n_cmp < LANES:
            sel_bias = jnp.concatenate([sel_bias, jnp.zeros((LANES - n_cmp, tq), F32)], axis=0)
        sel_bias = jnp.concatenate([sel_bias] * GROUP, axis=1)

        o_cs.append(o_c)
        qaugs_s.append(
            jnp.concatenate([q4_t, jnp.where(ext_row < CMP_BLOCK, sel_bias, ali_t).astype(BF16)], axis=0))

    n_kt = (q0 + tq + tk - 1) // tk
    for p, piece_scores in enumerate(qk_scores(qaugs_s, kaug_s[0])):
        for kv, s in enumerate(piece_scores):
            s_buf[p * N_KV + kv] = s

    dist_w = lane_t - (c0 * WIN_BLOCK + lax.broadcasted_iota(jnp.int32, (n_wblk * WIN_BLOCK, 1), 0))

    def win_mask(a, b):
        return (dist_w[a:b] >= 0) & (dist_w[a:b] < WINDOW)
    o_ws = [acc / l for _, l, acc in softmax_pv(scores_w, v_win, (init,) * N_KV, win_mask)]

    carries = lax.fori_loop(0, n_kt - 1, lambda j, c: sel_tile(j, c, None, True), (init,) * N_KV)
    carries = sel_tile(n_kt - 1, carries, lambda a, b: (n_kt - 1) * tk + key_row[a:b] <= lane_t, False)
    o_ss = [acc / l for _, l, acc in carries]

    heads = []
    for kv in range(N_KV):
        rows = slice(kv * HEAD_DIM, (kv + 1) * HEAD_DIM)
        for g in range(GROUP):
            h = kv * GROUP + g
            cols = slice(g * tq, (g + 1) * tq)
            heads.append(sig_t[3 * h:3 * h + 1] * o_cs[kv][rows, cols]
                         + sig_t[3 * h + 1:3 * h + 2] * o_ss[kv][rows, cols]
                         + sig_t[3 * h + 2:3 * h + 3] * o_ws[kv][rows, cols])
    o_ref[...] = jnp.concatenate(heads, axis=0).T


def _attn_prompt(qw, gl, kvc, ks_t, kw_t, ext, batch, seq, tq, tk):
    nq = seq // tq
    n_cmp = seq // CMP_BLOCK
    n_kt = seq // tk
    kern = functools.partial(_attn_prompt_kernel, seq=seq, tq=tq, tk=tk)
    kv_spec = lambda half: pl.BlockSpec((1, LANES, seq), lambda b, i: (b, half, 0))
    return pl.pallas_call(
        kern,
        grid=(batch, nq),
        in_specs=[pl.BlockSpec((tq, Q_WIDE), lambda b, i: (b * nq + i, 0)),
                  pl.BlockSpec((tq, LANES), lambda b, i: (b * nq + i, 0)),
                  pl.BlockSpec((n_cmp, D_KV), lambda b, i: (b, 0)),
                  kv_spec(0), kv_spec(1), kv_spec(0), kv_spec(1),
                  pl.BlockSpec((seq, LANES), lambda b, i: (0, 0))],
        out_specs=pl.BlockSpec((tq, D_NSA), lambda b, i: (b * nq + i, 0)),
        out_shape=jax.ShapeDtypeStruct((batch * seq, D_NSA), F32),
        scratch_shapes=[pltpu.VMEM((n_kt, tk, 2 * LANES), BF16),
                        pltpu.VMEM((seq // WIN_BLOCK, WIN_BLOCK, 2 * LANES), BF16),
                        pltpu.VMEM((n_kt, LANES, tk), BF16),
                        pltpu.VMEM((seq // WIN_BLOCK, LANES, WIN_BLOCK), BF16),
                        pltpu.VMEM((N_KV * tk // SUB_KEYS, SUB_KEYS, GROUP * tq), F32)],
        compiler_params=_params("arbitrary", "arbitrary"),
        name="attn_prompt",
    )(qw, gl, kvc, ks_t, ks_t, kw_t, kw_t, ext)


def _attn_sample_kernel(pt_ref, *refs, n_pages, past_len, win_keep):
    del pt_ref
    page_refs = refs[:n_pages]
    (q_ref, gl_ref, ksn_ref, kwn_ref, kvc_ref, win_ref, exts_ref, extw_ref, o_ref) = refs[n_pages:]
    n_cmp = past_len // CMP_BLOCK
    cur = past_len // CMP_BLOCK
    t_hi = float(cur * CMP_BLOCK)
    t_lo = float(past_len % CMP_BLOCK)

    qrow = q_ref[0]
    q8f = jnp.concatenate([qrow[:, h * LANES:(h + 1) * LANES] for h in range(N_HEADS)], axis=0) * Q_SCALE
    q8 = q8f.astype(BF16)
    head = lax.broadcasted_iota(jnp.int32, (N_HEADS, 1), 0)
    slope = jnp.exp2(-(head + 1).astype(F32))
    lane8 = lax.broadcasted_iota(jnp.int32, (N_HEADS, LANES), 1)
    ali = _alibi_lanes(slope, t_hi, t_lo, (N_HEADS, LANES))

    kvc = kvc_ref[0]
    kc = kvc[:, :LANES]
    if n_cmp < LANES:
        kc = jnp.concatenate([kc, jnp.zeros((LANES - n_cmp, LANES), F32)], axis=0)
    blk = lax.broadcasted_iota(jnp.int32, (1, LANES), 1)
    dist_c = past_len - ((blk + 1) * CMP_BLOCK - 1)
    mask_c = (dist_c >= 0) & (blk < n_cmp)
    s_c = _nt_dot(q8, kc.astype(BF16)) - slope * dist_c.astype(F32)
    p_c = _masked_softmax(s_c, mask_c, 1)
    o_c = _dot(p_c[:, :n_cmp].astype(BF16), kvc[:, LANES:].astype(BF16))

    forced = (blk == cur) | (blk == 0)
    started = blk <= cur
    below = (lax.broadcasted_iota(jnp.int32, (LANES, LANES), 0)
             < lax.broadcasted_iota(jnp.int32, (LANES, LANES), 1))
    sel_rows = []
    for kv in range(N_KV):
        imp = jnp.sum(p_c[kv * GROUP:(kv + 1) * GROUP], axis=0, keepdims=True)
        score = jnp.where(forced, SEL_FORCED, jnp.where(started, imp, SEL_INVALID))
        col = _row_to_col(score)
        beats = (col > score) | ((col == score) & below)
        rank = jnp.sum(beats.astype(F32), axis=0, keepdims=True)
        sel = ((rank < N_SELECT) & (score > SEL_INVALID * 0.5)).astype(F32)
        sel_rows += [sel] * GROUP
    sel_bias = (jnp.concatenate(sel_rows, axis=0) - 1.0) * -NEG_INF
    qaug_s = jnp.concatenate([q8, jnp.where(lane8 < CMP_BLOCK, sel_bias, ali).astype(BF16)], axis=1)
    qaug_w = jnp.concatenate([q8, ali.astype(BF16)], axis=1)

    def new_key_score(row_ref):
        k_new = row_ref[0][:, :LANES].astype(BF16).astype(F32)
        return jnp.sum(q8.astype(F32) * k_new, axis=1, keepdims=True)

    def finish(scores, values, s_new, v_new):
        m = s_new
        for s in scores:
            m = jnp.maximum(m, jnp.max(s, axis=1, keepdims=True))
        p_new = jnp.exp(s_new - m)
        l = p_new
        acc = p_new * v_new.astype(BF16).astype(F32)
        for s, v in zip(scores, values):
            p = jnp.exp(s - m)
            l = l + jnp.sum(p, axis=1, keepdims=True)
            acc = acc + _nt_dot(p.astype(BF16), v)
        return acc / l

    scores, values = [], []
    for p, ref in enumerate(page_refs):
        page = ref[0, 0]
        kaug = jnp.concatenate([page[:LANES].astype(BF16), exts_ref[p]], axis=0)
        scores.append(_dot(qaug_s, kaug))
        values.append(page[LANES:].astype(BF16))
    o_s = finish(scores, values, new_key_score(ksn_ref), ksn_ref[0][:, LANES:])

    win = win_ref[0, 0]
    kaug = jnp.concatenate([win[:LANES].astype(BF16), extw_ref[...]], axis=0)
    s_w = _dot(qaug_w, kaug)
    dist_w = win_keep - lax.broadcasted_iota(jnp.int32, (1, win_keep), 1)
    s_w = jnp.where((dist_w >= 0) & (dist_w < WINDOW), s_w, NEG_INF)
    o_w = finish([s_w], [win[LANES:].astype(BF16)], new_key_score(kwn_ref), kwn_ref[0][:, LANES:])

    sig = _sigmoid(gl_ref[0])
    gate = [jnp.sum(jnp.where(lane8 == 3 * head + j, sig, 0.0), axis=1, keepdims=True) for j in range(3)]
    out = gate[0] * o_c + gate[1] * o_s + gate[2] * o_w
    out = jnp.where(head < GROUP, out, pltpu.roll(out, HEAD_DIM, axis=1))
    o_ref[0] = out[:, :HEAD_DIM]


def _attn_sample(cache_t, win_t, pt_flat, layer, pr, kvc, ext_s, ext_w, db, n_pages, win_keep):
    past_len = n_pages * PAGE_SIZE
    n_cmp = past_len // CMP_BLOCK
    row = lambda width: pl.BlockSpec((1, 1, width), lambda b, pt: (b, 0, 0))
    grid_spec = pltpu.PrefetchScalarGridSpec(
        num_scalar_prefetch=1,
        grid=(db,),
        in_specs=[_page_spec(layer, p, n_pages) for p in range(n_pages)]
        + [row(Q_WIDE), row(LANES), row(D_KV), row(D_KV),
           pl.BlockSpec((1, n_cmp, D_KV), lambda b, pt: (b, 0, 0)),
           pl.BlockSpec((1, 1, D_KV, win_keep), lambda b, pt: (layer, b, 0, 0)),
           pl.BlockSpec((n_pages, LANES, PAGE_SIZE), lambda b, pt: (0, 0, 0)),
           pl.BlockSpec((LANES, win_keep), lambda b, pt: (0, 0))],
        out_specs=pl.BlockSpec((1, N_HEADS, HEAD_DIM), lambda b, pt: (b, 0, 0)),
    )
    kern = functools.partial(_attn_sample_kernel, n_pages=n_pages, past_len=past_len, win_keep=win_keep)
    out = pl.pallas_call(
        kern,
        grid_spec=grid_spec,
        out_shape=jax.ShapeDtypeStruct((db, N_HEADS, HEAD_DIM), F32),
        compiler_params=_params("parallel"),
        name="attn_sample",
    )(pt_flat, *([cache_t] * n_pages), pr["q"].reshape(db, 1, Q_WIDE), pr["gl"].reshape(db, 1, LANES),
      pr["ks"].reshape(db, 1, D_KV), pr["kw"].reshape(db, 1, D_KV), kvc, win_t, ext_s, ext_w)
    return out.reshape(db, D_NSA)


def _win_update_kernel(win_ref, new_ref, o_ref, *, bb, win_keep):
    lane = lax.broadcasted_iota(jnp.int32, (D_KV, win_keep), 1)
    for k in range(bb):
        row = new_ref[0, k]
        col = jnp.concatenate([_row_to_col(row[:, h * LANES:(h + 1) * LANES]) for h in range(D_KV // LANES)],
                              axis=0)
        shifted = pltpu.roll(win_ref[0, k], win_keep - 1, axis=1)
        o_ref[0, k] = jnp.where(lane == win_keep - 1, col, shifted)


def _win_update(win_t, kw_new, bb):
    depth, db, _, win_keep = win_t.shape
    return pl.pallas_call(
        functools.partial(_win_update_kernel, bb=bb, win_keep=win_keep),
        grid=(depth, db // bb),
        in_specs=[pl.BlockSpec((1, bb, D_KV, win_keep), lambda l, b: (l, b, 0, 0)),
                  pl.BlockSpec((1, bb, 1, D_KV), lambda l, b: (l, b, 0, 0))],
        out_specs=pl.BlockSpec((1, bb, D_KV, win_keep), lambda l, b: (l, b, 0, 0)),
        out_shape=jax.ShapeDtypeStruct(win_t.shape, F32),
        compiler_params=_params("parallel", "parallel"),
        name="win_update",
    )(win_t, kw_new)


def _pool_window_lane(shape):
    lane = lax.broadcasted_iota(jnp.int32, shape, len(shape) - 1)
    w = jnp.full(shape, POOL_WINDOWS[0], jnp.int32)
    for gi in range(1, len(POOL_WINDOWS)):
        w = jnp.where(lane >= gi * POOL_GROUP_DIM, POOL_WINDOWS[gi], w)
    return w


def _group_rms(v, ones_bd):
    sq = v * v
    hi = sq.astype(BF16)
    lo = (sq - hi.astype(F32)).astype(BF16)
    return (_dot(hi, ones_bd) + _dot(lo, ones_bd)) * (1.0 / GMLP_GROUP_DIM)


def _mix_out(x, ya, za, yb, zb, yc, zc, wo_ref, gpost):
    out = _dot((ya * _silu(za)).astype(BF16), wo_ref[0:D_NSA, :])
    out += _dot((yb * _silu(zb)).astype(BF16), wo_ref[D_NSA:D_NSA + D_POOL, :])
    out += _dot((yc * _silu(zc)).astype(BF16), wo_ref[D_NSA + D_POOL:, :])
    ms = jnp.mean(out * out, axis=-1, keepdims=True)
    return x + out * lax.rsqrt(ms + EPS) * gpost


def _merge_prompt_kernel(x_ref, ya_ref, za_ref, pin_ref, prev_ref, zb_ref, u_ref, v_ref, zc_ref,
                         pw_ref, ps_ref, gn_ref, ones_ref, ws_ref, bs_ref, wo_ref, gp_ref,
                         xo_ref, vn_ref, xext, *, tm, tiles_per_seq):
    i = pl.program_id(0)
    first = (i % tiles_per_seq) == 0
    pin = pin_ref[...]
    xext[POOL_MAX:, :] = pin
    xext[0:POOL_MAX, :] = jnp.where(first, 0.0, prev_ref[...])
    w_lane = _pool_window_lane((tm, D_POOL))
    acc = pin
    for k in range(1, POOL_MAX):
        acc = acc + jnp.where(w_lane > k, xext[pl.ds(POOL_MAX - k, tm), :], 0.0)
    pos = (i % tiles_per_seq) * tm + lax.broadcasted_iota(jnp.int32, (tm, D_POOL), 0)
    cnt = jnp.minimum(w_lane, pos + 1).astype(F32)
    diff = acc / cnt - pin
    yb = _dot(diff.astype(BF16), pw_ref[...]) * ps_ref[...]

    v = v_ref[...]
    vn = v * lax.rsqrt(_group_rms(v, ones_ref[...]) + EPS) * gn_ref[...]
    vn_ref[...] = vn
    vnb = vn.astype(BF16)
    lane = lax.broadcasted_iota(jnp.int32, (CHUNK, D_GMLP), 1)
    chunks = []
    for c in range(tm // CHUNK):
        vc = vnb[c * CHUNK:(c + 1) * CHUNK]
        s = _dot(ws_ref[0], vc)
        for g in range(1, D_GMLP // GMLP_GROUP_DIM):
            s = jnp.where(lane >= g * GMLP_GROUP_DIM, _dot(ws_ref[g], vc), s)
        chunks.append(s + bs_ref[...])
    yc = u_ref[...] * jnp.concatenate(chunks, axis=0)

    xo_ref[...] = _mix_out(x_ref[...], ya_ref[...], za_ref[...], yb, zb_ref[...], yc, zc_ref[...],
                           wo_ref, gp_ref[...])


def _merge_prompt(x2, ya, pr, wts, seq, tm):
    m = x2.shape[0]
    tiles_per_seq = seq // tm
    per = tm // POOL_MAX
    rows = lambda width: pl.BlockSpec((tm, width), lambda i: (i, 0))
    const = lambda shape: pl.BlockSpec(shape, lambda i: (0,) * len(shape))
    kern = functools.partial(_merge_prompt_kernel, tm=tm, tiles_per_seq=tiles_per_seq)
    return pl.pallas_call(
        kern,
        grid=(m // tm,),
        in_specs=[rows(D_MODEL), rows(D_NSA), rows(D_NSA), rows(D_POOL),
                  pl.BlockSpec((POOL_MAX, D_POOL), lambda i: (jnp.maximum(i * per - 1, 0), 0)),
                  rows(D_POOL), rows(D_GMLP), rows(D_GMLP), rows(D_GMLP),
                  const((D_POOL, D_POOL)), const((1, D_POOL)), const((1, D_GMLP)),
                  const((D_GMLP, D_GMLP)), const((4, CHUNK, CHUNK)), const((CHUNK, D_GMLP)),
                  const((D_MODEL, D_MODEL)), const((1, D_MODEL))],
        out_specs=[rows(D_MODEL), rows(D_GMLP)],
        out_shape=[jax.ShapeDtypeStruct((m, D_MODEL), F32), jax.ShapeDtypeStruct((m, D_GMLP), F32)],
        scratch_shapes=[pltpu.VMEM((tm + POOL_MAX, D_POOL), F32)],
        compiler_params=_params("parallel"),
        name="merge_prompt",
    )(x2, ya, pr["za"], pr["pin"], pr["pin"], pr["zb"], pr["u"], pr["v"], pr["zc"],
      wts["pool_w"], wts["pool_scale"], wts["gmlp_norm"], wts["ones_bd"], wts["ws_tril"], wts["bs_full"],
      wts["w_out"], wts["norm_post"])


def _merge_sample_kernel(x_ref, ya_ref, za_ref, pin_ref, st_ref, zb_ref, u_ref, v_ref, zc_ref,
                         pw_ref, ps_ref, gn_ref, ones_ref, w0_ref, b0_ref, wo_ref, gp_ref,
                         xo_ref, vn_ref):
    pin = pin_ref[...]
    w_lane = _pool_window_lane(pin.shape)
    acc = pin
    for k in range(1, POOL_MAX):
        acc = acc + jnp.where(w_lane > k, st_ref[POOL_MAX - 1 - k], 0.0)
    diff = acc / w_lane.astype(F32) - pin
    yb = _dot(diff.astype(BF16), pw_ref[...]) * ps_ref[...]

    v = v_ref[...]
    vn = v * lax.rsqrt(_group_rms(v, ones_ref[...]) + EPS) * gn_ref[...]
    vn_ref[...] = vn
    yc = u_ref[...] * (w0_ref[...] * vn + b0_ref[...])

    xo_ref[...] = _mix_out(x_ref[...], ya_ref[...], za_ref[...], yb, zb_ref[...], yc, zc_ref[...],
                           wo_ref, gp_ref[...])


def _merge_sample(x2, ya, pr, state_t, wts):
    db = x2.shape[0]
    full = lambda shape: pl.BlockSpec(shape, lambda i: (0,) * len(shape))
    args = (x2, ya, pr["za"], pr["pin"], state_t, pr["zb"], pr["u"], pr["v"], pr["zc"],
            wts["pool_w"], wts["pool_scale"], wts["gmlp_norm"], wts["ones_bd"], wts["w0"], wts["b0"],
            wts["w_out"], wts["norm_post"])
    return pl.pallas_call(
        _merge_sample_kernel,
        grid=(1,),
        in_specs=[full(a.shape) for a in args],
        out_specs=[full((db, D_MODEL)), full((db, D_GMLP))],
        out_shape=[jax.ShapeDtypeStruct((db, D_MODEL), F32), jax.ShapeDtypeStruct((db, D_GMLP), F32)],
        compiler_params=_params("arbitrary"),
        name="merge_sample",
    )(*args)


def _layer_weights(l, norm_pre, w_in, cmp_pe, cmp_w1, cmp_w2, pool_w, pool_scale, gmlp_norm, gmlp_ws, gmlp_bs,
                   w_out, norm_post):
    n_g = D_GMLP // GMLP_GROUP_DIM
    ones_bd = jnp.kron(jnp.eye(n_g, dtype=F32), jnp.ones((GMLP_GROUP_DIM, GMLP_GROUP_DIM), F32)).astype(BF16)
    pw = pool_w[l]
    zp = jnp.zeros_like(pw[0])
    pool_bd = jnp.concatenate(
        [jnp.concatenate([pw[i] if j == i else zp for j in range(4)], axis=1) for i in range(4)], axis=0)
    tril = jnp.tril(jnp.ones((CHUNK, CHUNK), F32))
    return dict(
        norm_pre=norm_pre[l].reshape(1, D_MODEL),
        w_in=_prep_w_in(w_in[l]),
        pe_t=_pe_t(cmp_pe[l]),
        w1=_blockdiag4(cmp_w1[l]).astype(BF16),
        w2=_blockdiag4(cmp_w2[l]).astype(BF16),
        pool_w=pool_bd.astype(BF16),
        pool_scale=pool_scale[l].reshape(1, D_POOL),
        gmlp_norm=gmlp_norm[l].reshape(1, D_GMLP),
        ones_bd=ones_bd,
        ws_tril=(gmlp_ws[l] * tril).astype(BF16),
        bs_full=jnp.repeat(gmlp_bs[l].T, GMLP_GROUP_DIM, axis=1),
        w0=jnp.repeat(gmlp_ws[l][:, 0, 0], GMLP_GROUP_DIM).reshape(1, D_GMLP),
        b0=jnp.repeat(gmlp_bs[l][:, 0], GMLP_GROUP_DIM).reshape(1, D_GMLP),
        w_out=w_out[l].astype(BF16),
        norm_post=norm_post[l].reshape(1, D_MODEL),
    )


def _largest_tile(n, cap):
    t = cap
    while n % t:
        t //= 2
    return t


def _feature_major(cache):
    d, n, t = cache.shape[:3]
    return jnp.transpose(cache, (0, 1, 3, 4, 5, 2)).reshape(d, n, D_KV, t)


def _token_major(x_t):
    d, n, _, t = x_t.shape
    return jnp.transpose(x_t.reshape(d, n, 2, N_KV, HEAD_DIM, t), (0, 1, 5, 2, 3, 4))


def kernel(x_prompt, x_sample, cache_kv_cmp, cache_kv_sel, cache_kv_win, state_pool, page_table, norm_pre, w_in,
           cmp_pe, cmp_w1, cmp_w2, pool_w, pool_scale, gmlp_norm, gmlp_ws, gmlp_bs, w_out, norm_post):
    bp, tp, _ = x_prompt.shape
    db, ts, _ = x_sample.shape
    depth = w_in.shape[0]
    n_pages = page_table.shape[1]
    past_len = n_pages * PAGE_SIZE
    win_keep = cache_kv_win.shape[2]
    assert ts == 1 and tp % CHUNK == 0 and tp <= CMP_BLOCK * CMP_BLOCK and tp >= WINDOW
    assert past_len // CMP_BLOCK + 1 <= CMP_BLOCK and win_keep == WINDOW and db % LANES == 0

    mp = bp * tp
    tm_proj = _largest_tile(tp, 512)
    tile_cmp = _largest_tile(tp, 1024)
    tm_merge = _largest_tile(tp, 512)
    tq, tk = 128, _largest_tile(tp, 512)
    assert tp >= WINDOW + tq

    cmp_t = _feature_major(cache_kv_cmp)
    sel_t = _feature_major(cache_kv_sel)
    win_t = _feature_major(cache_kv_win)
    pt_flat = page_table.reshape(-1).astype(jnp.int32)

    ext_prompt = _extra_rows(jnp.arange(tp, dtype=jnp.int32), True).T
    ext_s = jnp.transpose(
        _extra_rows(jnp.arange(past_len, dtype=jnp.int32), True).reshape(LANES, n_pages, PAGE_SIZE), (1, 0, 2))
    ext_w = _extra_rows(past_len - win_keep + jnp.arange(win_keep, dtype=jnp.int32), False)

    xp = x_prompt.reshape(mp, D_MODEL)
    xs = x_sample.reshape(db, D_MODEL)
    names = ("kvc_p", "kvc_s", "kvs_p", "kvs_s", "kvw_p", "kw_new", "pool_p", "pool_s", "gv_p", "gv_s")
    outs = {k: [] for k in names}
    for l in range(depth):
        wts = _layer_weights(l, norm_pre, w_in, cmp_pe, cmp_w1, cmp_w2, pool_w, pool_scale, gmlp_norm, gmlp_ws,
                             gmlp_bs, w_out, norm_post)
        pr = _project(xp, wts["norm_pre"], wts["w_in"], bp, tp, tm_proj, False)
        kvc = _compress_prompt(pr["kc_t"], wts, tile_cmp)
        ya = _attn_prompt(pr["q"], pr["gl"], kvc, pr["ks_t"], pr["kw_t"], ext_prompt, bp, tp, tq, tk)
        xp, vn = _merge_prompt(xp, ya, pr, wts, tp, tm_merge)
        outs["kvc_p"].append(pr["kc_t"])
        outs["kvs_p"].append(pr["ks_t"])
        outs["kvw_p"].append(pr["kw_t"][:, :, tp - WINDOW:])
        outs["pool_p"].append(pr["pin"].reshape(bp, tp, D_POOL)[:, tp - (POOL_MAX - 1):])
        outs["gv_p"].append(vn.reshape(bp, tp, D_GMLP)[:, tp - CHUNK:])
        pr = _project(xs, wts["norm_pre"], wts["w_in"], 1, db, db, True)
        kvc = _compress_sample(cmp_t, pt_flat, l, wts, db, n_pages)
        ya = _attn_sample(sel_t, win_t, pt_flat, l, pr, kvc, ext_s, ext_w, db, n_pages, win_keep)
        state_t = jnp.swapaxes(state_pool[l], 0, 1)
        xs, vn = _merge_sample(xs, ya, pr, state_t, wts)
        outs["kvc_s"].append(pr["kc_t"])
        outs["kvs_s"].append(pr["ks_t"])
        outs["kw_new"].append(pr["kw"].reshape(db, 1, D_KV))
        outs["pool_s"].append(jnp.concatenate([state_t[1:], pr["pin"][None]], axis=0))
        outs["gv_s"].append(vn.reshape(db, 1, D_GMLP))
    st = {k: jnp.stack(v) for k, v in outs.items()}
    kvw_s = _win_update(win_t, st["kw_new"], 4)
    sample_kv = lambda x_t: jnp.transpose(_token_major(x_t), (0, 2, 1, 3, 4, 5))
    return (xp.reshape(bp, tp, D_MODEL), xs.reshape(db, ts, D_MODEL),
            _token_major(st["kvc_p"]), sample_kv(st["kvc_s"]),
            _token_major(st["kvs_p"]), sample_kv(st["kvs_s"]),
            _token_major(st["kvw_p"]), _token_major(kvw_s),
            st["pool_p"], jnp.swapaxes(st["pool_s"], 1, 2), st["gv_p"], st["gv_s"])
```

```python
import functools

import jax
import jax.numpy as jnp
from jax import lax
from jax.experimental import pallas as pl
from jax.experimental.pallas import tpu as pltpu

F32 = jnp.float32
BF16 = jnp.bfloat16

D_MODEL = 1024
HEAD_DIM = 64
N_HEADS = 8
N_KV = 2
GROUP = N_HEADS // N_KV
D_NSA = N_HEADS * HEAD_DIM
D_KV = 2 * N_KV * HEAD_DIM
CMP_BLOCK = 64
N_SELECT = 16
WINDOW = 512
D_POOL = 256
POOL_WINDOWS = (2, 4, 8, 16)
POOL_GROUP_DIM = 64
POOL_MAX = 16
D_GMLP = 256
GMLP_GROUP_DIM = 64
CHUNK = 128
PAGE_SIZE = 128
EPS = 1e-6
NEG_INF = -1e30
SEL_FORCED = 1e4
SEL_INVALID = -1e4
Q_SCALE = HEAD_DIM ** -0.5

WIN_BLOCK = 128
SUB_KEYS = 256
LANES = 128
Q_WIDE = N_HEADS * LANES
VMEM_LIMIT = 56 * 1024 * 1024

ROW_ONE_A = 64
ROW_ONE_B = 65
ROW_POS_HI = 66
ROW_POS_LO = 67

_SEG = dict(q=(0, 512), kc=(512, 256), ks=(768, 256), kw=(1024, 256), gl=(1280, 24), za=(1304, 512),
            pin=(1816, 256), zb=(2072, 256), u=(2328, 256), v=(2584, 256), zc=(2840, 256))
_PROJ_OUT = (("q", Q_WIDE), ("kc", 256), ("ks", 256), ("kw", 256), ("za", 512), ("pin", 256),
             ("zb", 256), ("u", 256), ("v", 256), ("zc", 256), ("gl", LANES))
_KV_NAMES = ("kc", "ks", "kw")


def _nt_dot(a, b):
    return lax.dot_general(a, b, (((1,), (1,)), ((), ())), preferred_element_type=F32)


def _tn_dot(a, b):
    return lax.dot_general(a, b, (((0,), (0,)), ((), ())), preferred_element_type=F32)


def _dot(a, b):
    return jnp.dot(a, b, preferred_element_type=F32)


def _sigmoid(x):
    return 0.5 * jnp.tanh(0.5 * x) + 0.5


def _silu(x):
    return x * _sigmoid(x)


def _params(*sem):
    return pltpu.CompilerParams(dimension_semantics=sem, vmem_limit_bytes=VMEM_LIMIT)


def _row_to_col(row):
    eye = (lax.broadcasted_iota(jnp.int32, (LANES, LANES), 0)
           == lax.broadcasted_iota(jnp.int32, (LANES, LANES), 1))
    return jnp.sum(jnp.where(eye, row, 0.0), axis=1, keepdims=True)


def _proj_kernel(x_ref, g_ref, w_ref, *out_refs, emit_rows):
    x = x_ref[...]
    ms = jnp.mean(x * x, axis=-1, keepdims=True)
    xn = (x * lax.rsqrt(ms + EPS) * g_ref[...]).astype(BF16)
    refs = iter(out_refs)
    off = 0
    for name, width in _PROJ_OUT:
        res = _dot(xn, w_ref[:, off:off + width])
        off += width
        if name in _KV_NAMES:
            next(refs)[0] = res.T
            if not emit_rows:
                continue
        next(refs)[...] = res


def _project(x2, g, w, batch, seq, tm, emit_rows):
    m = x2.shape[0]
    n_tot = w.shape[1]
    tiles = seq // tm
    names, shapes, specs = [], [], []
    for name, width in _PROJ_OUT:
        if name in _KV_NAMES:
            names.append(name + "_t")
            shapes.append(jax.ShapeDtypeStruct((batch, width, seq), F32))
            specs.append(pl.BlockSpec((1, width, tm), lambda i: (i // tiles, 0, i % tiles)))
            if not emit_rows:
                continue
        names.append(name)
        shapes.append(jax.ShapeDtypeStruct((m, width), F32))
        specs.append(pl.BlockSpec((tm, width), lambda i: (i, 0)))
    res = pl.pallas_call(
        functools.partial(_proj_kernel, emit_rows=emit_rows),
        grid=(m // tm,),
        in_specs=[pl.BlockSpec((tm, D_MODEL), lambda i: (i, 0)),
                  pl.BlockSpec((1, D_MODEL), lambda i: (0, 0)),
                  pl.BlockSpec((D_MODEL, n_tot), lambda i: (0, 0))],
        out_specs=specs,
        out_shape=shapes,
        compiler_params=_params("parallel"),
        name="norm_project",
    )(x2, g, w)
    return dict(zip(names, res))


def _prep_w_in(w):
    def seg(name):
        o, n = _SEG[name]
        return w[:, o:o + n]
    wq = seg("q").reshape(D_MODEL, N_HEADS, HEAD_DIM)
    zeros = jnp.zeros_like(wq)
    lo = jnp.concatenate([wq, zeros], axis=-1)
    hi = jnp.concatenate([zeros, wq], axis=-1)
    first = (jnp.arange(N_HEADS) < GROUP)[None, :, None]
    q_wide = jnp.where(first, lo, hi).reshape(D_MODEL, Q_WIDE)
    gl = jnp.pad(seg("gl"), ((0, 0), (0, LANES - 3 * N_HEADS)))
    cols = [q_wide] + [seg(n) for n, _ in _PROJ_OUT[1:-1]] + [gl]
    return jnp.concatenate(cols, axis=1).astype(BF16)


def _blockdiag4(w):
    z = jnp.zeros((HEAD_DIM, HEAD_DIM), w.dtype)
    blocks = [w[0], w[0], w[1], w[1]]
    rows = [jnp.concatenate([blocks[i] if j == i else z for j in range(4)], axis=1) for i in range(4)]
    return jnp.concatenate(rows, axis=0)


def _pe_t(pe):
    one = jnp.concatenate([pe[0].T, pe[0].T, pe[1].T, pe[1].T], axis=0)
    return jnp.concatenate([one, one], axis=1)


def _compress_t(xt, pe128, w1, w2):
    n = xt.shape[1]
    pet = jnp.concatenate([pe128] * (n // LANES), axis=1)
    xb = (xt + pet).T.astype(BF16)
    hid = _silu(_dot(xb, w1))
    mean = jnp.sum(hid.reshape(n // CMP_BLOCK, CMP_BLOCK, D_KV), axis=1) * (1.0 / CMP_BLOCK)
    return _dot(mean.astype(BF16), w2)


def _compress_kernel(x_ref, pe_ref, w1_ref, w2_ref, o_ref):
    o_ref[...] = _compress_t(x_ref[0], pe_ref[...], w1_ref[...], w2_ref[...])


def _compress_prompt(kc_t, wts, tile):
    batch, _, seq = kc_t.shape
    tiles = seq // tile
    const = lambda shape: pl.BlockSpec(shape, lambda b, j: (0,) * len(shape))
    return pl.pallas_call(
        _compress_kernel,
        grid=(batch, tiles),
        in_specs=[pl.BlockSpec((1, D_KV, tile), lambda b, j: (b, 0, j)),
                  const((D_KV, LANES)), const((D_KV, D_KV)), const((D_KV, D_KV))],
        out_specs=pl.BlockSpec((tile // CMP_BLOCK, D_KV), lambda b, j: (b * tiles + j, 0)),
        out_shape=jax.ShapeDtypeStruct((batch * seq // CMP_BLOCK, D_KV), F32),
        compiler_params=_params("parallel", "parallel"),
        name="compress_prompt",
    )(kc_t, wts["pe_t"], wts["w1"], wts["w2"])


def _compress_pages_kernel(pt_ref, *refs, n_pages):
    del pt_ref
    page_refs = refs[:n_pages]
    pe_ref, w1_ref, w2_ref, o_ref = refs[n_pages:]
    xt = jnp.concatenate([r[0, 0] for r in page_refs], axis=1)
    o_ref[0] = _compress_t(xt, pe_ref[...], w1_ref[...], w2_ref[...])


def _page_spec(layer, p, n_pages):
    return pl.BlockSpec((1, 1, D_KV, PAGE_SIZE), lambda b, pt: (layer, pt[b * n_pages + p], 0, 0))


def _compress_sample(cache_t, pt_flat, layer, wts, db, n_pages):
    past_len = n_pages * PAGE_SIZE
    n_cmp = past_len // CMP_BLOCK
    const = lambda shape: pl.BlockSpec(shape, lambda b, pt: (0,) * len(shape))
    grid_spec = pltpu.PrefetchScalarGridSpec(
        num_scalar_prefetch=1,
        grid=(db,),
        in_specs=[_page_spec(layer, p, n_pages) for p in range(n_pages)]
        + [const((D_KV, LANES)), const((D_KV, D_KV)), const((D_KV, D_KV))],
        out_specs=pl.BlockSpec((1, n_cmp, D_KV), lambda b, pt: (b, 0, 0)),
    )
    return pl.pallas_call(
        functools.partial(_compress_pages_kernel, n_pages=n_pages),
        grid_spec=grid_spec,
        out_shape=jax.ShapeDtypeStruct((db, n_cmp, D_KV), F32),
        compiler_params=_params("parallel"),
        name="compress_sample",
    )(pt_flat, *([cache_t] * n_pages), wts["pe_t"], wts["w1"], wts["w2"])


def _extra_rows(pos, with_onehot):
    row = jnp.arange(LANES)[:, None]
    blk = (pos // CMP_BLOCK)[None, :]
    hi = ((pos // CMP_BLOCK) * CMP_BLOCK).astype(F32)[None, :]
    lo = (pos % CMP_BLOCK).astype(F32)[None, :]
    out = jnp.where((row == ROW_ONE_A) | (row == ROW_ONE_B), 1.0, 0.0) + jnp.zeros_like(hi)
    out = jnp.where(row == ROW_POS_HI, hi, out)
    out = jnp.where(row == ROW_POS_LO, lo, out)
    if with_onehot:
        out = jnp.where((row < CMP_BLOCK) & (row == blk), 1.0, out)
    return out.astype(BF16)


def _alibi_lanes(slope, t_hi, t_lo, shape):
    lane = lax.broadcasted_iota(jnp.int32, shape, 1)
    out = jnp.where(lane == ROW_ONE_A, -slope * t_hi, 0.0)
    out = jnp.where(lane == ROW_ONE_B, -slope * t_lo, out)
    return jnp.where((lane == ROW_POS_HI) | (lane == ROW_POS_LO), slope, out)


def _masked_softmax(s, mask, axis):
    s = jnp.where(mask, s, NEG_INF)
    p = jnp.exp(s - jnp.max(s, axis=axis, keepdims=True)) * mask.astype(F32)
    return p / jnp.maximum(jnp.sum(p, axis=axis, keepdims=True), 1e-30)


def _attn_prompt_kernel(q_ref, gl_ref, kvc_ref, ksk_ref, ksv_ref, kwk_ref, kwv_ref, ext_ref, o_ref,
                        kaug_s, kaug_w, v_s, v_w, s_buf, *, seq, tq, tk):
    i = pl.program_id(1)
    m = GROUP * tq
    n_cmp = seq // CMP_BLOCK

    n_wblk = WINDOW // WIN_BLOCK + tq // WIN_BLOCK

    @pl.when(i == 0)
    def _fill():
        for c in range(seq // tk):
            cols = slice(c * tk, (c + 1) * tk)
            kaug_s[c] = jnp.concatenate([ksk_ref[0, :, cols].T.astype(BF16), ext_ref[cols, :]], axis=1)
            v_s[c] = ksv_ref[0, :, cols].astype(BF16)
        for c in range(seq // WIN_BLOCK):
            cols = slice(c * WIN_BLOCK, (c + 1) * WIN_BLOCK)
            kaug_w[c] = jnp.concatenate([kwk_ref[0, :, cols].T.astype(BF16), ext_ref[cols, :]], axis=1)
            v_w[c] = kwv_ref[0, :, cols].astype(BF16)

    q0 = i * tq
    lane_t = q0 + lax.broadcasted_iota(jnp.int32, (1, m), 1) % tq
    t_hi = ((lane_t // CMP_BLOCK) * CMP_BLOCK).astype(F32)
    t_lo = (lane_t % CMP_BLOCK).astype(F32)
    ext_row = lax.broadcasted_iota(jnp.int32, (LANES, m), 0)
    key_row = lax.broadcasted_iota(jnp.int32, (tk, 1), 0)
    sig_t = _sigmoid(gl_ref[...]).T

    blk_row = lax.broadcasted_iota(jnp.int32, (n_cmp, m), 0)
    dist_c = lane_t - ((blk_row + 1) * CMP_BLOCK - 1)
    mask_c = dist_c >= 0
    blk2 = lax.broadcasted_iota(jnp.int32, (n_cmp, tq), 0)
    cur = (q0 + lax.broadcasted_iota(jnp.int32, (1, tq), 1)) // CMP_BLOCK
    forced = (blk2 == cur) | (blk2 == 0)
    started = blk2 <= cur

    def pieces_of(n):
        return [(a, min(a + SUB_KEYS, n)) for a in range(0, n, SUB_KEYS)]

    def qk_scores(qaugs, k_tile):
        return [[_dot(k_tile[a:b], qaug_t) for qaug_t in qaugs] for a, b in pieces_of(k_tile.shape[0])]

    def softmax_update(s, v_piece, carry, mask):
        m_i, l_i, acc = carry
        if mask is not None:
            s = jnp.where(mask, s, NEG_INF)
        m_new = jnp.maximum(m_i, jnp.max(s, axis=0, keepdims=True))
        alpha = jnp.exp(m_i - m_new)
        p = jnp.exp(s - m_new)
        l_new = alpha * l_i + jnp.sum(p, axis=0, keepdims=True)
        return m_new, l_new, alpha * acc + _dot(v_piece, p.astype(BF16))

    def softmax_pv(scores, v_tile, carries, mask_fn):
        carries = list(carries)
        for (a, b), piece_scores in zip(pieces_of(v_tile.shape[1]), scores):
            for kv, s in enumerate(piece_scores):
                carries[kv] = softmax_update(s, v_tile[:, a:b], carries[kv], mask_fn(a, b))
        return tuple(carries)

    def sel_fill(buf, j):
        for p, piece_scores in enumerate(qk_scores(qaugs_s, kaug_s[j])):
            for kv, s in enumerate(piece_scores):
                s_buf[buf, p * N_KV + kv] = s

    def sel_tile(buf, j, carries, causal):
        carries = list(carries)
        for p, (a, b) in enumerate(pieces_of(tk)):
            mask = (j * tk + key_row[a:b] <= lane_t) if causal else None
            for kv in range(N_KV):
                carries[kv] = softmax_update(s_buf[buf, p * N_KV + kv], v_s[j, :, a:b], carries[kv], mask)
        return tuple(carries)

    init = (jnp.full((1, m), NEG_INF, F32), jnp.zeros((1, m), F32), jnp.zeros((LANES, m), F32))
    n_chunk = n_cmp // 8
    q4_ts, slope_rows, ali_ts, qaugs_w = [], [], [], []
    for kv in range(N_KV):
        qk = q_ref[:, kv * GROUP * LANES:(kv + 1) * GROUP * LANES]
        q4_t = jnp.concatenate([(qk[:, g * LANES:(g + 1) * LANES] * Q_SCALE).T for g in range(GROUP)],
                               axis=1).astype(BF16)
        slopes = [2.0 ** -(kv * GROUP + g + 1) for g in range(GROUP)]
        slope = jnp.concatenate([jnp.full((1, tq), s, F32) for s in slopes], axis=1)
        ali_t = jnp.where(ext_row == ROW_ONE_A, -slope * t_hi, 0.0)
        ali_t = jnp.where(ext_row == ROW_ONE_B, -slope * t_lo, ali_t)
        ali_t = jnp.where((ext_row == ROW_POS_HI) | (ext_row == ROW_POS_LO), slope, ali_t)
        q4_ts.append(q4_t)
        slope_rows.append(slope)
        ali_ts.append(ali_t)
        qaugs_w.append(jnp.concatenate([q4_t, ali_t.astype(BF16)], axis=0))

    c0 = jnp.maximum(q0 // WIN_BLOCK - WINDOW // WIN_BLOCK, 0)
    k_win = jnp.concatenate([kaug_w[c0 + t] for t in range(n_wblk)], axis=0)
    v_win = jnp.concatenate([v_w[c0 + t] for t in range(n_wblk)], axis=1)
    scores_w = qk_scores(qaugs_w, k_win)

    o_cs, qaugs_s = [], []
    for kv in range(N_KV):
        q4_t, slope, ali_t = q4_ts[kv], slope_rows[kv], ali_ts[kv]
        kvc = kvc_ref[...]
        s_ct = _dot(kvc[:, :LANES].astype(BF16), q4_t) - slope * dist_c.astype(F32)
        p_ct = _masked_softmax(s_ct, mask_c, 0)
        o_c = _tn_dot(kvc[:, LANES:].astype(BF16), p_ct.astype(BF16))

        imp = p_ct[:, 0:tq]
        for g in range(1, GROUP):
            imp = imp + p_ct[:, g * tq:(g + 1) * tq]
        score = jnp.where(forced, SEL_FORCED, jnp.where(started, imp, SEL_INVALID))
        chunks = [score[8 * c:8 * c + 8] for c in range(n_chunk)]
        blk8 = lax.broadcasted_iota(jnp.int32, (8, tq), 0)
        ranks = [jnp.zeros((8, tq), F32) for _ in range(n_chunk)]
        for b in range(n_cmp):
            row = score[b:b + 1, :]
            for c in range(n_chunk):
                if 8 * c > b:
                    beats = row >= chunks[c]
                elif 8 * c + 7 < b:
                    beats = row > chunks[c]
                else:
                    beats = (row > chunks[c]) | ((row == chunks[c]) & (blk8 > b - 8 * c))
                ranks[c] = ranks[c] + jnp.where(beats, 1.0, 0.0)
        rank = jnp.concatenate(ranks, axis=0)
        sel_t = ((rank < N_SELECT) & (score > SEL_INVALID * 0.5)).astype(F32)
        sel_bias = (sel_t - 1.0) * -NEG_INF
        if n_cmp < LANES:
            sel_bias = jnp.concatenate([sel_bias, jnp.zeros((LANES - n_cmp, tq), F32)], axis=0)
        sel_bias = jnp.concatenate([sel_bias] * GROUP, axis=1)

        o_cs.append(o_c)
        qaugs_s.append(
            jnp.concatenate([q4_t, jnp.where(ext_row < CMP_BLOCK, sel_bias, ali_t).astype(BF16)], axis=0))

    n_kt = (q0 + tq + tk - 1) // tk
    sel_fill(0, 0)

    key_w = c0 * WIN_BLOCK + lax.broadcasted_iota(jnp.int32, (n_wblk * WIN_BLOCK, 1), 0)

    def win_mask(a, b):
        causal = key_w[a:b] <= lane_t
        if a >= WIN_BLOCK:
            return causal
        return causal & (lane_t - key_w[a:b] < WINDOW)
    o_ws = [acc / l for _, l, acc in softmax_pv(scores_w, v_win, (init,) * N_KV, win_mask)]

    n_full = n_kt - 1

    def sel_pair(jj, carries):
        j = 2 * jj
        sel_fill(1, j + 1)
        carries = sel_tile(0, j, carries, False)
        sel_fill(0, j + 2)
        return sel_tile(1, j + 1, carries, False)
    carries = lax.fori_loop(0, n_full // 2, sel_pair, (init,) * N_KV)
    j_a = 2 * (n_full // 2)
    odd = n_full - j_a
    sel_fill(1, n_kt - 1)
    carries = sel_tile(0, j_a, carries, True)
    carries = lax.cond(odd == 1, lambda c: sel_tile(1, n_kt - 1, c, True), lambda c: c, carries)
    o_ss = [acc / l for _, l, acc in carries]

    heads = []
    for kv in range(N_KV):
        rows = slice(kv * HEAD_DIM, (kv + 1) * HEAD_DIM)
        for g in range(GROUP):
            h = kv * GROUP + g
            cols = slice(g * tq, (g + 1) * tq)
            heads.append(sig_t[3 * h:3 * h + 1] * o_cs[kv][rows, cols]
                         + sig_t[3 * h + 1:3 * h + 2] * o_ss[kv][rows, cols]
                         + sig_t[3 * h + 2:3 * h + 3] * o_ws[kv][rows, cols])
    o_ref[...] = jnp.concatenate(heads, axis=0).T


def _attn_prompt(qw, gl, kvc, ks_t, kw_t, ext, batch, seq, tq, tk):
    nq = seq // tq
    n_cmp = seq // CMP_BLOCK
    n_kt = seq // tk
    kern = functools.partial(_attn_prompt_kernel, seq=seq, tq=tq, tk=tk)
    kv_spec = lambda half: pl.BlockSpec((1, LANES, seq), lambda b, i: (b, half, 0))
    return pl.pallas_call(
        kern,
        grid=(batch, nq),
        in_specs=[pl.BlockSpec((tq, Q_WIDE), lambda b, i: (b * nq + i, 0)),
                  pl.BlockSpec((tq, LANES), lambda b, i: (b * nq + i, 0)),
                  pl.BlockSpec((n_cmp, D_KV), lambda b, i: (b, 0)),
                  kv_spec(0), kv_spec(1), kv_spec(0), kv_spec(1),
                  pl.BlockSpec((seq, LANES), lambda b, i: (0, 0))],
        out_specs=pl.BlockSpec((tq, D_NSA), lambda b, i: (b * nq + i, 0)),
        out_shape=jax.ShapeDtypeStruct((batch * seq, D_NSA), F32),
        scratch_shapes=[pltpu.VMEM((n_kt, tk, 2 * LANES), BF16),
                        pltpu.VMEM((seq // WIN_BLOCK, WIN_BLOCK, 2 * LANES), BF16),
                        pltpu.VMEM((n_kt, LANES, tk), BF16),
                        pltpu.VMEM((seq // WIN_BLOCK, LANES, WIN_BLOCK), BF16),
                        pltpu.VMEM((2, N_KV * tk // SUB_KEYS, SUB_KEYS, GROUP * tq), F32)],
        compiler_params=_params("arbitrary", "arbitrary"),
        name="attn_prompt",
    )(qw, gl, kvc, ks_t, ks_t, kw_t, kw_t, ext)


def _attn_sample_kernel(pt_ref, *refs, n_pages, past_len, win_keep):
    del pt_ref
    page_refs = refs[:n_pages]
    (q_ref, gl_ref, ksn_ref, kwn_ref, kvc_ref, win_ref, exts_ref, extw_ref, o_ref) = refs[n_pages:]
    n_cmp = past_len // CMP_BLOCK
    cur = past_len // CMP_BLOCK
    t_hi = float(cur * CMP_BLOCK)
    t_lo = float(past_len % CMP_BLOCK)

    qrow = q_ref[0]
    q8f = jnp.concatenate([qrow[:, h * LANES:(h + 1) * LANES] for h in range(N_HEADS)], axis=0) * Q_SCALE
    q8 = q8f.astype(BF16)
    head = lax.broadcasted_iota(jnp.int32, (N_HEADS, 1), 0)
    slope = jnp.exp2(-(head + 1).astype(F32))
    lane8 = lax.broadcasted_iota(jnp.int32, (N_HEADS, LANES), 1)
    ali = _alibi_lanes(slope, t_hi, t_lo, (N_HEADS, LANES))

    kvc = kvc_ref[0]
    kc = kvc[:, :LANES]
    if n_cmp < LANES:
        kc = jnp.concatenate([kc, jnp.zeros((LANES - n_cmp, LANES), F32)], axis=0)
    blk = lax.broadcasted_iota(jnp.int32, (1, LANES), 1)
    dist_c = past_len - ((blk + 1) * CMP_BLOCK - 1)
    mask_c = (dist_c >= 0) & (blk < n_cmp)
    s_c = _nt_dot(q8, kc.astype(BF16)) - slope * dist_c.astype(F32)
    p_c = _masked_softmax(s_c, mask_c, 1)
    o_c = _dot(p_c[:, :n_cmp].astype(BF16), kvc[:, LANES:].astype(BF16))

    forced = (blk == cur) | (blk == 0)
    started = blk <= cur
    below = (lax.broadcasted_iota(jnp.int32, (LANES, LANES), 0)
             < lax.broadcasted_iota(jnp.int32, (LANES, LANES), 1))
    sel_rows = []
    for kv in range(N_KV):
        imp = jnp.sum(p_c[kv * GROUP:(kv + 1) * GROUP], axis=0, keepdims=True)
        score = jnp.where(forced, SEL_FORCED, jnp.where(started, imp, SEL_INVALID))
        col = _row_to_col(score)
        beats = (col > score) | ((col == score) & below)
        rank = jnp.sum(beats.astype(F32), axis=0, keepdims=True)
        sel = ((rank < N_SELECT) & (score > SEL_INVALID * 0.5)).astype(F32)
        sel_rows += [sel] * GROUP
    sel_bias = (jnp.concatenate(sel_rows, axis=0) - 1.0) * -NEG_INF
    qaug_s = jnp.concatenate([q8, jnp.where(lane8 < CMP_BLOCK, sel_bias, ali).astype(BF16)], axis=1)
    qaug_w = jnp.concatenate([q8, ali.astype(BF16)], axis=1)

    def new_key_score(row_ref):
        k_new = row_ref[0][:, :LANES].astype(BF16).astype(F32)
        return jnp.sum(q8.astype(F32) * k_new, axis=1, keepdims=True)

    def finish(scores, values, s_new, v_new):
        m = s_new
        for s in scores:
            m = jnp.maximum(m, jnp.max(s, axis=1, keepdims=True))
        p_new = jnp.exp(s_new - m)
        l = p_new
        acc = p_new * v_new.astype(BF16).astype(F32)
        for s, v in zip(scores, values):
            p = jnp.exp(s - m)
            l = l + jnp.sum(p, axis=1, keepdims=True)
            acc = acc + _nt_dot(p.astype(BF16), v)
        return acc / l

    scores, values = [], []
    for p, ref in enumerate(page_refs):
        page = ref[0, 0]
        kaug = jnp.concatenate([page[:LANES].astype(BF16), exts_ref[p]], axis=0)
        scores.append(_dot(qaug_s, kaug))
        values.append(page[LANES:].astype(BF16))
    o_s = finish(scores, values, new_key_score(ksn_ref), ksn_ref[0][:, LANES:])

    win = win_ref[0, 0]
    kaug = jnp.concatenate([win[:LANES].astype(BF16), extw_ref[...]], axis=0)
    s_w = _dot(qaug_w, kaug)
    dist_w = win_keep - lax.broadcasted_iota(jnp.int32, (1, win_keep), 1)
    s_w = jnp.where((dist_w >= 0) & (dist_w < WINDOW), s_w, NEG_INF)
    o_w = finish([s_w], [win[LANES:].astype(BF16)], new_key_score(kwn_ref), kwn_ref[0][:, LANES:])

    sig = _sigmoid(gl_ref[0])
    gate = [jnp.sum(jnp.where(lane8 == 3 * head + j, sig, 0.0), axis=1, keepdims=True) for j in range(3)]
    out = gate[0] * o_c + gate[1] * o_s + gate[2] * o_w
    out = jnp.where(head < GROUP, out, pltpu.roll(out, HEAD_DIM, axis=1))
    o_ref[0] = out[:, :HEAD_DIM]


def _attn_sample(cache_t, win_t, pt_flat, layer, pr, kvc, ext_s, ext_w, db, n_pages, win_keep):
    past_len = n_pages * PAGE_SIZE
    n_cmp = past_len // CMP_BLOCK
    row = lambda width: pl.BlockSpec((1, 1, width), lambda b, pt: (b, 0, 0))
    grid_spec = pltpu.PrefetchScalarGridSpec(
        num_scalar_prefetch=1,
        grid=(db,),
        in_specs=[_page_spec(layer, p, n_pages) for p in range(n_pages)]
        + [row(Q_WIDE), row(LANES), row(D_KV), row(D_KV),
           pl.BlockSpec((1, n_cmp, D_KV), lambda b, pt: (b, 0, 0)),
           pl.BlockSpec((1, 1, D_KV, win_keep), lambda b, pt: (layer, b, 0, 0)),
           pl.BlockSpec((n_pages, LANES, PAGE_SIZE), lambda b, pt: (0, 0, 0)),
           pl.BlockSpec((LANES, win_keep), lambda b, pt: (0, 0))],
        out_specs=pl.BlockSpec((1, N_HEADS, HEAD_DIM), lambda b, pt: (b, 0, 0)),
    )
    kern = functools.partial(_attn_sample_kernel, n_pages=n_pages, past_len=past_len, win_keep=win_keep)
    out = pl.pallas_call(
        kern,
        grid_spec=grid_spec,
        out_shape=jax.ShapeDtypeStruct((db, N_HEADS, HEAD_DIM), F32),
        compiler_params=_params("parallel"),
        name="attn_sample",
    )(pt_flat, *([cache_t] * n_pages), pr["q"].reshape(db, 1, Q_WIDE), pr["gl"].reshape(db, 1, LANES),
      pr["ks"].reshape(db, 1, D_KV), pr["kw"].reshape(db, 1, D_KV), kvc, win_t, ext_s, ext_w)
    return out.reshape(db, D_NSA)


def _win_update_kernel(win_ref, new_ref, o_ref, *, bb, win_keep):
    lane = lax.broadcasted_iota(jnp.int32, (D_KV, win_keep), 1)
    for k in range(bb):
        row = new_ref[0, k]
        col = jnp.concatenate([_row_to_col(row[:, h * LANES:(h + 1) * LANES]) for h in range(D_KV // LANES)],
                              axis=0)
        shifted = pltpu.roll(win_ref[0, k], win_keep - 1, axis=1)
        o_ref[0, k] = jnp.where(lane == win_keep - 1, col, shifted)


def _win_update(win_t, kw_new, bb):
    depth, db, _, win_keep = win_t.shape
    return pl.pallas_call(
        functools.partial(_win_update_kernel, bb=bb, win_keep=win_keep),
        grid=(depth, db // bb),
        in_specs=[pl.BlockSpec((1, bb, D_KV, win_keep), lambda l, b: (l, b, 0, 0)),
                  pl.BlockSpec((1, bb, 1, D_KV), lambda l, b: (l, b, 0, 0))],
        out_specs=pl.BlockSpec((1, bb, D_KV, win_keep), lambda l, b: (l, b, 0, 0)),
        out_shape=jax.ShapeDtypeStruct(win_t.shape, F32),
        compiler_params=_params("parallel", "parallel"),
        name="win_update",
    )(win_t, kw_new)


def _pool_window_lane(shape):
    lane = lax.broadcasted_iota(jnp.int32, shape, len(shape) - 1)
    w = jnp.full(shape, POOL_WINDOWS[0], jnp.int32)
    for gi in range(1, len(POOL_WINDOWS)):
        w = jnp.where(lane >= gi * POOL_GROUP_DIM, POOL_WINDOWS[gi], w)
    return w


def _group_rms(v, ones_bd):
    sq = v * v
    hi = sq.astype(BF16)
    lo = (sq - hi.astype(F32)).astype(BF16)
    return (_dot(hi, ones_bd) + _dot(lo, ones_bd)) * (1.0 / GMLP_GROUP_DIM)


def _mix_out(x, ya, za, yb, zb, yc, zc, wo_ref, gpost):
    out = _dot((ya * _silu(za)).astype(BF16), wo_ref[0:D_NSA, :])
    out += _dot((yb * _silu(zb)).astype(BF16), wo_ref[D_NSA:D_NSA + D_POOL, :])
    out += _dot((yc * _silu(zc)).astype(BF16), wo_ref[D_NSA + D_POOL:, :])
    ms = jnp.mean(out * out, axis=-1, keepdims=True)
    return x + out * lax.rsqrt(ms + EPS) * gpost


def _merge_prompt_kernel(x_ref, ya_ref, za_ref, pin_ref, prev_ref, zb_ref, u_ref, v_ref, zc_ref,
                         pw_ref, ps_ref, gn_ref, ones_ref, ws_ref, bs_ref, wo_ref, gp_ref,
                         xo_ref, vn_ref, xext, *, tm, tiles_per_seq):
    i = pl.program_id(0)
    first = (i % tiles_per_seq) == 0
    pin = pin_ref[...]
    xext[POOL_MAX:, :] = pin
    xext[0:POOL_MAX, :] = jnp.where(first, 0.0, prev_ref[...])
    w_lane = _pool_window_lane((tm, D_POOL))
    acc = pin
    for k in range(1, POOL_MAX):
        acc = acc + jnp.where(w_lane > k, xext[pl.ds(POOL_MAX - k, tm), :], 0.0)
    pos = (i % tiles_per_seq) * tm + lax.broadcasted_iota(jnp.int32, (tm, D_POOL), 0)
    cnt = jnp.minimum(w_lane, pos + 1).astype(F32)
    diff = acc / cnt - pin
    yb = _dot(diff.astype(BF16), pw_ref[...]) * ps_ref[...]

    v = v_ref[...]
    vn = v * lax.rsqrt(_group_rms(v, ones_ref[...]) + EPS) * gn_ref[...]
    vn_ref[...] = vn
    vnb = vn.astype(BF16)
    lane = lax.broadcasted_iota(jnp.int32, (CHUNK, D_GMLP), 1)
    chunks = []
    for c in range(tm // CHUNK):
        vc = vnb[c * CHUNK:(c + 1) * CHUNK]
        s = _dot(ws_ref[0], vc)
        for g in range(1, D_GMLP // GMLP_GROUP_DIM):
            s = jnp.where(lane >= g * GMLP_GROUP_DIM, _dot(ws_ref[g], vc), s)
        chunks.append(s + bs_ref[...])
    yc = u_ref[...] * jnp.concatenate(chunks, axis=0)

    xo_ref[...] = _mix_out(x_ref[...], ya_ref[...], za_ref[...], yb, zb_ref[...], yc, zc_ref[...],
                           wo_ref, gp_ref[...])


def _merge_prompt(x2, ya, pr, wts, seq, tm):
    m = x2.shape[0]
    tiles_per_seq = seq // tm
    per = tm // POOL_MAX
    rows = lambda width: pl.BlockSpec((tm, width), lambda i: (i, 0))
    const = lambda shape: pl.BlockSpec(shape, lambda i: (0,) * len(shape))
    kern = functools.partial(_merge_prompt_kernel, tm=tm, tiles_per_seq=tiles_per_seq)
    return pl.pallas_call(
        kern,
        grid=(m // tm,),
        in_specs=[rows(D_MODEL), rows(D_NSA), rows(D_NSA), rows(D_POOL),
                  pl.BlockSpec((POOL_MAX, D_POOL), lambda i: (jnp.maximum(i * per - 1, 0), 0)),
                  rows(D_POOL), rows(D_GMLP), rows(D_GMLP), rows(D_GMLP),
                  const((D_POOL, D_POOL)), const((1, D_POOL)), const((1, D_GMLP)),
                  const((D_GMLP, D_GMLP)), const((4, CHUNK, CHUNK)), const((CHUNK, D_GMLP)),
                  const((D_MODEL, D_MODEL)), const((1, D_MODEL))],
        out_specs=[rows(D_MODEL), rows(D_GMLP)],
        out_shape=[jax.ShapeDtypeStruct((m, D_MODEL), F32), jax.ShapeDtypeStruct((m, D_GMLP), F32)],
        scratch_shapes=[pltpu.VMEM((tm + POOL_MAX, D_POOL), F32)],
        compiler_params=_params("parallel"),
        name="merge_prompt",
    )(x2, ya, pr["za"], pr["pin"], pr["pin"], pr["zb"], pr["u"], pr["v"], pr["zc"],
      wts["pool_w"], wts["pool_scale"], wts["gmlp_norm"], wts["ones_bd"], wts["ws_tril"], wts["bs_full"],
      wts["w_out"], wts["norm_post"])


def _merge_sample_kernel(x_ref, ya_ref, za_ref, pin_ref, st_ref, zb_ref, u_ref, v_ref, zc_ref,
                         pw_ref, ps_ref, gn_ref, ones_ref, w0_ref, b0_ref, wo_ref, gp_ref,
                         xo_ref, vn_ref):
    pin = pin_ref[...]
    w_lane = _pool_window_lane(pin.shape)
    acc = pin
    for k in range(1, POOL_MAX):
        acc = acc + jnp.where(w_lane > k, st_ref[POOL_MAX - 1 - k], 0.0)
    diff = acc / w_lane.astype(F32) - pin
    yb = _dot(diff.astype(BF16), pw_ref[...]) * ps_ref[...]

    v = v_ref[...]
    vn = v * lax.rsqrt(_group_rms(v, ones_ref[...]) + EPS) * gn_ref[...]
    vn_ref[...] = vn
    yc = u_ref[...] * (w0_ref[...] * vn + b0_ref[...])

    xo_ref[...] = _mix_out(x_ref[...], ya_ref[...], za_ref[...], yb, zb_ref[...], yc, zc_ref[...],
                           wo_ref, gp_ref[...])


def _merge_sample(x2, ya, pr, state_t, wts):
    db = x2.shape[0]
    full = lambda shape: pl.BlockSpec(shape, lambda i: (0,) * len(shape))
    args = (x2, ya, pr["za"], pr["pin"], state_t, pr["zb"], pr["u"], pr["v"], pr["zc"],
            wts["pool_w"], wts["pool_scale"], wts["gmlp_norm"], wts["ones_bd"], wts["w0"], wts["b0"],
            wts["w_out"], wts["norm_post"])
    return pl.pallas_call(
        _merge_sample_kernel,
        grid=(1,),
        in_specs=[full(a.shape) for a in args],
        out_specs=[full((db, D_MODEL)), full((db, D_GMLP))],
        out_shape=[jax.ShapeDtypeStruct((db, D_MODEL), F32), jax.ShapeDtypeStruct((db, D_GMLP), F32)],
        compiler_params=_params("arbitrary"),
        name="merge_sample",
    )(*args)


def _layer_weights(l, norm_pre, w_in, cmp_pe, cmp_w1, cmp_w2, pool_w, pool_scale, gmlp_norm, gmlp_ws, gmlp_bs,
                   w_out, norm_post):
    n_g = D_GMLP // GMLP_GROUP_DIM
    ones_bd = jnp.kron(jnp.eye(n_g, dtype=F32), jnp.ones((GMLP_GROUP_DIM, GMLP_GROUP_DIM), F32)).astype(BF16)
    pw = pool_w[l]
    zp = jnp.zeros_like(pw[0])
    pool_bd = jnp.concatenate(
        [jnp.concatenate([pw[i] if j == i else zp for j in range(4)], axis=1) for i in range(4)], axis=0)
    tril = jnp.tril(jnp.ones((CHUNK, CHUNK), F32))
    return dict(
        norm_pre=norm_pre[l].reshape(1, D_MODEL),
        w_in=_prep_w_in(w_in[l]),
        pe_t=_pe_t(cmp_pe[l]),
        w1=_blockdiag4(cmp_w1[l]).astype(BF16),
        w2=_blockdiag4(cmp_w2[l]).astype(BF16),
        pool_w=pool_bd.astype(BF16),
        pool_scale=pool_scale[l].reshape(1, D_POOL),
        gmlp_norm=gmlp_norm[l].reshape(1, D_GMLP),
        ones_bd=ones_bd,
        ws_tril=(gmlp_ws[l] * tril).astype(BF16),
        bs_full=jnp.repeat(gmlp_bs[l].T, GMLP_GROUP_DIM, axis=1),
        w0=jnp.repeat(gmlp_ws[l][:, 0, 0], GMLP_GROUP_DIM).reshape(1, D_GMLP),
        b0=jnp.repeat(gmlp_bs[l][:, 0], GMLP_GROUP_DIM).reshape(1, D_GMLP),
        w_out=w_out[l].astype(BF16),
        norm_post=norm_post[l].reshape(1, D_MODEL),
    )


def _largest_tile(n, cap):
    t = cap
    while n % t:
        t //= 2
    return t


def _feature_major(cache):
    d, n, t = cache.shape[:3]
    return jnp.transpose(cache, (0, 1, 3, 4, 5, 2)).reshape(d, n, D_KV, t)


def _token_major(x_t):
    d, n, _, t = x_t.shape
    return jnp.transpose(x_t.reshape(d, n, 2, N_KV, HEAD_DIM, t), (0, 1, 5, 2, 3, 4))


def kernel(x_prompt, x_sample, cache_kv_cmp, cache_kv_sel, cache_kv_win, state_pool, page_table, norm_pre, w_in,
           cmp_pe, cmp_w1, cmp_w2, pool_w, pool_scale, gmlp_norm, gmlp_ws, gmlp_bs, w_out, norm_post):
    bp, tp, _ = x_prompt.shape
    db, ts, _ = x_sample.shape
    depth = w_in.shape[0]
    n_pages = page_table.shape[1]
    past_len = n_pages * PAGE_SIZE
    win_keep = cache_kv_win.shape[2]
    assert ts == 1 and tp % CHUNK == 0 and tp <= CMP_BLOCK * CMP_BLOCK and tp >= WINDOW
    assert past_len // CMP_BLOCK + 1 <= CMP_BLOCK and win_keep == WINDOW and db % LANES == 0

    mp = bp * tp
    tm_proj = _largest_tile(tp, 512)
    tile_cmp = _largest_tile(tp, 1024)
    tm_merge = _largest_tile(tp, 512)
    tq, tk = 128, _largest_tile(tp, 512)
    assert tp >= WINDOW + tq

    cmp_t = _feature_major(cache_kv_cmp)
    sel_t = _feature_major(cache_kv_sel)
    win_t = _feature_major(cache_kv_win)
    pt_flat = page_table.reshape(-1).astype(jnp.int32)

    ext_prompt = _extra_rows(jnp.arange(tp, dtype=jnp.int32), True).T
    ext_s = jnp.transpose(
        _extra_rows(jnp.arange(past_len, dtype=jnp.int32), True).reshape(LANES, n_pages, PAGE_SIZE), (1, 0, 2))
    ext_w = _extra_rows(past_len - win_keep + jnp.arange(win_keep, dtype=jnp.int32), False)

    xp = x_prompt.reshape(mp, D_MODEL)
    xs = x_sample.reshape(db, D_MODEL)
    names = ("kvc_p", "kvc_s", "kvs_p", "kvs_s", "kvw_p", "kw_new", "pool_p", "pool_s", "gv_p", "gv_s")
    outs = {k: [] for k in names}
    for l in range(depth):
        wts = _layer_weights(l, norm_pre, w_in, cmp_pe, cmp_w1, cmp_w2, pool_w, pool_scale, gmlp_norm, gmlp_ws,
                             gmlp_bs, w_out, norm_post)
        pr = _project(xp, wts["norm_pre"], wts["w_in"], bp, tp, tm_proj, False)
        kvc = _compress_prompt(pr["kc_t"], wts, tile_cmp)
        ya = _attn_prompt(pr["q"], pr["gl"], kvc, pr["ks_t"], pr["kw_t"], ext_prompt, bp, tp, tq, tk)
        xp, vn = _merge_prompt(xp, ya, pr, wts, tp, tm_merge)
        outs["kvc_p"].append(pr["kc_t"])
        outs["kvs_p"].append(pr["ks_t"])
        outs["kvw_p"].append(pr["kw_t"][:, :, tp - WINDOW:])
        outs["pool_p"].append(pr["pin"].reshape(bp, tp, D_POOL)[:, tp - (POOL_MAX - 1):])
        outs["gv_p"].append(vn.reshape(bp, tp, D_GMLP)[:, tp - CHUNK:])
        pr = _project(xs, wts["norm_pre"], wts["w_in"], 1, db, db, True)
        kvc = _compress_sample(cmp_t, pt_flat, l, wts, db, n_pages)
        ya = _attn_sample(sel_t, win_t, pt_flat, l, pr, kvc, ext_s, ext_w, db, n_pages, win_keep)
        state_t = jnp.swapaxes(state_pool[l], 0, 1)
        xs, vn = _merge_sample(xs, ya, pr, state_t, wts)
        outs["kvc_s"].append(pr["kc_t"])
        outs["kvs_s"].append(pr["ks_t"])
        outs["kw_new"].append(pr["kw"].reshape(db, 1, D_KV))
        outs["pool_s"].append(jnp.concatenate([state_t[1:], pr["pin"][None]], axis=0))
        outs["gv_s"].append(vn.reshape(db, 1, D_GMLP))
    st = {k: jnp.stack(v) for k, v in outs.items()}
    kvw_s = _win_update(win_t, st["kw_new"], 4)
    sample_kv = lambda x_t: jnp.transpose(_token_major(x_t), (0, 2, 1, 3, 4, 5))
    return (xp.reshape(bp, tp, D_MODEL), xs.reshape(db, ts, D_MODEL),
            _token_major(st["kvc_p"]), sample_kv(st["kvc_s"]),
            _token_major(st["kvs_p"]), sample_kv(st["kvs_s"]),
            _token_major(st["kvw_p"]), _token_major(kvw_s),
            st["pool_p"], jnp.swapaxes(st["pool_s"], 1, 2), st["gv_p"], st["gv_s"])
```

```python
import functools

import jax
import jax.numpy as jnp
from jax import lax
from jax.experimental import pallas as pl
from jax.experimental.pallas import tpu as pltpu

F32 = jnp.float32
BF16 = jnp.bfloat16

D_MODEL = 1024
HEAD_DIM = 64
N_HEADS = 8
N_KV = 2
GROUP = N_HEADS // N_KV
D_NSA = N_HEADS * HEAD_DIM
D_KV = 2 * N_KV * HEAD_DIM
CMP_BLOCK = 64
N_SELECT = 16
WINDOW = 512
D_POOL = 256
POOL_WINDOWS = (2, 4, 8, 16)
POOL_GROUP_DIM = 64
POOL_MAX = 16
D_GMLP = 256
GMLP_GROUP_DIM = 64
CHUNK = 128
PAGE_SIZE = 128
EPS = 1e-6
NEG_INF = -1e30
SEL_FORCED = 1e4
SEL_INVALID = -1e4
Q_SCALE = HEAD_DIM ** -0.5

WIN_BLOCK = 128
SUB_KEYS = 256
LANES = 128
Q_WIDE = N_HEADS * LANES
VMEM_LIMIT = 56 * 1024 * 1024

ROW_ONE_A = 64
ROW_ONE_B = 65
ROW_POS_HI = 66
ROW_POS_LO = 67

_SEG = dict(q=(0, 512), kc=(512, 256), ks=(768, 256), kw=(1024, 256), gl=(1280, 24), za=(1304, 512),
            pin=(1816, 256), zb=(2072, 256), u=(2328, 256), v=(2584, 256), zc=(2840, 256))
_PROJ_OUT = (("q", Q_WIDE), ("kc", 256), ("ks", 256), ("kw", 256), ("za", 512), ("pin", 256),
             ("zb", 256), ("u", 256), ("v", 256), ("zc", 256), ("gl", LANES))
_KV_NAMES = ("kc", "ks", "kw")


def _nt_dot(a, b):
    return lax.dot_general(a, b, (((1,), (1,)), ((), ())), preferred_element_type=F32)


def _tn_dot(a, b):
    return lax.dot_general(a, b, (((0,), (0,)), ((), ())), preferred_element_type=F32)


def _dot(a, b):
    return jnp.dot(a, b, preferred_element_type=F32)


def _sigmoid(x):
    return 0.5 * jnp.tanh(0.5 * x) + 0.5


def _silu(x):
    return x * _sigmoid(x)


def _params(*sem):
    return pltpu.CompilerParams(dimension_semantics=sem, vmem_limit_bytes=VMEM_LIMIT)


def _row_to_col(row):
    eye = (lax.broadcasted_iota(jnp.int32, (LANES, LANES), 0)
           == lax.broadcasted_iota(jnp.int32, (LANES, LANES), 1))
    return jnp.sum(jnp.where(eye, row, 0.0), axis=1, keepdims=True)


def _proj_kernel(x_ref, g_ref, w_ref, *out_refs, emit_rows):
    x = x_ref[...]
    ms = jnp.mean(x * x, axis=-1, keepdims=True)
    xn = (x * lax.rsqrt(ms + EPS) * g_ref[...]).astype(BF16)
    refs = iter(out_refs)
    off = 0
    for name, width in _PROJ_OUT:
        res = _dot(xn, w_ref[:, off:off + width])
        off += width
        if name in _KV_NAMES:
            next(refs)[0] = res.T
            if not emit_rows:
                continue
        next(refs)[...] = res


def _project(x2, g, w, batch, seq, tm, emit_rows):
    m = x2.shape[0]
    n_tot = w.shape[1]
    tiles = seq // tm
    names, shapes, specs = [], [], []
    for name, width in _PROJ_OUT:
        if name in _KV_NAMES:
            names.append(name + "_t")
            shapes.append(jax.ShapeDtypeStruct((batch, width, seq), F32))
            specs.append(pl.BlockSpec((1, width, tm), lambda i: (i // tiles, 0, i % tiles)))
            if not emit_rows:
                continue
        names.append(name)
        shapes.append(jax.ShapeDtypeStruct((m, width), F32))
        specs.append(pl.BlockSpec((tm, width), lambda i: (i, 0)))
    res = pl.pallas_call(
        functools.partial(_proj_kernel, emit_rows=emit_rows),
        grid=(m // tm,),
        in_specs=[pl.BlockSpec((tm, D_MODEL), lambda i: (i, 0)),
                  pl.BlockSpec((1, D_MODEL), lambda i: (0, 0)),
                  pl.BlockSpec((D_MODEL, n_tot), lambda i: (0, 0))],
        out_specs=specs,
        out_shape=shapes,
        compiler_params=_params("parallel"),
        name="norm_project",
    )(x2, g, w)
    return dict(zip(names, res))


def _prep_w_in(w):
    def seg(name):
        o, n = _SEG[name]
        return w[:, o:o + n]
    wq = seg("q").reshape(D_MODEL, N_HEADS, HEAD_DIM)
    zeros = jnp.zeros_like(wq)
    lo = jnp.concatenate([wq, zeros], axis=-1)
    hi = jnp.concatenate([zeros, wq], axis=-1)
    first = (jnp.arange(N_HEADS) < GROUP)[None, :, None]
    q_wide = jnp.where(first, lo, hi).reshape(D_MODEL, Q_WIDE)
    gl = jnp.pad(seg("gl"), ((0, 0), (0, LANES - 3 * N_HEADS)))
    cols = [q_wide] + [seg(n) for n, _ in _PROJ_OUT[1:-1]] + [gl]
    return jnp.concatenate(cols, axis=1).astype(BF16)


def _blockdiag4(w):
    z = jnp.zeros((HEAD_DIM, HEAD_DIM), w.dtype)
    blocks = [w[0], w[0], w[1], w[1]]
    rows = [jnp.concatenate([blocks[i] if j == i else z for j in range(4)], axis=1) for i in range(4)]
    return jnp.concatenate(rows, axis=0)


def _pe_t(pe):
    one = jnp.concatenate([pe[0].T, pe[0].T, pe[1].T, pe[1].T], axis=0)
    return jnp.concatenate([one, one], axis=1)


def _compress_mean(xt, pe128, w1):
    n = xt.shape[1]
    pet = jnp.concatenate([pe128] * (n // LANES), axis=1)
    xb = (xt + pet).T.astype(BF16)
    hid = _silu(_dot(xb, w1))
    return jnp.sum(hid.reshape(n // CMP_BLOCK, CMP_BLOCK, D_KV), axis=1) * (1.0 / CMP_BLOCK)


def _compress_t(xt, pe128, w1, w2):
    return _dot(_compress_mean(xt, pe128, w1).astype(BF16), w2)


def _compress_kernel(x_ref, pe_ref, w1_ref, w2_ref, o_ref):
    o_ref[...] = _compress_t(x_ref[0], pe_ref[...], w1_ref[...], w2_ref[...])


def _compress_prompt(kc_t, wts, tile):
    batch, _, seq = kc_t.shape
    tiles = seq // tile
    const = lambda shape: pl.BlockSpec(shape, lambda b, j: (0,) * len(shape))
    return pl.pallas_call(
        _compress_kernel,
        grid=(batch, tiles),
        in_specs=[pl.BlockSpec((1, D_KV, tile), lambda b, j: (b, 0, j)),
                  const((D_KV, LANES)), const((D_KV, D_KV)), const((D_KV, D_KV))],
        out_specs=pl.BlockSpec((tile // CMP_BLOCK, D_KV), lambda b, j: (b * tiles + j, 0)),
        out_shape=jax.ShapeDtypeStruct((batch * seq // CMP_BLOCK, D_KV), F32),
        compiler_params=_params("parallel", "parallel"),
        name="compress_prompt",
    )(kc_t, wts["pe_t"], wts["w1"], wts["w2"])


def _extra_rows(pos, with_onehot):
    row = jnp.arange(LANES)[:, None]
    blk = (pos // CMP_BLOCK)[None, :]
    hi = ((pos // CMP_BLOCK) * CMP_BLOCK).astype(F32)[None, :]
    lo = (pos % CMP_BLOCK).astype(F32)[None, :]
    out = jnp.where((row == ROW_ONE_A) | (row == ROW_ONE_B), 1.0, 0.0) + jnp.zeros_like(hi)
    out = jnp.where(row == ROW_POS_HI, hi, out)
    out = jnp.where(row == ROW_POS_LO, lo, out)
    if with_onehot:
        out = jnp.where((row < CMP_BLOCK) & (row == blk), 1.0, out)
    return out.astype(BF16)


def _alibi_lanes(slope, t_hi, t_lo, shape):
    lane = lax.broadcasted_iota(jnp.int32, shape, 1)
    out = jnp.where(lane == ROW_ONE_A, -slope * t_hi, 0.0)
    out = jnp.where(lane == ROW_ONE_B, -slope * t_lo, out)
    return jnp.where((lane == ROW_POS_HI) | (lane == ROW_POS_LO), slope, out)


def _masked_softmax(s, mask, axis):
    s = jnp.where(mask, s, NEG_INF)
    p = jnp.exp(s - jnp.max(s, axis=axis, keepdims=True)) * mask.astype(F32)
    return p / jnp.maximum(jnp.sum(p, axis=axis, keepdims=True), 1e-30)


def _attn_prompt_kernel(q_ref, gl_ref, kvc_ref, ksk_ref, ksv_ref, kwk_ref, kwv_ref, ext_ref, o_ref,
                        kaug_s, kaug_w, v_s, v_w, s_buf, *, seq, tq, tk):
    i = pl.program_id(1)
    m = GROUP * tq
    n_cmp = seq // CMP_BLOCK

    n_wblk = WINDOW // WIN_BLOCK + tq // WIN_BLOCK

    @pl.when(i == 0)
    def _fill():
        for c in range(seq // tk):
            cols = slice(c * tk, (c + 1) * tk)
            kaug_s[c] = jnp.concatenate([ksk_ref[0, :, cols].T.astype(BF16), ext_ref[cols, :]], axis=1)
            v_s[c] = ksv_ref[0, :, cols].astype(BF16)
        for c in range(seq // WIN_BLOCK):
            cols = slice(c * WIN_BLOCK, (c + 1) * WIN_BLOCK)
            kaug_w[c] = jnp.concatenate([kwk_ref[0, :, cols].T.astype(BF16), ext_ref[cols, :]], axis=1)
            v_w[c] = kwv_ref[0, :, cols].astype(BF16)

    q0 = i * tq
    lane_t = q0 + lax.broadcasted_iota(jnp.int32, (1, m), 1) % tq
    t_hi = ((lane_t // CMP_BLOCK) * CMP_BLOCK).astype(F32)
    t_lo = (lane_t % CMP_BLOCK).astype(F32)
    ext_row = lax.broadcasted_iota(jnp.int32, (LANES, m), 0)
    key_row = lax.broadcasted_iota(jnp.int32, (tk, 1), 0)
    sig_t = _sigmoid(gl_ref[...]).T

    blk_row = lax.broadcasted_iota(jnp.int32, (n_cmp, m), 0)
    dist_c = lane_t - ((blk_row + 1) * CMP_BLOCK - 1)
    mask_c = dist_c >= 0
    blk2 = lax.broadcasted_iota(jnp.int32, (n_cmp, tq), 0)
    cur = (q0 + lax.broadcasted_iota(jnp.int32, (1, tq), 1)) // CMP_BLOCK
    forced = (blk2 == cur) | (blk2 == 0)
    started = blk2 <= cur

    def pieces_of(n):
        return [(a, min(a + SUB_KEYS, n)) for a in range(0, n, SUB_KEYS)]

    def qk_scores(qaugs, k_tile):
        return [[_dot(k_tile[a:b], qaug_t) for qaug_t in qaugs] for a, b in pieces_of(k_tile.shape[0])]

    def softmax_update(s, v_piece, carry, mask):
        m_i, l_i, acc = carry
        if mask is not None:
            s = jnp.where(mask, s, NEG_INF)
        m_new = jnp.maximum(m_i, jnp.max(s, axis=0, keepdims=True))
        alpha = jnp.exp(m_i - m_new)
        p = jnp.exp(s - m_new)
        l_new = alpha * l_i + jnp.sum(p, axis=0, keepdims=True)
        return m_new, l_new, alpha * acc + _dot(v_piece, p.astype(BF16))

    def softmax_pv(scores, v_tile, carries, mask_fn):
        carries = list(carries)
        for (a, b), piece_scores in zip(pieces_of(v_tile.shape[1]), scores):
            for kv, s in enumerate(piece_scores):
                carries[kv] = softmax_update(s, v_tile[:, a:b], carries[kv], mask_fn(a, b))
        return tuple(carries)

    def sel_fill(buf, j):
        for p, piece_scores in enumerate(qk_scores(qaugs_s, kaug_s[j])):
            for kv, s in enumerate(piece_scores):
                s_buf[buf, p * N_KV + kv] = s

    def sel_tile(buf, j, carries, causal):
        carries = list(carries)
        for p, (a, b) in enumerate(pieces_of(tk)):
            mask = (j * tk + key_row[a:b] <= lane_t) if causal else None
            for kv in range(N_KV):
                carries[kv] = softmax_update(s_buf[buf, p * N_KV + kv], v_s[j, :, a:b], carries[kv], mask)
        return tuple(carries)

    init = (jnp.full((1, m), NEG_INF, F32), jnp.zeros((1, m), F32), jnp.zeros((LANES, m), F32))
    n_chunk = n_cmp // 8
    q4_ts, slope_rows, ali_ts, qaugs_w = [], [], [], []
    for kv in range(N_KV):
        qk = q_ref[:, kv * GROUP * LANES:(kv + 1) * GROUP * LANES]
        q4_t = jnp.concatenate([(qk[:, g * LANES:(g + 1) * LANES] * Q_SCALE).T for g in range(GROUP)],
                               axis=1).astype(BF16)
        slopes = [2.0 ** -(kv * GROUP + g + 1) for g in range(GROUP)]
        slope = jnp.concatenate([jnp.full((1, tq), s, F32) for s in slopes], axis=1)
        ali_t = jnp.where(ext_row == ROW_ONE_A, -slope * t_hi, 0.0)
        ali_t = jnp.where(ext_row == ROW_ONE_B, -slope * t_lo, ali_t)
        ali_t = jnp.where((ext_row == ROW_POS_HI) | (ext_row == ROW_POS_LO), slope, ali_t)
        q4_ts.append(q4_t)
        slope_rows.append(slope)
        ali_ts.append(ali_t)
        qaugs_w.append(jnp.concatenate([q4_t, ali_t.astype(BF16)], axis=0))

    c0 = jnp.maximum(q0 // WIN_BLOCK - WINDOW // WIN_BLOCK, 0)
    k_win = jnp.concatenate([kaug_w[c0 + t] for t in range(n_wblk)], axis=0)
    v_win = jnp.concatenate([v_w[c0 + t] for t in range(n_wblk)], axis=1)
    scores_w = qk_scores(qaugs_w, k_win)

    o_cs, qaugs_s = [], []
    for kv in range(N_KV):
        q4_t, slope, ali_t = q4_ts[kv], slope_rows[kv], ali_ts[kv]
        kvc = kvc_ref[...]
        s_ct = _dot(kvc[:, :LANES].astype(BF16), q4_t) - slope * dist_c.astype(F32)
        p_ct = _masked_softmax(s_ct, mask_c, 0)
        o_c = _tn_dot(kvc[:, LANES:].astype(BF16), p_ct.astype(BF16))

        imp = p_ct[:, 0:tq]
        for g in range(1, GROUP):
            imp = imp + p_ct[:, g * tq:(g + 1) * tq]
        score = jnp.where(forced, SEL_FORCED, jnp.where(started, imp, SEL_INVALID))
        chunks = [score[8 * c:8 * c + 8] for c in range(n_chunk)]
        blk8 = lax.broadcasted_iota(jnp.int32, (8, tq), 0)
        ranks = [jnp.zeros((8, tq), F32) for _ in range(n_chunk)]
        for b in range(n_cmp):
            row = score[b:b + 1, :]
            for c in range(n_chunk):
                if 8 * c > b:
                    beats = row >= chunks[c]
                elif 8 * c + 7 < b:
                    beats = row > chunks[c]
                else:
                    beats = (row > chunks[c]) | ((row == chunks[c]) & (blk8 > b - 8 * c))
                ranks[c] = ranks[c] + jnp.where(beats, 1.0, 0.0)
        rank = jnp.concatenate(ranks, axis=0)
        sel_t = ((rank < N_SELECT) & (score > SEL_INVALID * 0.5)).astype(F32)
        sel_bias = (sel_t - 1.0) * -NEG_INF
        if n_cmp < LANES:
            sel_bias = jnp.concatenate([sel_bias, jnp.zeros((LANES - n_cmp, tq), F32)], axis=0)
        sel_bias = jnp.concatenate([sel_bias] * GROUP, axis=1)

        o_cs.append(o_c)
        qaugs_s.append(
            jnp.concatenate([q4_t, jnp.where(ext_row < CMP_BLOCK, sel_bias, ali_t).astype(BF16)], axis=0))

    n_kt = (q0 + tq + tk - 1) // tk
    sel_fill(0, 0)

    key_w = c0 * WIN_BLOCK + lax.broadcasted_iota(jnp.int32, (n_wblk * WIN_BLOCK, 1), 0)

    def win_mask(a, b):
        causal = key_w[a:b] <= lane_t
        if a >= WIN_BLOCK:
            return causal
        return causal & (lane_t - key_w[a:b] < WINDOW)
    o_ws = [acc / l for _, l, acc in softmax_pv(scores_w, v_win, (init,) * N_KV, win_mask)]

    n_full = n_kt - 1

    def sel_pair(jj, carries):
        j = 2 * jj
        sel_fill(1, j + 1)
        carries = sel_tile(0, j, carries, False)
        sel_fill(0, j + 2)
        return sel_tile(1, j + 1, carries, False)
    carries = lax.fori_loop(0, n_full // 2, sel_pair, (init,) * N_KV)
    j_a = 2 * (n_full // 2)
    odd = n_full - j_a
    sel_fill(1, n_kt - 1)
    carries = sel_tile(0, j_a, carries, True)
    carries = lax.cond(odd == 1, lambda c: sel_tile(1, n_kt - 1, c, True), lambda c: c, carries)
    o_ss = [acc / l for _, l, acc in carries]

    heads = []
    for kv in range(N_KV):
        rows = slice(kv * HEAD_DIM, (kv + 1) * HEAD_DIM)
        for g in range(GROUP):
            h = kv * GROUP + g
            cols = slice(g * tq, (g + 1) * tq)
            heads.append(sig_t[3 * h:3 * h + 1] * o_cs[kv][rows, cols]
                         + sig_t[3 * h + 1:3 * h + 2] * o_ss[kv][rows, cols]
                         + sig_t[3 * h + 2:3 * h + 3] * o_ws[kv][rows, cols])
    o_ref[...] = jnp.concatenate(heads, axis=0).T


def _attn_prompt(qw, gl, kvc, ks_t, kw_t, ext, batch, seq, tq, tk):
    nq = seq // tq
    n_cmp = seq // CMP_BLOCK
    n_kt = seq // tk
    kern = functools.partial(_attn_prompt_kernel, seq=seq, tq=tq, tk=tk)
    kv_spec = lambda half: pl.BlockSpec((1, LANES, seq), lambda b, i: (b, half, 0))
    return pl.pallas_call(
        kern,
        grid=(batch, nq),
        in_specs=[pl.BlockSpec((tq, Q_WIDE), lambda b, i: (b * nq + i, 0)),
                  pl.BlockSpec((tq, LANES), lambda b, i: (b * nq + i, 0)),
                  pl.BlockSpec((n_cmp, D_KV), lambda b, i: (b, 0)),
                  kv_spec(0), kv_spec(1), kv_spec(0), kv_spec(1),
                  pl.BlockSpec((seq, LANES), lambda b, i: (0, 0))],
        out_specs=pl.BlockSpec((tq, D_NSA), lambda b, i: (b * nq + i, 0)),
        out_shape=jax.ShapeDtypeStruct((batch * seq, D_NSA), F32),
        scratch_shapes=[pltpu.VMEM((n_kt, tk, 2 * LANES), BF16),
                        pltpu.VMEM((seq // WIN_BLOCK, WIN_BLOCK, 2 * LANES), BF16),
                        pltpu.VMEM((n_kt, LANES, tk), BF16),
                        pltpu.VMEM((seq // WIN_BLOCK, LANES, WIN_BLOCK), BF16),
                        pltpu.VMEM((2, N_KV * tk // SUB_KEYS, SUB_KEYS, GROUP * tq), F32)],
        compiler_params=_params("arbitrary", "arbitrary"),
        name="attn_prompt",
    )(qw, gl, kvc, ks_t, ks_t, kw_t, kw_t, ext)


def _attn_sample_kernel(pt_ref, *refs, n_pages, past_len, win_keep):
    del pt_ref
    cmp_refs = refs[:n_pages]
    page_refs = refs[n_pages:2 * n_pages]
    (q_ref, gl_ref, ksn_ref, kwn_ref, win_ref, exts_ref, extw_ref, pe_ref, w1_ref, w2_ref,
     o_ref, kvc_buf, mean_buf) = refs[2 * n_pages:]
    n_cmp = past_len // CMP_BLOCK
    per_page = PAGE_SIZE // CMP_BLOCK

    @pl.when(pl.program_id(0) == 0)
    def _first():
        kvc_buf[...] = jnp.zeros(kvc_buf.shape, F32)

    kvc = kvc_buf[...]

    todo = list(range(n_pages))

    def compress_pages(count):
        for _ in range(min(count, len(todo))):
            p = todo.pop(0)
            mean_buf[p * per_page:(p + 1) * per_page, :] = _compress_mean(
                cmp_refs[p][0, 0], pe_ref[...], w1_ref[...])
    cur = past_len // CMP_BLOCK
    t_hi = float(cur * CMP_BLOCK)
    t_lo = float(past_len % CMP_BLOCK)

    qrow = q_ref[0]
    q8f = jnp.concatenate([qrow[:, h * LANES:(h + 1) * LANES] for h in range(N_HEADS)], axis=0) * Q_SCALE
    q8 = q8f.astype(BF16)
    head = lax.broadcasted_iota(jnp.int32, (N_HEADS, 1), 0)
    slope = jnp.exp2(-(head + 1).astype(F32))
    lane8 = lax.broadcasted_iota(jnp.int32, (N_HEADS, LANES), 1)
    ali = _alibi_lanes(slope, t_hi, t_lo, (N_HEADS, LANES))

    kc = kvc[:, :LANES]
    if n_cmp < LANES:
        kc = jnp.concatenate([kc, jnp.zeros((LANES - n_cmp, LANES), F32)], axis=0)
    blk = lax.broadcasted_iota(jnp.int32, (1, LANES), 1)
    dist_c = past_len - ((blk + 1) * CMP_BLOCK - 1)
    mask_c = (dist_c >= 0) & (blk < n_cmp)
    s_c = _nt_dot(q8, kc.astype(BF16)) - slope * dist_c.astype(F32)
    compress_pages(2)
    p_c = _masked_softmax(s_c, mask_c, 1)
    o_c = _dot(p_c[:, :n_cmp].astype(BF16), kvc[:, LANES:].astype(BF16))

    forced = (blk == cur) | (blk == 0)
    started = blk <= cur
    below = (lax.broadcasted_iota(jnp.int32, (LANES, LANES), 0)
             < lax.broadcasted_iota(jnp.int32, (LANES, LANES), 1))
    sel_rows = []
    for kv in range(N_KV):
        imp = jnp.sum(p_c[kv * GROUP:(kv + 1) * GROUP], axis=0, keepdims=True)
        score = jnp.where(forced, SEL_FORCED, jnp.where(started, imp, SEL_INVALID))
        col = _row_to_col(score)
        compress_pages(1)
        beats = (col > score) | ((col == score) & below)
        rank = jnp.sum(beats.astype(F32), axis=0, keepdims=True)
        sel = ((rank < N_SELECT) & (score > SEL_INVALID * 0.5)).astype(F32)
        sel_rows += [sel] * GROUP
    sel_bias = (jnp.concatenate(sel_rows, axis=0) - 1.0) * -NEG_INF
    qaug_s = jnp.concatenate([q8, jnp.where(lane8 < CMP_BLOCK, sel_bias, ali).astype(BF16)], axis=1)
    qaug_w = jnp.concatenate([q8, ali.astype(BF16)], axis=1)

    def new_key_score(row_ref):
        k_new = row_ref[0][:, :LANES].astype(BF16).astype(F32)
        return jnp.sum(q8.astype(F32) * k_new, axis=1, keepdims=True)

    def finish(scores, values, s_new, v_new):
        m = s_new
        for s in scores:
            m = jnp.maximum(m, jnp.max(s, axis=1, keepdims=True))
        p_new = jnp.exp(s_new - m)
        l = p_new
        acc = p_new * v_new.astype(BF16).astype(F32)
        for i, (s, v) in enumerate(zip(scores, values)):
            p = jnp.exp(s - m)
            l = l + jnp.sum(p, axis=1, keepdims=True)
            acc = acc + _nt_dot(p.astype(BF16), v)
            if i % 4 == 3:
                compress_pages(1)
        return acc / l

    scores, values = [], []
    for p, ref in enumerate(page_refs):
        if p % 2 == 0:
            compress_pages(1)
        page = ref[0, 0]
        kaug = jnp.concatenate([page[:LANES].astype(BF16), exts_ref[p]], axis=0)
        scores.append(_dot(qaug_s, kaug))
        values.append(page[LANES:].astype(BF16))
    o_s = finish(scores, values, new_key_score(ksn_ref), ksn_ref[0][:, LANES:])

    win = win_ref[0, 0]
    kaug = jnp.concatenate([win[:LANES].astype(BF16), extw_ref[...]], axis=0)
    s_w = _dot(qaug_w, kaug)
    dist_w = win_keep - lax.broadcasted_iota(jnp.int32, (1, win_keep), 1)
    s_w = jnp.where((dist_w >= 0) & (dist_w < WINDOW), s_w, NEG_INF)
    o_w = finish([s_w], [win[LANES:].astype(BF16)], new_key_score(kwn_ref), kwn_ref[0][:, LANES:])

    sig = _sigmoid(gl_ref[0])
    gate = [jnp.sum(jnp.where(lane8 == 3 * head + j, sig, 0.0), axis=1, keepdims=True) for j in range(3)]
    out = gate[0] * o_c + gate[1] * o_s + gate[2] * o_w
    out = jnp.where(head < GROUP, out, pltpu.roll(out, HEAD_DIM, axis=1))
    o_ref[0] = out[:, :HEAD_DIM]
    compress_pages(len(todo))
    kvc_buf[...] = _dot(mean_buf[...].astype(BF16), w2_ref[...])


def _attn_sample(cmp_t, sel_t, win_t, pt_flat, layer, pr, wts, ext_s, ext_w, db, n_pages, win_keep):
    past_len = n_pages * PAGE_SIZE
    n_cmp = past_len // CMP_BLOCK
    attn_row = lambda s: jnp.maximum(s - 1, 0)
    cmp_row = lambda s: jnp.minimum(s, db - 1)
    page = lambda row_of, p: pl.BlockSpec(
        (1, 1, D_KV, PAGE_SIZE), lambda s, pt: (layer, pt[row_of(s) * n_pages + p], 0, 0))
    row = lambda width: pl.BlockSpec((1, 1, width), lambda s, pt: (attn_row(s), 0, 0))
    const = lambda shape: pl.BlockSpec(shape, lambda s, pt: (0,) * len(shape))
    grid_spec = pltpu.PrefetchScalarGridSpec(
        num_scalar_prefetch=1,
        grid=(db + 1,),
        in_specs=[page(cmp_row, p) for p in range(n_pages)] + [page(attn_row, p) for p in range(n_pages)]
        + [row(Q_WIDE), row(LANES), row(D_KV), row(D_KV),
           pl.BlockSpec((1, 1, D_KV, win_keep), lambda s, pt: (layer, attn_row(s), 0, 0)),
           const((n_pages, LANES, PAGE_SIZE)), const((LANES, win_keep)),
           const((D_KV, LANES)), const((D_KV, D_KV)), const((D_KV, D_KV))],
        out_specs=pl.BlockSpec((1, N_HEADS, HEAD_DIM), lambda s, pt: (attn_row(s), 0, 0)),
        scratch_shapes=[pltpu.VMEM((n_cmp, D_KV), F32), pltpu.VMEM((n_cmp, D_KV), F32)],
    )
    kern = functools.partial(_attn_sample_kernel, n_pages=n_pages, past_len=past_len, win_keep=win_keep)
    out = pl.pallas_call(
        kern,
        grid_spec=grid_spec,
        out_shape=jax.ShapeDtypeStruct((db, N_HEADS, HEAD_DIM), F32),
        compiler_params=_params("arbitrary"),
        name="attn_sample",
    )(pt_flat, *([cmp_t] * n_pages), *([sel_t] * n_pages), pr["q"].reshape(db, 1, Q_WIDE),
      pr["gl"].reshape(db, 1, LANES), pr["ks"].reshape(db, 1, D_KV), pr["kw"].reshape(db, 1, D_KV),
      win_t, ext_s, ext_w, wts["pe_t"], wts["w1"], wts["w2"])
    return out.reshape(db, D_NSA)


def _win_update_kernel(win_ref, new_ref, o_ref, *, bb, win_keep):
    lane = lax.broadcasted_iota(jnp.int32, (D_KV, win_keep), 1)
    for k in range(bb):
        row = new_ref[0, k]
        col = jnp.concatenate([_row_to_col(row[:, h * LANES:(h + 1) * LANES]) for h in range(D_KV // LANES)],
                              axis=0)
        shifted = pltpu.roll(win_ref[0, k], win_keep - 1, axis=1)
        o_ref[0, k] = jnp.where(lane == win_keep - 1, col, shifted)


def _win_update(win_t, kw_new, bb):
    depth, db, _, win_keep = win_t.shape
    return pl.pallas_call(
        functools.partial(_win_update_kernel, bb=bb, win_keep=win_keep),
        grid=(depth, db // bb),
        in_specs=[pl.BlockSpec((1, bb, D_KV, win_keep), lambda l, b: (l, b, 0, 0)),
                  pl.BlockSpec((1, bb, 1, D_KV), lambda l, b: (l, b, 0, 0))],
        out_specs=pl.BlockSpec((1, bb, D_KV, win_keep), lambda l, b: (l, b, 0, 0)),
        out_shape=jax.ShapeDtypeStruct(win_t.shape, F32),
        compiler_params=_params("parallel", "parallel"),
        name="win_update",
    )(win_t, kw_new)


def _pool_window_lane(shape):
    lane = lax.broadcasted_iota(jnp.int32, shape, len(shape) - 1)
    w = jnp.full(shape, POOL_WINDOWS[0], jnp.int32)
    for gi in range(1, len(POOL_WINDOWS)):
        w = jnp.where(lane >= gi * POOL_GROUP_DIM, POOL_WINDOWS[gi], w)
    return w


def _group_rms(v, ones_bd):
    sq = v * v
    hi = sq.astype(BF16)
    lo = (sq - hi.astype(F32)).astype(BF16)
    return (_dot(hi, ones_bd) + _dot(lo, ones_bd)) * (1.0 / GMLP_GROUP_DIM)


def _mix_out(x, ya, za, yb, zb, yc, zc, wo_ref, gpost):
    out = _dot((ya * _silu(za)).astype(BF16), wo_ref[0:D_NSA, :])
    out += _dot((yb * _silu(zb)).astype(BF16), wo_ref[D_NSA:D_NSA + D_POOL, :])
    out += _dot((yc * _silu(zc)).astype(BF16), wo_ref[D_NSA + D_POOL:, :])
    ms = jnp.mean(out * out, axis=-1, keepdims=True)
    return x + out * lax.rsqrt(ms + EPS) * gpost


def _merge_prompt_kernel(x_ref, ya_ref, za_ref, pin_ref, prev_ref, zb_ref, u_ref, v_ref, zc_ref,
                         pw_ref, ps_ref, gn_ref, ones_ref, ws_ref, bs_ref, wo_ref, gp_ref,
                         xo_ref, vn_ref, xext, *, tm, tiles_per_seq):
    i = pl.program_id(0)
    first = (i % tiles_per_seq) == 0
    pin = pin_ref[...]
    xext[POOL_MAX:, :] = pin
    xext[0:POOL_MAX, :] = jnp.where(first, 0.0, prev_ref[...])
    w_lane = _pool_window_lane((tm, D_POOL))
    acc = pin
    for k in range(1, POOL_MAX):
        acc = acc + jnp.where(w_lane > k, xext[pl.ds(POOL_MAX - k, tm), :], 0.0)
    pos = (i % tiles_per_seq) * tm + lax.broadcasted_iota(jnp.int32, (tm, D_POOL), 0)
    cnt = jnp.minimum(w_lane, pos + 1).astype(F32)
    diff = acc / cnt - pin
    yb = _dot(diff.astype(BF16), pw_ref[...]) * ps_ref[...]

    v = v_ref[...]
    vn = v * lax.rsqrt(_group_rms(v, ones_ref[...]) + EPS) * gn_ref[...]
    vn_ref[...] = vn
    vnb = vn.astype(BF16)
    lane = lax.broadcasted_iota(jnp.int32, (CHUNK, D_GMLP), 1)
    chunks = []
    for c in range(tm // CHUNK):
        vc = vnb[c * CHUNK:(c + 1) * CHUNK]
        s = _dot(ws_ref[0], vc)
        for g in range(1, D_GMLP // GMLP_GROUP_DIM):
            s = jnp.where(lane >= g * GMLP_GROUP_DIM, _dot(ws_ref[g], vc), s)
        chunks.append(s + bs_ref[...])
    yc = u_ref[...] * jnp.concatenate(chunks, axis=0)

    xo_ref[...] = _mix_out(x_ref[...], ya_ref[...], za_ref[...], yb, zb_ref[...], yc, zc_ref[...],
                           wo_ref, gp_ref[...])


def _merge_prompt(x2, ya, pr, wts, seq, tm):
    m = x2.shape[0]
    tiles_per_seq = seq // tm
    per = tm // POOL_MAX
    rows = lambda width: pl.BlockSpec((tm, width), lambda i: (i, 0))
    const = lambda shape: pl.BlockSpec(shape, lambda i: (0,) * len(shape))
    kern = functools.partial(_merge_prompt_kernel, tm=tm, tiles_per_seq=tiles_per_seq)
    return pl.pallas_call(
        kern,
        grid=(m // tm,),
        in_specs=[rows(D_MODEL), rows(D_NSA), rows(D_NSA), rows(D_POOL),
                  pl.BlockSpec((POOL_MAX, D_POOL), lambda i: (jnp.maximum(i * per - 1, 0), 0)),
                  rows(D_POOL), rows(D_GMLP), rows(D_GMLP), rows(D_GMLP),
                  const((D_POOL, D_POOL)), const((1, D_POOL)), const((1, D_GMLP)),
                  const((D_GMLP, D_GMLP)), const((4, CHUNK, CHUNK)), const((CHUNK, D_GMLP)),
                  const((D_MODEL, D_MODEL)), const((1, D_MODEL))],
        out_specs=[rows(D_MODEL), rows(D_GMLP)],
        out_shape=[jax.ShapeDtypeStruct((m, D_MODEL), F32), jax.ShapeDtypeStruct((m, D_GMLP), F32)],
        scratch_shapes=[pltpu.VMEM((tm + POOL_MAX, D_POOL), F32)],
        compiler_params=_params("parallel"),
        name="merge_prompt",
    )(x2, ya, pr["za"], pr["pin"], pr["pin"], pr["zb"], pr["u"], pr["v"], pr["zc"],
      wts["pool_w"], wts["pool_scale"], wts["gmlp_norm"], wts["ones_bd"], wts["ws_tril"], wts["bs_full"],
      wts["w_out"], wts["norm_post"])


def _merge_sample_kernel(x_ref, ya_ref, za_ref, pin_ref, st_ref, zb_ref, u_ref, v_ref, zc_ref,
                         pw_ref, ps_ref, gn_ref, ones_ref, w0_ref, b0_ref, wo_ref, gp_ref,
                         xo_ref, vn_ref):
    pin = pin_ref[...]
    w_lane = _pool_window_lane(pin.shape)
    acc = pin
    for k in range(1, POOL_MAX):
        acc = acc + jnp.where(w_lane > k, st_ref[POOL_MAX - 1 - k], 0.0)
    diff = acc / w_lane.astype(F32) - pin
    yb = _dot(diff.astype(BF16), pw_ref[...]) * ps_ref[...]

    v = v_ref[...]
    vn = v * lax.rsqrt(_group_rms(v, ones_ref[...]) + EPS) * gn_ref[...]
    vn_ref[...] = vn
    yc = u_ref[...] * (w0_ref[...] * vn + b0_ref[...])

    xo_ref[...] = _mix_out(x_ref[...], ya_ref[...], za_ref[...], yb, zb_ref[...], yc, zc_ref[...],
                           wo_ref, gp_ref[...])


def _merge_sample(x2, ya, pr, state_t, wts):
    db = x2.shape[0]
    full = lambda shape: pl.BlockSpec(shape, lambda i: (0,) * len(shape))
    args = (x2, ya, pr["za"], pr["pin"], state_t, pr["zb"], pr["u"], pr["v"], pr["zc"],
            wts["pool_w"], wts["pool_scale"], wts["gmlp_norm"], wts["ones_bd"], wts["w0"], wts["b0"],
            wts["w_out"], wts["norm_post"])
    return pl.pallas_call(
        _merge_sample_kernel,
        grid=(1,),
        in_specs=[full(a.shape) for a in args],
        out_specs=[full((db, D_MODEL)), full((db, D_GMLP))],
        out_shape=[jax.ShapeDtypeStruct((db, D_MODEL), F32), jax.ShapeDtypeStruct((db, D_GMLP), F32)],
        compiler_params=_params("arbitrary"),
        name="merge_sample",
    )(*args)


def _layer_weights(l, norm_pre, w_in, cmp_pe, cmp_w1, cmp_w2, pool_w, pool_scale, gmlp_norm, gmlp_ws, gmlp_bs,
                   w_out, norm_post):
    n_g = D_GMLP // GMLP_GROUP_DIM
    ones_bd = jnp.kron(jnp.eye(n_g, dtype=F32), jnp.ones((GMLP_GROUP_DIM, GMLP_GROUP_DIM), F32)).astype(BF16)
    pw = pool_w[l]
    zp = jnp.zeros_like(pw[0])
    pool_bd = jnp.concatenate(
        [jnp.concatenate([pw[i] if j == i else zp for j in range(4)], axis=1) for i in range(4)], axis=0)
    tril = jnp.tril(jnp.ones((CHUNK, CHUNK), F32))
    return dict(
        norm_pre=norm_pre[l].reshape(1, D_MODEL),
        w_in=_prep_w_in(w_in[l]),
        pe_t=_pe_t(cmp_pe[l]),
        w1=_blockdiag4(cmp_w1[l]).astype(BF16),
        w2=_blockdiag4(cmp_w2[l]).astype(BF16),
        pool_w=pool_bd.astype(BF16),
        pool_scale=pool_scale[l].reshape(1, D_POOL),
        gmlp_norm=gmlp_norm[l].reshape(1, D_GMLP),
        ones_bd=ones_bd,
        ws_tril=(gmlp_ws[l] * tril).astype(BF16),
        bs_full=jnp.repeat(gmlp_bs[l].T, GMLP_GROUP_DIM, axis=1),
        w0=jnp.repeat(gmlp_ws[l][:, 0, 0], GMLP_GROUP_DIM).reshape(1, D_GMLP),
        b0=jnp.repeat(gmlp_bs[l][:, 0], GMLP_GROUP_DIM).reshape(1, D_GMLP),
        w_out=w_out[l].astype(BF16),
        norm_post=norm_post[l].reshape(1, D_MODEL),
    )


def _largest_tile(n, cap):
    t = cap
    while n % t:
        t //= 2
    return t


def _feature_major(cache):
    d, n, t = cache.shape[:3]
    return jnp.transpose(cache, (0, 1, 3, 4, 5, 2)).reshape(d, n, D_KV, t)


def _token_major(x_t):
    d, n, _, t = x_t.shape
    return jnp.transpose(x_t.reshape(d, n, 2, N_KV, HEAD_DIM, t), (0, 1, 5, 2, 3, 4))


def kernel(x_prompt, x_sample, cache_kv_cmp, cache_kv_sel, cache_kv_win, state_pool, page_table, norm_pre, w_in,
           cmp_pe, cmp_w1, cmp_w2, pool_w, pool_scale, gmlp_norm, gmlp_ws, gmlp_bs, w_out, norm_post):
    bp, tp, _ = x_prompt.shape
    db, ts, _ = x_sample.shape
    depth = w_in.shape[0]
    n_pages = page_table.shape[1]
    past_len = n_pages * PAGE_SIZE
    win_keep = cache_kv_win.shape[2]
    assert ts == 1 and tp % CHUNK == 0 and tp <= CMP_BLOCK * CMP_BLOCK and tp >= WINDOW
    assert past_len // CMP_BLOCK + 1 <= CMP_BLOCK and win_keep == WINDOW and db % LANES == 0

    mp = bp * tp
    tm_proj = _largest_tile(tp, 512)
    tile_cmp = _largest_tile(tp, 1024)
    tm_merge = _largest_tile(tp, 512)
    tq, tk = 128, _largest_tile(tp, 512)
    assert tp >= WINDOW + tq

    cmp_t = _feature_major(cache_kv_cmp)
    sel_t = _feature_major(cache_kv_sel)
    win_t = _feature_major(cache_kv_win)
    pt_flat = page_table.reshape(-1).astype(jnp.int32)

    ext_prompt = _extra_rows(jnp.arange(tp, dtype=jnp.int32), True).T
    ext_s = jnp.transpose(
        _extra_rows(jnp.arange(past_len, dtype=jnp.int32), True).reshape(LANES, n_pages, PAGE_SIZE), (1, 0, 2))
    ext_w = _extra_rows(past_len - win_keep + jnp.arange(win_keep, dtype=jnp.int32), False)

    xp = x_prompt.reshape(mp, D_MODEL)
    xs = x_sample.reshape(db, D_MODEL)
    names = ("kvc_p", "kvc_s", "kvs_p", "kvs_s", "kvw_p", "kw_new", "pool_p", "pool_s", "gv_p", "gv_s")
    outs = {k: [] for k in names}
    for l in range(depth):
        wts = _layer_weights(l, norm_pre, w_in, cmp_pe, cmp_w1, cmp_w2, pool_w, pool_scale, gmlp_norm, gmlp_ws,
                             gmlp_bs, w_out, norm_post)
        pr = _project(xp, wts["norm_pre"], wts["w_in"], bp, tp, tm_proj, False)
        kvc = _compress_prompt(pr["kc_t"], wts, tile_cmp)
        ya = _attn_prompt(pr["q"], pr["gl"], kvc, pr["ks_t"], pr["kw_t"], ext_prompt, bp, tp, tq, tk)
        xp, vn = _merge_prompt(xp, ya, pr, wts, tp, tm_merge)
        outs["kvc_p"].append(pr["kc_t"])
        outs["kvs_p"].append(pr["ks_t"])
        outs["kvw_p"].append(pr["kw_t"][:, :, tp - WINDOW:])
        outs["pool_p"].append(pr["pin"].reshape(bp, tp, D_POOL)[:, tp - (POOL_MAX - 1):])
        outs["gv_p"].append(vn.reshape(bp, tp, D_GMLP)[:, tp - CHUNK:])
        pr = _project(xs, wts["norm_pre"], wts["w_in"], 1, db, db, True)
        ya = _attn_sample(cmp_t, sel_t, win_t, pt_flat, l, pr, wts, ext_s, ext_w, db, n_pages, win_keep)
        state_t = jnp.swapaxes(state_pool[l], 0, 1)
        xs, vn = _merge_sample(xs, ya, pr, state_t, wts)
        outs["kvc_s"].append(pr["kc_t"])
        outs["kvs_s"].append(pr["ks_t"])
        outs["kw_new"].append(pr["kw"].reshape(db, 1, D_KV))
        outs["pool_s"].append(jnp.concatenate([state_t[1:], pr["pin"][None]], axis=0))
        outs["gv_s"].append(vn.reshape(db, 1, D_GMLP))
    st = {k: jnp.stack(v) for k, v in outs.items()}
    kvw_s = _win_update(win_t, st["kw_new"], 4)
    sample_kv = lambda x_t: jnp.transpose(_token_major(x_t), (0, 2, 1, 3, 4, 5))
    return (xp.reshape(bp, tp, D_MODEL), xs.reshape(db, ts, D_MODEL),
            _token_major(st["kvc_p"]), sample_kv(st["kvc_s"]),
            _token_major(st["kvs_p"]), sample_kv(st["kvs_s"]),
            _token_major(st["kvw_p"]), _token_major(kvw_s),
            st["pool_p"], jnp.swapaxes(st["pool_s"], 1, 2), st["gv_p"], st["gv_s"])
```

```python
import functools

import jax
import jax.numpy as jnp
import ml_dtypes
import numpy as np
from jax import lax
from jax.experimental import pallas as pl
from jax.experimental.pallas import tpu as pltpu

F32 = jnp.float32
BF16 = jnp.bfloat16

D_MODEL = 1024
HEAD_DIM = 64
N_HEADS = 8
N_KV = 2
GROUP = N_HEADS // N_KV
D_NSA = N_HEADS * HEAD_DIM
D_KV = 2 * N_KV * HEAD_DIM
CMP_BLOCK = 64
N_SELECT = 16
WINDOW = 512
D_POOL = 256
POOL_WINDOWS = (2, 4, 8, 16)
POOL_GROUP_DIM = 64
POOL_MAX = 16
D_GMLP = 256
GMLP_GROUP_DIM = 64
CHUNK = 128
PAGE_SIZE = 128
EPS = 1e-6
NEG_INF = -1e30
SEL_FORCED = 1e4
SEL_INVALID = -1e4
Q_SCALE = HEAD_DIM ** -0.5

WIN_BLOCK = 128
SUB_KEYS = 256
LANES = 128
Q_WIDE = N_HEADS * LANES
VMEM_LIMIT = 56 * 1024 * 1024

ROW_ONE_A = 64
ROW_ONE_B = 65
ROW_POS_HI = 66
ROW_POS_LO = 67

LOG2E = float(np.log2(np.e))
LOG2E_HI = float(np.asarray(LOG2E, ml_dtypes.bfloat16))
LOG2E_LO = float(np.asarray(LOG2E - LOG2E_HI, ml_dtypes.bfloat16))
P_POS_HI = (64, 65)
P_POS_LO = (66, 67)
P_ONE = (68, 69, 70)
ONES_ROW = HEAD_DIM
V_ROWS = HEAD_DIM + 16

_SEG = dict(q=(0, 512), kc=(512, 256), ks=(768, 256), kw=(1024, 256), gl=(1280, 24), za=(1304, 512),
            pin=(1816, 256), zb=(2072, 256), u=(2328, 256), v=(2584, 256), zc=(2840, 256))
_PROJ_OUT = (("q", Q_WIDE), ("kc", 256), ("ks", 256), ("kw", 256), ("za", 512), ("pin", 256),
             ("zb", 256), ("u", 256), ("v", 256), ("zc", 256), ("gl", LANES))
_KV_NAMES = ("kc", "ks", "kw")


def _nt_dot(a, b):
    return lax.dot_general(a, b, (((1,), (1,)), ((), ())), preferred_element_type=F32)


def _tn_dot(a, b):
    return lax.dot_general(a, b, (((0,), (0,)), ((), ())), preferred_element_type=F32)


def _dot(a, b):
    return jnp.dot(a, b, preferred_element_type=F32)


def _sigmoid(x):
    return 0.5 * jnp.tanh(0.5 * x) + 0.5


def _silu(x):
    return x * _sigmoid(x)


def _params(*sem):
    return pltpu.CompilerParams(dimension_semantics=sem, vmem_limit_bytes=VMEM_LIMIT)


def _row_to_col(row):
    eye = (lax.broadcasted_iota(jnp.int32, (LANES, LANES), 0)
           == lax.broadcasted_iota(jnp.int32, (LANES, LANES), 1))
    return jnp.sum(jnp.where(eye, row, 0.0), axis=1, keepdims=True)


def _proj_kernel(x_ref, g_ref, w_ref, *out_refs, emit_rows):
    x = x_ref[...]
    ms = jnp.mean(x * x, axis=-1, keepdims=True)
    xn = (x * lax.rsqrt(ms + EPS) * g_ref[...]).astype(BF16)
    refs = iter(out_refs)
    off = 0
    for name, width in _PROJ_OUT:
        res = _dot(xn, w_ref[:, off:off + width])
        off += width
        if name in _KV_NAMES:
            next(refs)[0] = res.T
            if not emit_rows:
                continue
        next(refs)[...] = res


def _project(x2, g, w, batch, seq, tm, emit_rows):
    m = x2.shape[0]
    n_tot = w.shape[1]
    tiles = seq // tm
    names, shapes, specs = [], [], []
    for name, width in _PROJ_OUT:
        if name in _KV_NAMES:
            names.append(name + "_t")
            shapes.append(jax.ShapeDtypeStruct((batch, width, seq), F32))
            specs.append(pl.BlockSpec((1, width, tm), lambda i: (i // tiles, 0, i % tiles)))
            if not emit_rows:
                continue
        names.append(name)
        shapes.append(jax.ShapeDtypeStruct((m, width), F32))
        specs.append(pl.BlockSpec((tm, width), lambda i: (i, 0)))
    res = pl.pallas_call(
        functools.partial(_proj_kernel, emit_rows=emit_rows),
        grid=(m // tm,),
        in_specs=[pl.BlockSpec((tm, D_MODEL), lambda i: (i, 0)),
                  pl.BlockSpec((1, D_MODEL), lambda i: (0, 0)),
                  pl.BlockSpec((D_MODEL, n_tot), lambda i: (0, 0))],
        out_specs=specs,
        out_shape=shapes,
        compiler_params=_params("parallel"),
        name="norm_project",
    )(x2, g, w)
    return dict(zip(names, res))


def _prep_w_in(w):
    def seg(name):
        o, n = _SEG[name]
        return w[:, o:o + n]
    wq = seg("q").reshape(D_MODEL, N_HEADS, HEAD_DIM)
    zeros = jnp.zeros_like(wq)
    lo = jnp.concatenate([wq, zeros], axis=-1)
    hi = jnp.concatenate([zeros, wq], axis=-1)
    first = (jnp.arange(N_HEADS) < GROUP)[None, :, None]
    q_wide = jnp.where(first, lo, hi).reshape(D_MODEL, Q_WIDE)
    gl = jnp.pad(seg("gl"), ((0, 0), (0, LANES - 3 * N_HEADS)))
    cols = [q_wide] + [seg(n) for n, _ in _PROJ_OUT[1:-1]] + [gl]
    return jnp.concatenate(cols, axis=1).astype(BF16)


def _blockdiag4(w):
    z = jnp.zeros((HEAD_DIM, HEAD_DIM), w.dtype)
    blocks = [w[0], w[0], w[1], w[1]]
    rows = [jnp.concatenate([blocks[i] if j == i else z for j in range(4)], axis=1) for i in range(4)]
    return jnp.concatenate(rows, axis=0)


def _pe_t(pe):
    one = jnp.concatenate([pe[0].T, pe[0].T, pe[1].T, pe[1].T], axis=0)
    return jnp.concatenate([one, one], axis=1)


def _compress_mean(xt, pe128, w1):
    n = xt.shape[1]
    pet = jnp.concatenate([pe128] * (n // LANES), axis=1)
    xb = (xt + pet).T.astype(BF16)
    hid = _silu(_dot(xb, w1))
    return jnp.sum(hid.reshape(n // CMP_BLOCK, CMP_BLOCK, D_KV), axis=1) * (1.0 / CMP_BLOCK)


def _compress_t(xt, pe128, w1, w2):
    return _dot(_compress_mean(xt, pe128, w1).astype(BF16), w2)


def _compress_kernel(x_ref, pe_ref, w1_ref, w2_ref, o_ref):
    o_ref[...] = _compress_t(x_ref[0], pe_ref[...], w1_ref[...], w2_ref[...])


def _compress_prompt(kc_t, wts, tile):
    batch, _, seq = kc_t.shape
    tiles = seq // tile
    const = lambda shape: pl.BlockSpec(shape, lambda b, j: (0,) * len(shape))
    return pl.pallas_call(
        _compress_kernel,
        grid=(batch, tiles),
        in_specs=[pl.BlockSpec((1, D_KV, tile), lambda b, j: (b, 0, j)),
                  const((D_KV, LANES)), const((D_KV, D_KV)), const((D_KV, D_KV))],
        out_specs=pl.BlockSpec((tile // CMP_BLOCK, D_KV), lambda b, j: (b * tiles + j, 0)),
        out_shape=jax.ShapeDtypeStruct((batch * seq // CMP_BLOCK, D_KV), F32),
        compiler_params=_params("parallel", "parallel"),
        name="compress_prompt",
    )(kc_t, wts["pe_t"], wts["w1"], wts["w2"])


def _extra_rows(pos, with_onehot):
    row = jnp.arange(LANES)[:, None]
    blk = (pos // CMP_BLOCK)[None, :]
    hi = ((pos // CMP_BLOCK) * CMP_BLOCK).astype(F32)[None, :]
    lo = (pos % CMP_BLOCK).astype(F32)[None, :]
    out = jnp.where((row == ROW_ONE_A) | (row == ROW_ONE_B), 1.0, 0.0) + jnp.zeros_like(hi)
    out = jnp.where(row == ROW_POS_HI, hi, out)
    out = jnp.where(row == ROW_POS_LO, lo, out)
    if with_onehot:
        out = jnp.where((row < CMP_BLOCK) & (row == blk), 1.0, out)
    return out.astype(BF16)


def _prompt_key_lanes(pos):
    lane = jnp.arange(LANES)[None, :]
    hi = ((pos // CMP_BLOCK) * CMP_BLOCK).astype(F32)[:, None]
    lo = (pos % CMP_BLOCK).astype(F32)[:, None]
    out = jnp.where((lane < CMP_BLOCK) & (lane == (pos // CMP_BLOCK)[:, None]), 1.0, 0.0)
    out = jnp.where((lane == P_POS_HI[0]) | (lane == P_POS_HI[1]), hi, out)
    out = jnp.where((lane == P_POS_LO[0]) | (lane == P_POS_LO[1]), lo, out)
    out = jnp.where((lane >= P_ONE[0]) & (lane <= P_ONE[-1]), 1.0, out)
    return out.astype(BF16)


def _alibi_lanes(slope, t_hi, t_lo, shape):
    lane = lax.broadcasted_iota(jnp.int32, shape, 1)
    out = jnp.where(lane == ROW_ONE_A, -slope * t_hi, 0.0)
    out = jnp.where(lane == ROW_ONE_B, -slope * t_lo, out)
    return jnp.where((lane == ROW_POS_HI) | (lane == ROW_POS_LO), slope, out)


def _masked_softmax(s, mask, axis):
    s = jnp.where(mask, s, NEG_INF)
    p = jnp.exp(s - jnp.max(s, axis=axis, keepdims=True)) * mask.astype(F32)
    return p / jnp.maximum(jnp.sum(p, axis=axis, keepdims=True), 1e-30)


def _attn_prompt_kernel(q_ref, gl_ref, kvc_ref, ksk_ref, ksv_ref, kwk_ref, kwv_ref, ext_ref, o_ref,
                        kaug_s, kaug_w, v_s, v_w, s_buf, *, seq, tq, tk):
    i = pl.program_id(1)
    m = GROUP * tq
    n_cmp = seq // CMP_BLOCK

    n_wblk = WINDOW // WIN_BLOCK + tq // WIN_BLOCK

    def value_tile(v_t):
        n = v_t.shape[1]
        ones_row = (lax.broadcasted_iota(jnp.int32, (V_ROWS - HEAD_DIM, n), 0) == 0).astype(F32)
        return jnp.concatenate([v_t, ones_row], axis=0).astype(BF16)

    @pl.when(i == 0)
    def _fill():
        for c in range(seq // tk):
            cols = slice(c * tk, (c + 1) * tk)
            kaug_s[c] = jnp.concatenate([ksk_ref[0, :, cols].T.astype(BF16), ext_ref[cols, :]], axis=1)
            for kv in range(N_KV):
                v_s[c, kv] = value_tile(ksv_ref[0, kv * HEAD_DIM:(kv + 1) * HEAD_DIM, cols])
        for c in range(seq // WIN_BLOCK):
            cols = slice(c * WIN_BLOCK, (c + 1) * WIN_BLOCK)
            kaug_w[c] = jnp.concatenate([kwk_ref[0, :, cols].T.astype(BF16), ext_ref[cols, :]], axis=1)
            for kv in range(N_KV):
                v_w[c, kv] = value_tile(kwv_ref[0, kv * HEAD_DIM:(kv + 1) * HEAD_DIM, cols])

    q0 = i * tq
    lane_t = q0 + lax.broadcasted_iota(jnp.int32, (1, m), 1) % tq
    ext_row = lax.broadcasted_iota(jnp.int32, (LANES, m), 0)
    key_row = lax.broadcasted_iota(jnp.int32, (tk, 1), 0)
    sig_t = _sigmoid(gl_ref[...]).T

    blk_row = lax.broadcasted_iota(jnp.int32, (n_cmp, m), 0)
    dist_c = lane_t - ((blk_row + 1) * CMP_BLOCK - 1)
    mask_c = dist_c >= 0
    blk2 = lax.broadcasted_iota(jnp.int32, (n_cmp, tq), 0)
    cur = (q0 + lax.broadcasted_iota(jnp.int32, (1, tq), 1)) // CMP_BLOCK
    forced = (blk2 == cur) | (blk2 == 0)
    started = blk2 <= cur

    def pieces_of(n):
        return [(a, min(a + SUB_KEYS, n)) for a in range(0, n, SUB_KEYS)]

    def qk_scores(qaugs, k_tile):
        return [[_dot(k_tile[a:b], qaug_t) for qaug_t in qaugs] for a, b in pieces_of(k_tile.shape[0])]

    def softmax_update(s, v_piece, carry, mask):
        m_i, acc = carry
        if mask is not None:
            s = jnp.where(mask, s, NEG_INF)
        m_new = jnp.maximum(m_i, jnp.max(s, axis=0, keepdims=True))
        p = jnp.exp2(s - m_new).astype(BF16)
        return m_new, jnp.exp2(m_i - m_new) * acc + _dot(v_piece, p)

    def softmax_pv(scores, v_tiles, carries, mask_fn):
        carries = list(carries)
        for (a, b), piece_scores in zip(pieces_of(v_tiles[0].shape[1]), scores):
            for kv, s in enumerate(piece_scores):
                carries[kv] = softmax_update(s, v_tiles[kv][:, a:b], carries[kv], mask_fn(a, b))
        return tuple(carries)

    def normalized(carries):
        return [acc[:HEAD_DIM] / acc[ONES_ROW:ONES_ROW + 1] for _, acc in carries]

    def sel_fill(buf, j):
        for p, piece_scores in enumerate(qk_scores(qaugs_s, kaug_s[j])):
            for kv, s in enumerate(piece_scores):
                s_buf[buf, p * N_KV + kv] = s

    def sel_tile(buf, j, carries, causal):
        carries = list(carries)
        for p, (a, b) in enumerate(pieces_of(tk)):
            mask = (j * tk + key_row[a:b] <= lane_t) if causal else None
            for kv in range(N_KV):
                carries[kv] = softmax_update(s_buf[buf, p * N_KV + kv], v_s[j, kv, :, a:b], carries[kv], mask)
        return tuple(carries)

    def split3(x):
        hi = x.astype(BF16).astype(F32)
        mid = (x - hi).astype(BF16).astype(F32)
        return hi, mid, x - hi - mid

    init = (jnp.full((1, m), NEG_INF, F32), jnp.zeros((V_ROWS, m), F32))
    n_chunk = n_cmp // 8
    q4_ts, slope_rows, ali_ts, qaugs_w = [], [], [], []
    for kv in range(N_KV):
        qk = q_ref[:, kv * GROUP * LANES:(kv + 1) * GROUP * LANES]
        q_heads = [qk[:, g * LANES:(g + 1) * LANES].T for g in range(GROUP)]
        q4_t = jnp.concatenate([q * Q_SCALE for q in q_heads], axis=1).astype(BF16)
        q4l_t = jnp.concatenate([q * (Q_SCALE * LOG2E) for q in q_heads], axis=1).astype(BF16)
        slopes = [2.0 ** -(kv * GROUP + g + 1) for g in range(GROUP)]
        slope = jnp.concatenate([jnp.full((1, tq), s, F32) for s in slopes], axis=1)
        c_hi, c_lo = slope * LOG2E_HI, slope * LOG2E_LO
        d_hi, d_mid, d_lo = split3(-(c_hi + c_lo) * lane_t.astype(F32))
        ali_t = jnp.zeros((LANES, m), F32)
        for rows, val in ((P_POS_HI, (c_hi, c_lo)), (P_POS_LO, (c_hi, c_lo)), (P_ONE, (d_hi, d_mid, d_lo))):
            for r, v in zip(rows, val):
                ali_t = jnp.where(ext_row == r, v, ali_t)
        q4_ts.append(q4_t)
        slope_rows.append(slope)
        ali_ts.append((q4l_t, ali_t))
        qaugs_w.append(jnp.concatenate([q4l_t, ali_t.astype(BF16)], axis=0))

    c0 = jnp.maximum(q0 // WIN_BLOCK - WINDOW // WIN_BLOCK, 0)
    k_win = jnp.concatenate([kaug_w[c0 + t] for t in range(n_wblk)], axis=0)
    v_wins = [jnp.concatenate([v_w[c0 + t, kv] for t in range(n_wblk)], axis=1) for kv in range(N_KV)]
    scores_w = qk_scores(qaugs_w, k_win)

    o_cs, qaugs_s = [], []
    for kv in range(N_KV):
        q4_t, slope, (q4l_t, ali_t) = q4_ts[kv], slope_rows[kv], ali_ts[kv]
        kvc = kvc_ref[...]
        s_ct = _dot(kvc[:, :LANES].astype(BF16), q4_t) - slope * dist_c.astype(F32)
        p_ct = _masked_softmax(s_ct, mask_c, 0)
        o_c = _tn_dot(kvc[:, LANES:].astype(BF16), p_ct.astype(BF16))

        imp = p_ct[:, 0:tq]
        for g in range(1, GROUP):
            imp = imp + p_ct[:, g * tq:(g + 1) * tq]
        score = jnp.where(forced, SEL_FORCED, jnp.where(started, imp, SEL_INVALID))
        chunks = [score[8 * c:8 * c + 8] for c in range(n_chunk)]
        blk8 = lax.broadcasted_iota(jnp.int32, (8, tq), 0)
        ranks = [jnp.zeros((8, tq), F32) for _ in range(n_chunk)]
        for b in range(n_cmp):
            row = score[b:b + 1, :]
            for c in range(n_chunk):
                if 8 * c > b:
                    beats = row >= chunks[c]
                elif 8 * c + 7 < b:
                    beats = row > chunks[c]
                else:
                    beats = (row > chunks[c]) | ((row == chunks[c]) & (blk8 > b - 8 * c))
                ranks[c] = ranks[c] + jnp.where(beats, 1.0, 0.0)
        rank = jnp.concatenate(ranks, axis=0)
        sel_t = ((rank < N_SELECT) & (score > SEL_INVALID * 0.5)).astype(F32)
        sel_bias = (sel_t - 1.0) * -NEG_INF
        if n_cmp < LANES:
            sel_bias = jnp.concatenate([sel_bias, jnp.zeros((LANES - n_cmp, tq), F32)], axis=0)
        sel_bias = jnp.concatenate([sel_bias] * GROUP, axis=1)

        o_cs.append(o_c)
        qaugs_s.append(
            jnp.concatenate([q4l_t, jnp.where(ext_row < CMP_BLOCK, sel_bias, ali_t).astype(BF16)], axis=0))

    n_kt = (q0 + tq + tk - 1) // tk
    sel_fill(0, 0)

    key_w = c0 * WIN_BLOCK + lax.broadcasted_iota(jnp.int32, (n_wblk * WIN_BLOCK, 1), 0)

    def win_mask(a, b):
        causal = key_w[a:b] <= lane_t
        if a >= WIN_BLOCK:
            return causal
        return causal & (lane_t - key_w[a:b] < WINDOW)
    o_ws = normalized(softmax_pv(scores_w, v_wins, (init,) * N_KV, win_mask))

    n_full = n_kt - 1

    def sel_pair(jj, carries):
        j = 2 * jj
        sel_fill(1, j + 1)
        carries = sel_tile(0, j, carries, False)
        sel_fill(0, j + 2)
        return sel_tile(1, j + 1, carries, False)
    carries = lax.fori_loop(0, n_full // 2, sel_pair, (init,) * N_KV)
    j_a = 2 * (n_full // 2)
    odd = n_full - j_a
    sel_fill(1, n_kt - 1)
    carries = sel_tile(0, j_a, carries, True)
    carries = lax.cond(odd == 1, lambda c: sel_tile(1, n_kt - 1, c, True), lambda c: c, carries)
    o_ss = normalized(carries)

    heads = []
    for kv in range(N_KV):
        rows = slice(kv * HEAD_DIM, (kv + 1) * HEAD_DIM)
        for g in range(GROUP):
            h = kv * GROUP + g
            cols = slice(g * tq, (g + 1) * tq)
            heads.append(sig_t[3 * h:3 * h + 1] * o_cs[kv][rows, cols]
                         + sig_t[3 * h + 1:3 * h + 2] * o_ss[kv][:, cols]
                         + sig_t[3 * h + 2:3 * h + 3] * o_ws[kv][:, cols])
    o_ref[...] = jnp.concatenate(heads, axis=0).T


def _attn_prompt(qw, gl, kvc, ks_t, kw_t, ext, batch, seq, tq, tk):
    nq = seq // tq
    n_cmp = seq // CMP_BLOCK
    n_kt = seq // tk
    kern = functools.partial(_attn_prompt_kernel, seq=seq, tq=tq, tk=tk)
    kv_spec = lambda half: pl.BlockSpec((1, LANES, seq), lambda b, i: (b, half, 0))
    return pl.pallas_call(
        kern,
        grid=(batch, nq),
        in_specs=[pl.BlockSpec((tq, Q_WIDE), lambda b, i: (b * nq + i, 0)),
                  pl.BlockSpec((tq, LANES), lambda b, i: (b * nq + i, 0)),
                  pl.BlockSpec((n_cmp, D_KV), lambda b, i: (b, 0)),
                  kv_spec(0), kv_spec(1), kv_spec(0), kv_spec(1),
                  pl.BlockSpec((seq, LANES), lambda b, i: (0, 0))],
        out_specs=pl.BlockSpec((tq, D_NSA), lambda b, i: (b * nq + i, 0)),
        out_shape=jax.ShapeDtypeStruct((batch * seq, D_NSA), F32),
        scratch_shapes=[pltpu.VMEM((n_kt, tk, 2 * LANES), BF16),
                        pltpu.VMEM((seq // WIN_BLOCK, WIN_BLOCK, 2 * LANES), BF16),
                        pltpu.VMEM((n_kt, N_KV, V_ROWS, tk), BF16),
                        pltpu.VMEM((seq // WIN_BLOCK, N_KV, V_ROWS, WIN_BLOCK), BF16),
                        pltpu.VMEM((2, N_KV * tk // SUB_KEYS, SUB_KEYS, GROUP * tq), F32)],
        compiler_params=_params("arbitrary", "arbitrary"),
        name="attn_prompt",
    )(qw, gl, kvc, ks_t, ks_t, kw_t, kw_t, ext)


def _attn_sample_kernel(pt_ref, *refs, n_pages, past_len, win_keep):
    del pt_ref
    cmp_refs = refs[:n_pages]
    page_refs = refs[n_pages:2 * n_pages]
    (q_ref, gl_ref, ksn_ref, kwn_ref, win_ref, exts_ref, extw_ref, pe_ref, w1_ref, w2_ref,
     o_ref, kvc_buf, mean_buf) = refs[2 * n_pages:]
    n_cmp = past_len // CMP_BLOCK
    per_page = PAGE_SIZE // CMP_BLOCK

    @pl.when(pl.program_id(0) == 0)
    def _first():
        kvc_buf[...] = jnp.zeros(kvc_buf.shape, F32)

    kvc = kvc_buf[...]

    todo = list(range(n_pages))

    def compress_pages(count):
        for _ in range(min(count, len(todo))):
            p = todo.pop(0)
            mean_buf[p * per_page:(p + 1) * per_page, :] = _compress_mean(
                cmp_refs[p][0, 0], pe_ref[...], w1_ref[...])
    cur = past_len // CMP_BLOCK
    t_hi = float(cur * CMP_BLOCK)
    t_lo = float(past_len % CMP_BLOCK)

    qrow = q_ref[0]
    q8f = jnp.concatenate([qrow[:, h * LANES:(h + 1) * LANES] for h in range(N_HEADS)], axis=0) * Q_SCALE
    q8 = q8f.astype(BF16)
    head = lax.broadcasted_iota(jnp.int32, (N_HEADS, 1), 0)
    slope = jnp.exp2(-(head + 1).astype(F32))
    lane8 = lax.broadcasted_iota(jnp.int32, (N_HEADS, LANES), 1)
    ali = _alibi_lanes(slope, t_hi, t_lo, (N_HEADS, LANES))

    kc = kvc[:, :LANES]
    if n_cmp < LANES:
        kc = jnp.concatenate([kc, jnp.zeros((LANES - n_cmp, LANES), F32)], axis=0)
    blk = lax.broadcasted_iota(jnp.int32, (1, LANES), 1)
    dist_c = past_len - ((blk + 1) * CMP_BLOCK - 1)
    mask_c = (dist_c >= 0) & (blk < n_cmp)
    s_c = _nt_dot(q8, kc.astype(BF16)) - slope * dist_c.astype(F32)
    compress_pages(2)
    p_c = _masked_softmax(s_c, mask_c, 1)
    o_c = _dot(p_c[:, :n_cmp].astype(BF16), kvc[:, LANES:].astype(BF16))

    forced = (blk == cur) | (blk == 0)
    started = blk <= cur
    below = (lax.broadcasted_iota(jnp.int32, (LANES, LANES), 0)
             < lax.broadcasted_iota(jnp.int32, (LANES, LANES), 1))
    sel_rows = []
    for kv in range(N_KV):
        imp = jnp.sum(p_c[kv * GROUP:(kv + 1) * GROUP], axis=0, keepdims=True)
        score = jnp.where(forced, SEL_FORCED, jnp.where(started, imp, SEL_INVALID))
        col = _row_to_col(score)
        compress_pages(1)
        beats = (col > score) | ((col == score) & below)
        rank = jnp.sum(beats.astype(F32), axis=0, keepdims=True)
        sel = ((rank < N_SELECT) & (score > SEL_INVALID * 0.5)).astype(F32)
        sel_rows += [sel] * GROUP
    sel_bias = (jnp.concatenate(sel_rows, axis=0) - 1.0) * -NEG_INF
    qaug_s = jnp.concatenate([q8, jnp.where(lane8 < CMP_BLOCK, sel_bias, ali).astype(BF16)], axis=1)
    qaug_w = jnp.concatenate([q8, ali.astype(BF16)], axis=1)

    def new_key_score(row_ref):
        k_new = row_ref[0][:, :LANES].astype(BF16).astype(F32)
        return jnp.sum(q8.astype(F32) * k_new, axis=1, keepdims=True)

    def finish(scores, values, s_new, v_new):
        m = s_new
        for s in scores:
            m = jnp.maximum(m, jnp.max(s, axis=1, keepdims=True))
        p_new = jnp.exp(s_new - m)
        l = p_new
        acc = p_new * v_new.astype(BF16).astype(F32)
        for i, (s, v) in enumerate(zip(scores, values)):
            p = jnp.exp(s - m)
            l = l + jnp.sum(p, axis=1, keepdims=True)
            acc = acc + _nt_dot(p.astype(BF16), v)
            if i % 4 == 3:
                compress_pages(1)
        return acc / l

    scores, values = [], []
    for p, ref in enumerate(page_refs):
        if p % 2 == 0:
            compress_pages(1)
        page = ref[0, 0]
        kaug = jnp.concatenate([page[:LANES].astype(BF16), exts_ref[p]], axis=0)
        scores.append(_dot(qaug_s, kaug))
        values.append(page[LANES:].astype(BF16))
    o_s = finish(scores, values, new_key_score(ksn_ref), ksn_ref[0][:, LANES:])

    win = win_ref[0, 0]
    kaug = jnp.concatenate([win[:LANES].astype(BF16), extw_ref[...]], axis=0)
    s_w = _dot(qaug_w, kaug)
    dist_w = win_keep - lax.broadcasted_iota(jnp.int32, (1, win_keep), 1)
    s_w = jnp.where((dist_w >= 0) & (dist_w < WINDOW), s_w, NEG_INF)
    o_w = finish([s_w], [win[LANES:].astype(BF16)], new_key_score(kwn_ref), kwn_ref[0][:, LANES:])

    sig = _sigmoid(gl_ref[0])
    gate = [jnp.sum(jnp.where(lane8 == 3 * head + j, sig, 0.0), axis=1, keepdims=True) for j in range(3)]
    out = gate[0] * o_c + gate[1] * o_s + gate[2] * o_w
    out = jnp.where(head < GROUP, out, pltpu.roll(out, HEAD_DIM, axis=1))
    o_ref[0] = out[:, :HEAD_DIM]
    compress_pages(len(todo))
    kvc_buf[...] = _dot(mean_buf[...].astype(BF16), w2_ref[...])


def _attn_sample(cmp_t, sel_t, win_t, pt_flat, layer, pr, wts, ext_s, ext_w, db, n_pages, win_keep):
    past_len = n_pages * PAGE_SIZE
    n_cmp = past_len // CMP_BLOCK
    attn_row = lambda s: jnp.maximum(s - 1, 0)
    cmp_row = lambda s: jnp.minimum(s, db - 1)
    page = lambda row_of, p: pl.BlockSpec(
        (1, 1, D_KV, PAGE_SIZE), lambda s, pt: (layer, pt[row_of(s) * n_pages + p], 0, 0))
    row = lambda width: pl.BlockSpec((1, 1, width), lambda s, pt: (attn_row(s), 0, 0))
    const = lambda shape: pl.BlockSpec(shape, lambda s, pt: (0,) * len(shape))
    grid_spec = pltpu.PrefetchScalarGridSpec(
        num_scalar_prefetch=1,
        grid=(db + 1,),
        in_specs=[page(cmp_row, p) for p in range(n_pages)] + [page(attn_row, p) for p in range(n_pages)]
        + [row(Q_WIDE), row(LANES), row(D_KV), row(D_KV),
           pl.BlockSpec((1, 1, D_KV, win_keep), lambda s, pt: (layer, attn_row(s), 0, 0)),
           const((n_pages, LANES, PAGE_SIZE)), const((LANES, win_keep)),
           const((D_KV, LANES)), const((D_KV, D_KV)), const((D_KV, D_KV))],
        out_specs=pl.BlockSpec((1, N_HEADS, HEAD_DIM), lambda s, pt: (attn_row(s), 0, 0)),
        scratch_shapes=[pltpu.VMEM((n_cmp, D_KV), F32), pltpu.VMEM((n_cmp, D_KV), F32)],
    )
    kern = functools.partial(_attn_sample_kernel, n_pages=n_pages, past_len=past_len, win_keep=win_keep)
    out = pl.pallas_call(
        kern,
        grid_spec=grid_spec,
        out_shape=jax.ShapeDtypeStruct((db, N_HEADS, HEAD_DIM), F32),
        compiler_params=_params("arbitrary"),
        name="attn_sample",
    )(pt_flat, *([cmp_t] * n_pages), *([sel_t] * n_pages), pr["q"].reshape(db, 1, Q_WIDE),
      pr["gl"].reshape(db, 1, LANES), pr["ks"].reshape(db, 1, D_KV), pr["kw"].reshape(db, 1, D_KV),
      win_t, ext_s, ext_w, wts["pe_t"], wts["w1"], wts["w2"])
    return out.reshape(db, D_NSA)


def _win_update_kernel(win_ref, new_ref, o_ref, *, bb, win_keep):
    lane = lax.broadcasted_iota(jnp.int32, (D_KV, win_keep), 1)
    for k in range(bb):
        row = new_ref[0, k]
        col = jnp.concatenate([_row_to_col(row[:, h * LANES:(h + 1) * LANES]) for h in range(D_KV // LANES)],
                              axis=0)
        shifted = pltpu.roll(win_ref[0, k], win_keep - 1, axis=1)
        o_ref[0, k] = jnp.where(lane == win_keep - 1, col, shifted)


def _win_update(win_t, kw_new, bb):
    depth, db, _, win_keep = win_t.shape
    return pl.pallas_call(
        functools.partial(_win_update_kernel, bb=bb, win_keep=win_keep),
        grid=(depth, db // bb),
        in_specs=[pl.BlockSpec((1, bb, D_KV, win_keep), lambda l, b: (l, b, 0, 0)),
                  pl.BlockSpec((1, bb, 1, D_KV), lambda l, b: (l, b, 0, 0))],
        out_specs=pl.BlockSpec((1, bb, D_KV, win_keep), lambda l, b: (l, b, 0, 0)),
        out_shape=jax.ShapeDtypeStruct(win_t.shape, F32),
        compiler_params=_params("parallel", "parallel"),
        name="win_update",
    )(win_t, kw_new)


def _pool_window_lane(shape):
    lane = lax.broadcasted_iota(jnp.int32, shape, len(shape) - 1)
    w = jnp.full(shape, POOL_WINDOWS[0], jnp.int32)
    for gi in range(1, len(POOL_WINDOWS)):
        w = jnp.where(lane >= gi * POOL_GROUP_DIM, POOL_WINDOWS[gi], w)
    return w


def _group_rms(v, ones_bd):
    sq = v * v
    hi = sq.astype(BF16)
    lo = (sq - hi.astype(F32)).astype(BF16)
    return (_dot(hi, ones_bd) + _dot(lo, ones_bd)) * (1.0 / GMLP_GROUP_DIM)


def _mix_out(x, ya, za, yb, zb, yc, zc, wo_ref, gpost):
    out = _dot((ya * _silu(za)).astype(BF16), wo_ref[0:D_NSA, :])
    out += _dot((yb * _silu(zb)).astype(BF16), wo_ref[D_NSA:D_NSA + D_POOL, :])
    out += _dot((yc * _silu(zc)).astype(BF16), wo_ref[D_NSA + D_POOL:, :])
    ms = jnp.mean(out * out, axis=-1, keepdims=True)
    return x + out * lax.rsqrt(ms + EPS) * gpost


def _merge_prompt_kernel(x_ref, ya_ref, za_ref, pin_ref, prev_ref, zb_ref, u_ref, v_ref, zc_ref,
                         pw_ref, ps_ref, gn_ref, ones_ref, ws_ref, bs_ref, wo_ref, gp_ref,
                         xo_ref, vn_ref, xext, *, tm, tiles_per_seq):
    i = pl.program_id(0)
    first = (i % tiles_per_seq) == 0
    pin = pin_ref[...]
    xext[POOL_MAX:, :] = pin
    xext[0:POOL_MAX, :] = jnp.where(first, 0.0, prev_ref[...])
    w_lane = _pool_window_lane((tm, D_POOL))
    acc = pin
    for k in range(1, POOL_MAX):
        acc = acc + jnp.where(w_lane > k, xext[pl.ds(POOL_MAX - k, tm), :], 0.0)
    pos = (i % tiles_per_seq) * tm + lax.broadcasted_iota(jnp.int32, (tm, D_POOL), 0)
    cnt = jnp.minimum(w_lane, pos + 1).astype(F32)
    diff = acc / cnt - pin
    yb = _dot(diff.astype(BF16), pw_ref[...]) * ps_ref[...]

    v = v_ref[...]
    vn = v * lax.rsqrt(_group_rms(v, ones_ref[...]) + EPS) * gn_ref[...]
    vn_ref[...] = vn
    vnb = vn.astype(BF16)
    lane = lax.broadcasted_iota(jnp.int32, (CHUNK, D_GMLP), 1)
    chunks = []
    for c in range(tm // CHUNK):
        vc = vnb[c * CHUNK:(c + 1) * CHUNK]
        s = _dot(ws_ref[0], vc)
        for g in range(1, D_GMLP // GMLP_GROUP_DIM):
            s = jnp.where(lane >= g * GMLP_GROUP_DIM, _dot(ws_ref[g], vc), s)
        chunks.append(s + bs_ref[...])
    yc = u_ref[...] * jnp.concatenate(chunks, axis=0)

    xo_ref[...] = _mix_out(x_ref[...], ya_ref[...], za_ref[...], yb, zb_ref[...], yc, zc_ref[...],
                           wo_ref, gp_ref[...])


def _merge_prompt(x2, ya, pr, wts, seq, tm):
    m = x2.shape[0]
    tiles_per_seq = seq // tm
    per = tm // POOL_MAX
    rows = lambda width: pl.BlockSpec((tm, width), lambda i: (i, 0))
    const = lambda shape: pl.BlockSpec(shape, lambda i: (0,) * len(shape))
    kern = functools.partial(_merge_prompt_kernel, tm=tm, tiles_per_seq=tiles_per_seq)
    return pl.pallas_call(
        kern,
        grid=(m // tm,),
        in_specs=[rows(D_MODEL), rows(D_NSA), rows(D_NSA), rows(D_POOL),
                  pl.BlockSpec((POOL_MAX, D_POOL), lambda i: (jnp.maximum(i * per - 1, 0), 0)),
                  rows(D_POOL), rows(D_GMLP), rows(D_GMLP), rows(D_GMLP),
                  const((D_POOL, D_POOL)), const((1, D_POOL)), const((1, D_GMLP)),
                  const((D_GMLP, D_GMLP)), const((4, CHUNK, CHUNK)), const((CHUNK, D_GMLP)),
                  const((D_MODEL, D_MODEL)), const((1, D_MODEL))],
        out_specs=[rows(D_MODEL), rows(D_GMLP)],
        out_shape=[jax.ShapeDtypeStruct((m, D_MODEL), F32), jax.ShapeDtypeStruct((m, D_GMLP), F32)],
        scratch_shapes=[pltpu.VMEM((tm + POOL_MAX, D_POOL), F32)],
        compiler_params=_params("parallel"),
        name="merge_prompt",
    )(x2, ya, pr["za"], pr["pin"], pr["pin"], pr["zb"], pr["u"], pr["v"], pr["zc"],
      wts["pool_w"], wts["pool_scale"], wts["gmlp_norm"], wts["ones_bd"], wts["ws_tril"], wts["bs_full"],
      wts["w_out"], wts["norm_post"])


def _merge_sample_kernel(x_ref, ya_ref, za_ref, pin_ref, st_ref, zb_ref, u_ref, v_ref, zc_ref,
                         pw_ref, ps_ref, gn_ref, ones_ref, w0_ref, b0_ref, wo_ref, gp_ref,
                         xo_ref, vn_ref):
    pin = pin_ref[...]
    w_lane = _pool_window_lane(pin.shape)
    acc = pin
    for k in range(1, POOL_MAX):
        acc = acc + jnp.where(w_lane > k, st_ref[POOL_MAX - 1 - k], 0.0)
    diff = acc / w_lane.astype(F32) - pin
    yb = _dot(diff.astype(BF16), pw_ref[...]) * ps_ref[...]

    v = v_ref[...]
    vn = v * lax.rsqrt(_group_rms(v, ones_ref[...]) + EPS) * gn_ref[...]
    vn_ref[...] = vn
    yc = u_ref[...] * (w0_ref[...] * vn + b0_ref[...])

    xo_ref[...] = _mix_out(x_ref[...], ya_ref[...], za_ref[...], yb, zb_ref[...], yc, zc_ref[...],
                           wo_ref, gp_ref[...])


def _merge_sample(x2, ya, pr, state_t, wts):
    db = x2.shape[0]
    full = lambda shape: pl.BlockSpec(shape, lambda i: (0,) * len(shape))
    args = (x2, ya, pr["za"], pr["pin"], state_t, pr["zb"], pr["u"], pr["v"], pr["zc"],
            wts["pool_w"], wts["pool_scale"], wts["gmlp_norm"], wts["ones_bd"], wts["w0"], wts["b0"],
            wts["w_out"], wts["norm_post"])
    return pl.pallas_call(
        _merge_sample_kernel,
        grid=(1,),
        in_specs=[full(a.shape) for a in args],
        out_specs=[full((db, D_MODEL)), full((db, D_GMLP))],
        out_shape=[jax.ShapeDtypeStruct((db, D_MODEL), F32), jax.ShapeDtypeStruct((db, D_GMLP), F32)],
        compiler_params=_params("arbitrary"),
        name="merge_sample",
    )(*args)


def _layer_weights(l, norm_pre, w_in, cmp_pe, cmp_w1, cmp_w2, pool_w, pool_scale, gmlp_norm, gmlp_ws, gmlp_bs,
                   w_out, norm_post):
    n_g = D_GMLP // GMLP_GROUP_DIM
    ones_bd = jnp.kron(jnp.eye(n_g, dtype=F32), jnp.ones((GMLP_GROUP_DIM, GMLP_GROUP_DIM), F32)).astype(BF16)
    pw = pool_w[l]
    zp = jnp.zeros_like(pw[0])
    pool_bd = jnp.concatenate(
        [jnp.concatenate([pw[i] if j == i else zp for j in range(4)], axis=1) for i in range(4)], axis=0)
    tril = jnp.tril(jnp.ones((CHUNK, CHUNK), F32))
    return dict(
        norm_pre=norm_pre[l].reshape(1, D_MODEL),
        w_in=_prep_w_in(w_in[l]),
        pe_t=_pe_t(cmp_pe[l]),
        w1=_blockdiag4(cmp_w1[l]).astype(BF16),
        w2=_blockdiag4(cmp_w2[l]).astype(BF16),
        pool_w=pool_bd.astype(BF16),
        pool_scale=pool_scale[l].reshape(1, D_POOL),
        gmlp_norm=gmlp_norm[l].reshape(1, D_GMLP),
        ones_bd=ones_bd,
        ws_tril=(gmlp_ws[l] * tril).astype(BF16),
        bs_full=jnp.repeat(gmlp_bs[l].T, GMLP_GROUP_DIM, axis=1),
        w0=jnp.repeat(gmlp_ws[l][:, 0, 0], GMLP_GROUP_DIM).reshape(1, D_GMLP),
        b0=jnp.repeat(gmlp_bs[l][:, 0], GMLP_GROUP_DIM).reshape(1, D_GMLP),
        w_out=w_out[l].astype(BF16),
        norm_post=norm_post[l].reshape(1, D_MODEL),
    )


def _largest_tile(n, cap):
    t = cap
    while n % t:
        t //= 2
    return t


def _feature_major(cache):
    d, n, t = cache.shape[:3]
    return jnp.transpose(cache, (0, 1, 3, 4, 5, 2)).reshape(d, n, D_KV, t)


def _token_major(x_t):
    d, n, _, t = x_t.shape
    return jnp.transpose(x_t.reshape(d, n, 2, N_KV, HEAD_DIM, t), (0, 1, 5, 2, 3, 4))


def kernel(x_prompt, x_sample, cache_kv_cmp, cache_kv_sel, cache_kv_win, state_pool, page_table, norm_pre, w_in,
           cmp_pe, cmp_w1, cmp_w2, pool_w, pool_scale, gmlp_norm, gmlp_ws, gmlp_bs, w_out, norm_post):
    bp, tp, _ = x_prompt.shape
    db, ts, _ = x_sample.shape
    depth = w_in.shape[0]
    n_pages = page_table.shape[1]
    past_len = n_pages * PAGE_SIZE
    win_keep = cache_kv_win.shape[2]
    assert ts == 1 and tp % CHUNK == 0 and tp <= CMP_BLOCK * CMP_BLOCK and tp >= WINDOW
    assert past_len // CMP_BLOCK + 1 <= CMP_BLOCK and win_keep == WINDOW and db % LANES == 0

    mp = bp * tp
    tm_proj = _largest_tile(tp, 512)
    tile_cmp = _largest_tile(tp, 1024)
    tm_merge = _largest_tile(tp, 512)
    tq, tk = 128, _largest_tile(tp, 512)
    assert tp >= WINDOW + tq

    cmp_t = _feature_major(cache_kv_cmp)
    sel_t = _feature_major(cache_kv_sel)
    win_t = _feature_major(cache_kv_win)
    pt_flat = page_table.reshape(-1).astype(jnp.int32)

    ext_prompt = _prompt_key_lanes(jnp.arange(tp, dtype=jnp.int32))
    ext_s = jnp.transpose(
        _extra_rows(jnp.arange(past_len, dtype=jnp.int32), True).reshape(LANES, n_pages, PAGE_SIZE), (1, 0, 2))
    ext_w = _extra_rows(past_len - win_keep + jnp.arange(win_keep, dtype=jnp.int32), False)

    xp = x_prompt.reshape(mp, D_MODEL)
    xs = x_sample.reshape(db, D_MODEL)
    names = ("kvc_p", "kvc_s", "kvs_p", "kvs_s", "kvw_p", "kw_new", "pool_p", "pool_s", "gv_p", "gv_s")
    outs = {k: [] for k in names}
    for l in range(depth):
        wts = _layer_weights(l, norm_pre, w_in, cmp_pe, cmp_w1, cmp_w2, pool_w, pool_scale, gmlp_norm, gmlp_ws,
                             gmlp_bs, w_out, norm_post)
        pr = _project(xp, wts["norm_pre"], wts["w_in"], bp, tp, tm_proj, False)
        kvc = _compress_prompt(pr["kc_t"], wts, tile_cmp)
        ya = _attn_prompt(pr["q"], pr["gl"], kvc, pr["ks_t"], pr["kw_t"], ext_prompt, bp, tp, tq, tk)
        xp, vn = _merge_prompt(xp, ya, pr, wts, tp, tm_merge)
        outs["kvc_p"].append(pr["kc_t"])
        outs["kvs_p"].append(pr["ks_t"])
        outs["kvw_p"].append(pr["kw_t"][:, :, tp - WINDOW:])
        outs["pool_p"].append(pr["pin"].reshape(bp, tp, D_POOL)[:, tp - (POOL_MAX - 1):])
        outs["gv_p"].append(vn.reshape(bp, tp, D_GMLP)[:, tp - CHUNK:])
        pr = _project(xs, wts["norm_pre"], wts["w_in"], 1, db, db, True)
        ya = _attn_sample(cmp_t, sel_t, win_t, pt_flat, l, pr, wts, ext_s, ext_w, db, n_pages, win_keep)
        state_t = jnp.swapaxes(state_pool[l], 0, 1)
        xs, vn = _merge_sample(xs, ya, pr, state_t, wts)
        outs["kvc_s"].append(pr["kc_t"])
        outs["kvs_s"].append(pr["ks_t"])
        outs["kw_new"].append(pr["kw"].reshape(db, 1, D_KV))
        outs["pool_s"].append(jnp.concatenate([state_t[1:], pr["pin"][None]], axis=0))
        outs["gv_s"].append(vn.reshape(db, 1, D_GMLP))
    st = {k: jnp.stack(v) for k, v in outs.items()}
    kvw_s = _win_update(win_t, st["kw_new"], 4)
    sample_kv = lambda x_t: jnp.transpose(_token_major(x_t), (0, 2, 1, 3, 4, 5))
    return (xp.reshape(bp, tp, D_MODEL), xs.reshape(db, ts, D_MODEL),
            _token_major(st["kvc_p"]), sample_kv(st["kvc_s"]),
            _token_major(st["kvs_p"]), sample_kv(st["kvs_s"]),
            _token_major(st["kvw_p"]), _token_major(kvw_s),
            st["pool_p"], jnp.swapaxes(st["pool_s"], 1, 2), st["gv_p"], st["gv_s"])
```

```python
import functools

import jax
import jax.numpy as jnp
import ml_dtypes
import numpy as np
from jax import lax
from jax.experimental import pallas as pl
from jax.experimental.pallas import tpu as pltpu

F32 = jnp.float32
BF16 = jnp.bfloat16

D_MODEL = 1024
HEAD_DIM = 64
N_HEADS = 8
N_KV = 2
GROUP = N_HEADS // N_KV
D_NSA = N_HEADS * HEAD_DIM
D_KV = 2 * N_KV * HEAD_DIM
CMP_BLOCK = 64
N_SELECT = 16
WINDOW = 512
D_POOL = 256
POOL_WINDOWS = (2, 4, 8, 16)
POOL_GROUP_DIM = 64
POOL_MAX = 16
D_GMLP = 256
GMLP_GROUP_DIM = 64
CHUNK = 128
PAGE_SIZE = 128
EPS = 1e-6
NEG_INF = -1e30
SEL_FORCED = 1e4
SEL_INVALID = -1e4
Q_SCALE = HEAD_DIM ** -0.5

WIN_BLOCK = 128
SUB_KEYS = 256
LANES = 128
Q_WIDE = N_HEADS * LANES
VMEM_LIMIT = 56 * 1024 * 1024

ROW_ONE_A = 64
ROW_ONE_B = 65
ROW_POS_HI = 66
ROW_POS_LO = 67

LOG2E = float(np.log2(np.e))
LOG2E_HI = float(np.asarray(LOG2E, ml_dtypes.bfloat16))
LOG2E_LO = float(np.asarray(LOG2E - LOG2E_HI, ml_dtypes.bfloat16))
P_POS_HI = (64, 65)
P_POS_LO = (66, 67)
P_ONE = (68, 69, 70)
ONES_ROW = HEAD_DIM
V_ROWS = HEAD_DIM + 16

_SEG = dict(q=(0, 512), kc=(512, 256), ks=(768, 256), kw=(1024, 256), gl=(1280, 24), za=(1304, 512),
            pin=(1816, 256), zb=(2072, 256), u=(2328, 256), v=(2584, 256), zc=(2840, 256))
_PROJ_OUT = (("q", Q_WIDE), ("kc", 256), ("ks", 256), ("kw", 256), ("za", 512), ("pin", 256),
             ("zb", 256), ("u", 256), ("v", 256), ("zc", 256), ("gl", LANES))
_KV_NAMES = ("kc", "ks", "kw")


def _nt_dot(a, b):
    return lax.dot_general(a, b, (((1,), (1,)), ((), ())), preferred_element_type=F32)


def _tn_dot(a, b):
    return lax.dot_general(a, b, (((0,), (0,)), ((), ())), preferred_element_type=F32)


def _dot(a, b):
    return jnp.dot(a, b, preferred_element_type=F32)


def _sigmoid(x):
    return 0.5 * jnp.tanh(0.5 * x) + 0.5


def _silu(x):
    return x * _sigmoid(x)


def _params(*sem):
    return pltpu.CompilerParams(dimension_semantics=sem, vmem_limit_bytes=VMEM_LIMIT)


def _row_to_col(row):
    eye = (lax.broadcasted_iota(jnp.int32, (LANES, LANES), 0)
           == lax.broadcasted_iota(jnp.int32, (LANES, LANES), 1))
    return jnp.sum(jnp.where(eye, row, 0.0), axis=1, keepdims=True)


def _proj_kernel(x_ref, g_ref, w_ref, *out_refs, emit_rows):
    x = x_ref[...]
    ms = jnp.mean(x * x, axis=-1, keepdims=True)
    xn = (x * lax.rsqrt(ms + EPS) * g_ref[...]).astype(BF16)
    refs = iter(out_refs)
    off = 0
    for name, width in _PROJ_OUT:
        res = _dot(xn, w_ref[:, off:off + width])
        off += width
        if name in _KV_NAMES:
            next(refs)[0] = res.T
            if not emit_rows:
                continue
        next(refs)[...] = res


def _project(x2, g, w, batch, seq, tm, emit_rows):
    m = x2.shape[0]
    n_tot = w.shape[1]
    tiles = seq // tm
    names, shapes, specs = [], [], []
    for name, width in _PROJ_OUT:
        if name in _KV_NAMES:
            names.append(name + "_t")
            shapes.append(jax.ShapeDtypeStruct((batch, width, seq), F32))
            specs.append(pl.BlockSpec((1, width, tm), lambda i: (i // tiles, 0, i % tiles)))
            if not emit_rows:
                continue
        names.append(name)
        shapes.append(jax.ShapeDtypeStruct((m, width), F32))
        specs.append(pl.BlockSpec((tm, width), lambda i: (i, 0)))
    res = pl.pallas_call(
        functools.partial(_proj_kernel, emit_rows=emit_rows),
        grid=(m // tm,),
        in_specs=[pl.BlockSpec((tm, D_MODEL), lambda i: (i, 0)),
                  pl.BlockSpec((1, D_MODEL), lambda i: (0, 0)),
                  pl.BlockSpec((D_MODEL, n_tot), lambda i: (0, 0))],
        out_specs=specs,
        out_shape=shapes,
        compiler_params=_params("parallel"),
        name="norm_project",
    )(x2, g, w)
    return dict(zip(names, res))


def _prep_w_in(w):
    def seg(name):
        o, n = _SEG[name]
        return w[:, o:o + n]
    wq = seg("q").reshape(D_MODEL, N_HEADS, HEAD_DIM)
    zeros = jnp.zeros_like(wq)
    lo = jnp.concatenate([wq, zeros], axis=-1)
    hi = jnp.concatenate([zeros, wq], axis=-1)
    first = (jnp.arange(N_HEADS) < GROUP)[None, :, None]
    q_wide = jnp.where(first, lo, hi).reshape(D_MODEL, Q_WIDE)
    gl = jnp.pad(seg("gl"), ((0, 0), (0, LANES - 3 * N_HEADS)))
    cols = [q_wide] + [seg(n) for n, _ in _PROJ_OUT[1:-1]] + [gl]
    return jnp.concatenate(cols, axis=1).astype(BF16)


def _blockdiag4(w):
    z = jnp.zeros((HEAD_DIM, HEAD_DIM), w.dtype)
    blocks = [w[0], w[0], w[1], w[1]]
    rows = [jnp.concatenate([blocks[i] if j == i else z for j in range(4)], axis=1) for i in range(4)]
    return jnp.concatenate(rows, axis=0)


def _pe_t(pe):
    one = jnp.concatenate([pe[0].T, pe[0].T, pe[1].T, pe[1].T], axis=0)
    return jnp.concatenate([one, one], axis=1)


def _compress_mean(xt, pe128, w1):
    n = xt.shape[1]
    pet = jnp.concatenate([pe128] * (n // LANES), axis=1)
    xb = (xt + pet).T.astype(BF16)
    hid = _silu(_dot(xb, w1))
    return jnp.sum(hid.reshape(n // CMP_BLOCK, CMP_BLOCK, D_KV), axis=1) * (1.0 / CMP_BLOCK)


def _compress_t(xt, pe128, w1, w2):
    return _dot(_compress_mean(xt, pe128, w1).astype(BF16), w2)


def _compress_kernel(x_ref, pe_ref, w1_ref, w2_ref, o_ref):
    o_ref[...] = _compress_t(x_ref[0], pe_ref[...], w1_ref[...], w2_ref[...])


def _compress_prompt(kc_t, wts, tile):
    batch, _, seq = kc_t.shape
    tiles = seq // tile
    const = lambda shape: pl.BlockSpec(shape, lambda b, j: (0,) * len(shape))
    return pl.pallas_call(
        _compress_kernel,
        grid=(batch, tiles),
        in_specs=[pl.BlockSpec((1, D_KV, tile), lambda b, j: (b, 0, j)),
                  const((D_KV, LANES)), const((D_KV, D_KV)), const((D_KV, D_KV))],
        out_specs=pl.BlockSpec((tile // CMP_BLOCK, D_KV), lambda b, j: (b * tiles + j, 0)),
        out_shape=jax.ShapeDtypeStruct((batch * seq // CMP_BLOCK, D_KV), F32),
        compiler_params=_params("parallel", "parallel"),
        name="compress_prompt",
    )(kc_t, wts["pe_t"], wts["w1"], wts["w2"])


def _extra_rows(pos, with_onehot):
    row = jnp.arange(LANES)[:, None]
    blk = (pos // CMP_BLOCK)[None, :]
    hi = ((pos // CMP_BLOCK) * CMP_BLOCK).astype(F32)[None, :]
    lo = (pos % CMP_BLOCK).astype(F32)[None, :]
    out = jnp.where((row == ROW_ONE_A) | (row == ROW_ONE_B), 1.0, 0.0) + jnp.zeros_like(hi)
    out = jnp.where(row == ROW_POS_HI, hi, out)
    out = jnp.where(row == ROW_POS_LO, lo, out)
    if with_onehot:
        out = jnp.where((row < CMP_BLOCK) & (row == blk), 1.0, out)
    return out.astype(BF16)


def _prompt_key_lanes(pos):
    lane = jnp.arange(LANES)[None, :]
    hi = ((pos // CMP_BLOCK) * CMP_BLOCK).astype(F32)[:, None]
    lo = (pos % CMP_BLOCK).astype(F32)[:, None]
    out = jnp.where((lane < CMP_BLOCK) & (lane == (pos // CMP_BLOCK)[:, None]), 1.0, 0.0)
    out = jnp.where((lane == P_POS_HI[0]) | (lane == P_POS_HI[1]), hi, out)
    out = jnp.where((lane == P_POS_LO[0]) | (lane == P_POS_LO[1]), lo, out)
    out = jnp.where((lane >= P_ONE[0]) & (lane <= P_ONE[-1]), 1.0, out)
    return out.astype(BF16)


def _alibi_lanes(slope, t_hi, t_lo, shape):
    lane = lax.broadcasted_iota(jnp.int32, shape, 1)
    out = jnp.where(lane == ROW_ONE_A, -slope * t_hi, 0.0)
    out = jnp.where(lane == ROW_ONE_B, -slope * t_lo, out)
    return jnp.where((lane == ROW_POS_HI) | (lane == ROW_POS_LO), slope, out)


def _masked_softmax(s, mask, axis):
    s = jnp.where(mask, s, NEG_INF)
    p = jnp.exp(s - jnp.max(s, axis=axis, keepdims=True)) * mask.astype(F32)
    return p / jnp.maximum(jnp.sum(p, axis=axis, keepdims=True), 1e-30)


def _attn_prompt_kernel(q_ref, gl_ref, kvc_ref, ksk_ref, ksv_ref, kwk_ref, kwv_ref, ext_ref, o_ref,
                        kaug_s, kaug_w, v_s, v_w, s_buf, *, seq, tq, tk):
    i = pl.program_id(1)
    m = GROUP * tq
    n_cmp = seq // CMP_BLOCK

    n_wblk = WINDOW // WIN_BLOCK + tq // WIN_BLOCK

    def value_tile(v_t):
        n = v_t.shape[1]
        ones_row = (lax.broadcasted_iota(jnp.int32, (V_ROWS - HEAD_DIM, n), 0) == 0).astype(F32)
        return jnp.concatenate([v_t, ones_row], axis=0).astype(BF16)

    @pl.when(i == 0)
    def _fill():
        for c in range(seq // tk):
            cols = slice(c * tk, (c + 1) * tk)
            kaug_s[c] = jnp.concatenate([ksk_ref[0, :, cols].T.astype(BF16), ext_ref[cols, :]], axis=1)
            for kv in range(N_KV):
                v_s[c, kv] = value_tile(ksv_ref[0, kv * HEAD_DIM:(kv + 1) * HEAD_DIM, cols])
        for c in range(seq // WIN_BLOCK):
            cols = slice(c * WIN_BLOCK, (c + 1) * WIN_BLOCK)
            kaug_w[c] = jnp.concatenate([kwk_ref[0, :, cols].T.astype(BF16), ext_ref[cols, :]], axis=1)
            for kv in range(N_KV):
                v_w[c, kv] = value_tile(kwv_ref[0, kv * HEAD_DIM:(kv + 1) * HEAD_DIM, cols])

    q0 = i * tq
    lane_t = q0 + lax.broadcasted_iota(jnp.int32, (1, m), 1) % tq
    ext_row = lax.broadcasted_iota(jnp.int32, (LANES, m), 0)
    key_row = lax.broadcasted_iota(jnp.int32, (tk, 1), 0)
    sig_t = _sigmoid(gl_ref[...]).T

    blk_row = lax.broadcasted_iota(jnp.int32, (n_cmp, m), 0)
    dist_c = lane_t - ((blk_row + 1) * CMP_BLOCK - 1)
    mask_c = dist_c >= 0
    blk2 = lax.broadcasted_iota(jnp.int32, (n_cmp, tq), 0)
    cur = (q0 + lax.broadcasted_iota(jnp.int32, (1, tq), 1)) // CMP_BLOCK
    forced = (blk2 == cur) | (blk2 == 0)
    started = blk2 <= cur

    def pieces_of(n):
        return [(a, min(a + SUB_KEYS, n)) for a in range(0, n, SUB_KEYS)]

    def qk_scores(qaugs, k_tile):
        return [[_dot(k_tile[a:b], qaug_t) for qaug_t in qaugs] for a, b in pieces_of(k_tile.shape[0])]

    def softmax_update(s, v_piece, carry, mask):
        m_i, acc = carry
        if mask is not None:
            s = jnp.where(mask, s, NEG_INF)
        m_new = jnp.maximum(m_i, jnp.max(s, axis=0, keepdims=True))
        p = jnp.exp2(s - m_new).astype(BF16)
        return m_new, jnp.exp2(m_i - m_new) * acc + _dot(v_piece, p)

    def softmax_pv(scores, v_tiles, carries, mask_fn):
        carries = list(carries)
        for (a, b), piece_scores in zip(pieces_of(v_tiles[0].shape[1]), scores):
            for kv, s in enumerate(piece_scores):
                carries[kv] = softmax_update(s, v_tiles[kv][:, a:b], carries[kv], mask_fn(a, b))
        return tuple(carries)

    def normalized(carries):
        return [acc[:HEAD_DIM] / acc[ONES_ROW:ONES_ROW + 1] for _, acc in carries]

    def sel_fill(buf, j):
        for p, piece_scores in enumerate(qk_scores(qaugs_s, kaug_s[j])):
            for kv, s in enumerate(piece_scores):
                s_buf[buf, p * N_KV + kv] = s

    def sel_tile(buf, j, carries, causal):
        carries = list(carries)
        for p, (a, b) in enumerate(pieces_of(tk)):
            mask = (j * tk + key_row[a:b] <= lane_t) if causal else None
            for kv in range(N_KV):
                carries[kv] = softmax_update(s_buf[buf, p * N_KV + kv], v_s[j, kv, :, a:b], carries[kv], mask)
        return tuple(carries)

    def split3(x):
        hi = x.astype(BF16).astype(F32)
        mid = (x - hi).astype(BF16).astype(F32)
        return hi, mid, x - hi - mid

    init = (jnp.full((1, m), NEG_INF, F32), jnp.zeros((V_ROWS, m), F32))
    n_chunk = n_cmp // 8
    q4_ts, slope_rows, ali_ts, qaugs_w = [], [], [], []
    for kv in range(N_KV):
        qk = q_ref[:, kv * GROUP * LANES:(kv + 1) * GROUP * LANES]
        q_heads = [qk[:, g * LANES:(g + 1) * LANES].T for g in range(GROUP)]
        q4_t = jnp.concatenate([q * Q_SCALE for q in q_heads], axis=1).astype(BF16)
        q4l_t = jnp.concatenate([q * (Q_SCALE * LOG2E) for q in q_heads], axis=1).astype(BF16)
        slopes = [2.0 ** -(kv * GROUP + g + 1) for g in range(GROUP)]
        slope = jnp.concatenate([jnp.full((1, tq), s, F32) for s in slopes], axis=1)
        c_hi, c_lo = slope * LOG2E_HI, slope * LOG2E_LO
        d_hi, d_mid, d_lo = split3(-(c_hi + c_lo) * lane_t.astype(F32))
        ali_t = jnp.zeros((LANES, m), F32)
        for rows, val in ((P_POS_HI, (c_hi, c_lo)), (P_POS_LO, (c_hi, c_lo)), (P_ONE, (d_hi, d_mid, d_lo))):
            for r, v in zip(rows, val):
                ali_t = jnp.where(ext_row == r, v, ali_t)
        q4_ts.append(q4_t)
        slope_rows.append(slope)
        ali_ts.append((q4l_t, ali_t))
        qaugs_w.append(jnp.concatenate([q4l_t, ali_t.astype(BF16)], axis=0))

    c0 = jnp.maximum(q0 // WIN_BLOCK - WINDOW // WIN_BLOCK, 0)
    k_win = jnp.concatenate([kaug_w[c0 + t] for t in range(n_wblk)], axis=0)
    v_wins = [jnp.concatenate([v_w[c0 + t, kv] for t in range(n_wblk)], axis=1) for kv in range(N_KV)]
    scores_w = qk_scores(qaugs_w, k_win)

    o_cs, qaugs_s = [], []
    for kv in range(N_KV):
        q4_t, slope, (q4l_t, ali_t) = q4_ts[kv], slope_rows[kv], ali_ts[kv]
        kvc = kvc_ref[...]
        s_ct = _dot(kvc[:, :LANES].astype(BF16), q4_t) - slope * dist_c.astype(F32)
        p_ct = _masked_softmax(s_ct, mask_c, 0)
        o_c = _tn_dot(kvc[:, LANES:].astype(BF16), p_ct.astype(BF16))

        imp = p_ct[:, 0:tq]
        for g in range(1, GROUP):
            imp = imp + p_ct[:, g * tq:(g + 1) * tq]
        score = jnp.where(forced, SEL_FORCED, jnp.where(started, imp, SEL_INVALID))
        chunks = [score[8 * c:8 * c + 8] for c in range(n_chunk)]
        blk8 = lax.broadcasted_iota(jnp.int32, (8, tq), 0)
        ranks = [jnp.zeros((8, tq), F32) for _ in range(n_chunk)]
        for b in range(n_cmp):
            row = score[b:b + 1, :]
            for c in range(n_chunk):
                if 8 * c > b:
                    beats = row >= chunks[c]
                elif 8 * c + 7 < b:
                    beats = row > chunks[c]
                else:
                    beats = (row > chunks[c]) | ((row == chunks[c]) & (blk8 > b - 8 * c))
                ranks[c] = ranks[c] + jnp.where(beats, 1.0, 0.0)
        rank = jnp.concatenate(ranks, axis=0)
        sel_t = ((rank < N_SELECT) & (score > SEL_INVALID * 0.5)).astype(F32)
        sel_bias = (sel_t - 1.0) * -NEG_INF
        if n_cmp < LANES:
            sel_bias = jnp.concatenate([sel_bias, jnp.zeros((LANES - n_cmp, tq), F32)], axis=0)
        sel_bias = jnp.concatenate([sel_bias] * GROUP, axis=1)

        o_cs.append(o_c)
        qaugs_s.append(
            jnp.concatenate([q4l_t, jnp.where(ext_row < CMP_BLOCK, sel_bias, ali_t).astype(BF16)], axis=0))

    n_kt = (q0 + tq + tk - 1) // tk
    sel_fill(0, 0)

    key_w = c0 * WIN_BLOCK + lax.broadcasted_iota(jnp.int32, (n_wblk * WIN_BLOCK, 1), 0)

    def win_mask(a, b):
        causal = key_w[a:b] <= lane_t
        if a >= WIN_BLOCK:
            return causal
        return causal & (lane_t - key_w[a:b] < WINDOW)
    o_ws = normalized(softmax_pv(scores_w, v_wins, (init,) * N_KV, win_mask))

    n_full = n_kt - 1

    def sel_pair(jj, carries):
        j = 2 * jj
        sel_fill(1, j + 1)
        carries = sel_tile(0, j, carries, False)
        sel_fill(0, j + 2)
        return sel_tile(1, j + 1, carries, False)
    carries = lax.fori_loop(0, n_full // 2, sel_pair, (init,) * N_KV)
    j_a = 2 * (n_full // 2)
    odd = n_full - j_a
    sel_fill(1, n_kt - 1)
    carries = sel_tile(0, j_a, carries, True)
    carries = lax.cond(odd == 1, lambda c: sel_tile(1, n_kt - 1, c, True), lambda c: c, carries)
    o_ss = normalized(carries)

    heads = []
    for kv in range(N_KV):
        rows = slice(kv * HEAD_DIM, (kv + 1) * HEAD_DIM)
        for g in range(GROUP):
            h = kv * GROUP + g
            cols = slice(g * tq, (g + 1) * tq)
            heads.append(sig_t[3 * h:3 * h + 1] * o_cs[kv][rows, cols]
                         + sig_t[3 * h + 1:3 * h + 2] * o_ss[kv][:, cols]
                         + sig_t[3 * h + 2:3 * h + 3] * o_ws[kv][:, cols])
    o_ref[...] = jnp.concatenate(heads, axis=0).T


def _attn_prompt(qw, gl, kvc, ks_t, kw_t, ext, batch, seq, tq, tk):
    nq = seq // tq
    n_cmp = seq // CMP_BLOCK
    n_kt = seq // tk
    kern = functools.partial(_attn_prompt_kernel, seq=seq, tq=tq, tk=tk)
    kv_spec = lambda half: pl.BlockSpec((1, LANES, seq), lambda b, i: (b, half, 0))
    return pl.pallas_call(
        kern,
        grid=(batch, nq),
        in_specs=[pl.BlockSpec((tq, Q_WIDE), lambda b, i: (b * nq + i, 0)),
                  pl.BlockSpec((tq, LANES), lambda b, i: (b * nq + i, 0)),
                  pl.BlockSpec((n_cmp, D_KV), lambda b, i: (b, 0)),
                  kv_spec(0), kv_spec(1), kv_spec(0), kv_spec(1),
                  pl.BlockSpec((seq, LANES), lambda b, i: (0, 0))],
        out_specs=pl.BlockSpec((tq, D_NSA), lambda b, i: (b * nq + i, 0)),
        out_shape=jax.ShapeDtypeStruct((batch * seq, D_NSA), F32),
        scratch_shapes=[pltpu.VMEM((n_kt, tk, 2 * LANES), BF16),
                        pltpu.VMEM((seq // WIN_BLOCK, WIN_BLOCK, 2 * LANES), BF16),
                        pltpu.VMEM((n_kt, N_KV, V_ROWS, tk), BF16),
                        pltpu.VMEM((seq // WIN_BLOCK, N_KV, V_ROWS, WIN_BLOCK), BF16),
                        pltpu.VMEM((2, N_KV * tk // SUB_KEYS, SUB_KEYS, GROUP * tq), F32)],
        compiler_params=_params("arbitrary", "arbitrary"),
        name="attn_prompt",
    )(qw, gl, kvc, ks_t, ks_t, kw_t, kw_t, ext)


def _attn_sample_kernel(pt_ref, *refs, n_pages, past_len, win_keep):
    del pt_ref
    cmp_refs = refs[:n_pages]
    page_refs = refs[n_pages:2 * n_pages]
    (q_ref, gl_ref, ksn_ref, kwn_ref, win_ref, exts_ref, extw_ref, pe_ref, w1_ref, w2_ref,
     o_ref, kvc_buf, mean_buf) = refs[2 * n_pages:]
    n_cmp = past_len // CMP_BLOCK
    per_page = PAGE_SIZE // CMP_BLOCK

    @pl.when(pl.program_id(0) == 0)
    def _first():
        kvc_buf[...] = jnp.zeros(kvc_buf.shape, F32)

    kvc = kvc_buf[...]

    todo = list(range(n_pages))

    def compress_pages(count):
        for _ in range(min(count, len(todo))):
            p = todo.pop(0)
            mean_buf[p * per_page:(p + 1) * per_page, :] = _compress_mean(
                cmp_refs[p][0, 0], pe_ref[...], w1_ref[...])
    cur = past_len // CMP_BLOCK
    t_hi = float(cur * CMP_BLOCK)
    t_lo = float(past_len % CMP_BLOCK)

    qrow = q_ref[0]
    q8f = jnp.concatenate([qrow[:, h * LANES:(h + 1) * LANES] for h in range(N_HEADS)], axis=0) * Q_SCALE
    q8 = q8f.astype(BF16)
    head = lax.broadcasted_iota(jnp.int32, (N_HEADS, 1), 0)
    slope = jnp.exp2(-(head + 1).astype(F32))
    lane8 = lax.broadcasted_iota(jnp.int32, (N_HEADS, LANES), 1)
    ali = _alibi_lanes(slope, t_hi, t_lo, (N_HEADS, LANES))

    kc = kvc[:, :LANES]
    if n_cmp < LANES:
        kc = jnp.concatenate([kc, jnp.zeros((LANES - n_cmp, LANES), F32)], axis=0)
    blk = lax.broadcasted_iota(jnp.int32, (1, LANES), 1)
    dist_c = past_len - ((blk + 1) * CMP_BLOCK - 1)
    mask_c = (dist_c >= 0) & (blk < n_cmp)
    s_c = _nt_dot(q8, kc.astype(BF16)) - slope * dist_c.astype(F32)
    compress_pages(2)
    p_c = _masked_softmax(s_c, mask_c, 1)
    o_c = _dot(p_c[:, :n_cmp].astype(BF16), kvc[:, LANES:].astype(BF16))

    forced = (blk == cur) | (blk == 0)
    started = blk <= cur
    below = (lax.broadcasted_iota(jnp.int32, (LANES, LANES), 0)
             < lax.broadcasted_iota(jnp.int32, (LANES, LANES), 1))
    sel_rows = []
    for kv in range(N_KV):
        imp = jnp.sum(p_c[kv * GROUP:(kv + 1) * GROUP], axis=0, keepdims=True)
        score = jnp.where(forced, SEL_FORCED, jnp.where(started, imp, SEL_INVALID))
        col = _row_to_col(score)
        compress_pages(1)
        beats = (col > score) | ((col == score) & below)
        rank = jnp.sum(beats.astype(F32), axis=0, keepdims=True)
        sel = ((rank < N_SELECT) & (score > SEL_INVALID * 0.5)).astype(F32)
        sel_rows += [sel] * GROUP
    sel_bias = (jnp.concatenate(sel_rows, axis=0) - 1.0) * -NEG_INF
    qaug_s = jnp.concatenate([q8, jnp.where(lane8 < CMP_BLOCK, sel_bias, ali).astype(BF16)], axis=1)
    qaug_w = jnp.concatenate([q8, ali.astype(BF16)], axis=1)

    def new_key_score(row_ref):
        k_new = row_ref[0][:, :LANES].astype(BF16).astype(F32)
        return jnp.sum(q8.astype(F32) * k_new, axis=1, keepdims=True)

    def finish(scores, values, s_new, v_new):
        m = s_new
        for s in scores:
            m = jnp.maximum(m, jnp.max(s, axis=1, keepdims=True))
        p_new = jnp.exp(s_new - m)
        l = p_new
        acc = p_new * v_new.astype(BF16).astype(F32)
        for i, (s, v) in enumerate(zip(scores, values)):
            p = jnp.exp(s - m)
            l = l + jnp.sum(p, axis=1, keepdims=True)
            acc = acc + _nt_dot(p.astype(BF16), v)
            if i % 4 == 3:
                compress_pages(1)
        return acc / l

    scores, values = [], []
    for p, ref in enumerate(page_refs):
        if p % 2 == 0:
            compress_pages(1)
        page = ref[0, 0]
        kaug = jnp.concatenate([page[:LANES].astype(BF16), exts_ref[p]], axis=0)
        scores.append(_dot(qaug_s, kaug))
        values.append(page[LANES:].astype(BF16))
    o_s = finish(scores, values, new_key_score(ksn_ref), ksn_ref[0][:, LANES:])

    win = win_ref[0, 0]
    kaug = jnp.concatenate([win[:LANES].astype(BF16), extw_ref[...]], axis=0)
    s_w = _dot(qaug_w, kaug)
    dist_w = win_keep - lax.broadcasted_iota(jnp.int32, (1, win_keep), 1)
    s_w = jnp.where((dist_w >= 0) & (dist_w < WINDOW), s_w, NEG_INF)
    o_w = finish([s_w], [win[LANES:].astype(BF16)], new_key_score(kwn_ref), kwn_ref[0][:, LANES:])

    sig = _sigmoid(gl_ref[0])
    gate = [jnp.sum(jnp.where(lane8 == 3 * head + j, sig, 0.0), axis=1, keepdims=True) for j in range(3)]
    out = gate[0] * o_c + gate[1] * o_s + gate[2] * o_w
    out = jnp.where(head < GROUP, out, pltpu.roll(out, HEAD_DIM, axis=1))
    o_ref[0] = out[:, :HEAD_DIM]
    compress_pages(len(todo))
    kvc_buf[...] = _dot(mean_buf[...].astype(BF16), w2_ref[...])


def _attn_sample(cmp_t, sel_t, win_t, pt_flat, layer, pr, wts, ext_s, ext_w, db, n_pages, win_keep):
    past_len = n_pages * PAGE_SIZE
    n_cmp = past_len // CMP_BLOCK
    attn_row = lambda s: jnp.maximum(s - 1, 0)
    cmp_row = lambda s: jnp.minimum(s, db - 1)
    page = lambda row_of, p: pl.BlockSpec(
        (1, 1, D_KV, PAGE_SIZE), lambda s, pt: (layer, pt[row_of(s) * n_pages + p], 0, 0))
    row = lambda width: pl.BlockSpec((1, 1, width), lambda s, pt: (attn_row(s), 0, 0))
    const = lambda shape: pl.BlockSpec(shape, lambda s, pt: (0,) * len(shape))
    grid_spec = pltpu.PrefetchScalarGridSpec(
        num_scalar_prefetch=1,
        grid=(db + 1,),
        in_specs=[page(cmp_row, p) for p in range(n_pages)] + [page(attn_row, p) for p in range(n_pages)]
        + [row(Q_WIDE), row(LANES), row(D_KV), row(D_KV),
           pl.BlockSpec((1, 1, D_KV, win_keep), lambda s, pt: (layer, attn_row(s), 0, 0)),
           const((n_pages, LANES, PAGE_SIZE)), const((LANES, win_keep)),
           const((D_KV, LANES)), const((D_KV, D_KV)), const((D_KV, D_KV))],
        out_specs=pl.BlockSpec((1, N_HEADS, HEAD_DIM), lambda s, pt: (attn_row(s), 0, 0)),
        scratch_shapes=[pltpu.VMEM((n_cmp, D_KV), F32), pltpu.VMEM((n_cmp, D_KV), F32)],
    )
    kern = functools.partial(_attn_sample_kernel, n_pages=n_pages, past_len=past_len, win_keep=win_keep)
    out = pl.pallas_call(
        kern,
        grid_spec=grid_spec,
        out_shape=jax.ShapeDtypeStruct((db, N_HEADS, HEAD_DIM), F32),
        compiler_params=_params("arbitrary"),
        name="attn_sample",
    )(pt_flat, *([cmp_t] * n_pages), *([sel_t] * n_pages), pr["q"].reshape(db, 1, Q_WIDE),
      pr["gl"].reshape(db, 1, LANES), pr["ks"].reshape(db, 1, D_KV), pr["kw"].reshape(db, 1, D_KV),
      win_t, ext_s, ext_w, wts["pe_t"], wts["w1"], wts["w2"])
    return out.reshape(db, D_NSA)


def _win_update_kernel(win_ref, new_ref, o_ref, *, bb, win_keep):
    lane = lax.broadcasted_iota(jnp.int32, (D_KV, win_keep), 1)
    for k in range(bb):
        row = new_ref[0, k]
        col = jnp.concatenate([_row_to_col(row[:, h * LANES:(h + 1) * LANES]) for h in range(D_KV // LANES)],
                              axis=0)
        shifted = pltpu.roll(win_ref[0, k], win_keep - 1, axis=1)
        o_ref[0, k] = jnp.where(lane == win_keep - 1, col, shifted)


def _win_update(win_t, kw_new, bb):
    depth, db, _, win_keep = win_t.shape
    return pl.pallas_call(
        functools.partial(_win_update_kernel, bb=bb, win_keep=win_keep),
        grid=(depth, db // bb),
        in_specs=[pl.BlockSpec((1, bb, D_KV, win_keep), lambda l, b: (l, b, 0, 0)),
                  pl.BlockSpec((1, bb, 1, D_KV), lambda l, b: (l, b, 0, 0))],
        out_specs=pl.BlockSpec((1, bb, D_KV, win_keep), lambda l, b: (l, b, 0, 0)),
        out_shape=jax.ShapeDtypeStruct(win_t.shape, F32),
        compiler_params=_params("parallel", "parallel"),
        name="win_update",
    )(win_t, kw_new)


def _pool_window_lane(shape):
    lane = lax.broadcasted_iota(jnp.int32, shape, len(shape) - 1)
    w = jnp.full(shape, POOL_WINDOWS[0], jnp.int32)
    for gi in range(1, len(POOL_WINDOWS)):
        w = jnp.where(lane >= gi * POOL_GROUP_DIM, POOL_WINDOWS[gi], w)
    return w


def _group_rms(v, ones_bd):
    sq = v * v
    hi = sq.astype(BF16)
    lo = (sq - hi.astype(F32)).astype(BF16)
    return (_dot(hi, ones_bd) + _dot(lo, ones_bd)) * (1.0 / GMLP_GROUP_DIM)


def _mix_out(x, ya, za, yb, zb, yc, zc, wo_ref, gpost):
    out = _dot((ya * _silu(za)).astype(BF16), wo_ref[0:D_NSA, :])
    out += _dot((yb * _silu(zb)).astype(BF16), wo_ref[D_NSA:D_NSA + D_POOL, :])
    out += _dot((yc * _silu(zc)).astype(BF16), wo_ref[D_NSA + D_POOL:, :])
    ms = jnp.mean(out * out, axis=-1, keepdims=True)
    return x + out * lax.rsqrt(ms + EPS) * gpost


def _merge_prompt_kernel(x_ref, ya_ref, za_ref, pin_ref, prev_ref, zb_ref, u_ref, v_ref, zc_ref,
                         pw_ref, ps_ref, gn_ref, ones_ref, ws_ref, bs_ref, wo_ref, gp_ref,
                         xo_ref, vn_ref, buf_a, buf_b, *, tm, tiles_per_seq):
    i = pl.program_id(0)
    first = (i % tiles_per_seq) == 0
    pin = pin_ref[...]
    lo, n = POOL_MAX, tm + POOL_MAX
    buf_a[0:lo, :] = jnp.zeros((lo, D_POOL), F32)
    buf_b[0:lo, :] = jnp.zeros((lo, D_POOL), F32)
    buf_a[lo:2 * lo, :] = jnp.where(first, 0.0, prev_ref[...])
    buf_a[2 * lo:, :] = pin
    s2 = buf_a[pl.ds(lo, n), :] + buf_a[pl.ds(lo - 1, n), :]
    buf_b[lo:, :] = s2
    s4 = s2 + buf_b[pl.ds(lo - 2, n), :]
    buf_a[lo:, :] = s4
    s8 = s4 + buf_a[pl.ds(lo - 4, n), :]
    buf_b[lo:, :] = s8
    s16 = s8[lo:] + buf_b[pl.ds(2 * lo - 8, tm), :]
    w_lane = _pool_window_lane((tm, D_POOL))
    acc = jnp.where(w_lane == POOL_WINDOWS[0], s2[lo:],
                    jnp.where(w_lane == POOL_WINDOWS[1], s4[lo:], jnp.where(w_lane == POOL_WINDOWS[2], s8[lo:], s16)))
    pos = (i % tiles_per_seq) * tm + lax.broadcasted_iota(jnp.int32, (tm, D_POOL), 0)
    cnt = jnp.minimum(w_lane, pos + 1).astype(F32)
    diff = acc / cnt - pin
    yb = _dot(diff.astype(BF16), pw_ref[...]) * ps_ref[...]

    v = v_ref[...]
    vn = v * lax.rsqrt(_group_rms(v, ones_ref[...]) + EPS) * gn_ref[...]
    vn_ref[...] = vn
    vnb = vn.astype(BF16)
    lane = lax.broadcasted_iota(jnp.int32, (CHUNK, D_GMLP), 1)
    chunks = []
    for c in range(tm // CHUNK):
        vc = vnb[c * CHUNK:(c + 1) * CHUNK]
        s = _dot(ws_ref[0], vc)
        for g in range(1, D_GMLP // GMLP_GROUP_DIM):
            s = jnp.where(lane >= g * GMLP_GROUP_DIM, _dot(ws_ref[g], vc), s)
        chunks.append(s + bs_ref[...])
    yc = u_ref[...] * jnp.concatenate(chunks, axis=0)

    xo_ref[...] = _mix_out(x_ref[...], ya_ref[...], za_ref[...], yb, zb_ref[...], yc, zc_ref[...],
                           wo_ref, gp_ref[...])


def _merge_prompt(x2, ya, pr, wts, seq, tm):
    m = x2.shape[0]
    tiles_per_seq = seq // tm
    per = tm // POOL_MAX
    rows = lambda width: pl.BlockSpec((tm, width), lambda i: (i, 0))
    const = lambda shape: pl.BlockSpec(shape, lambda i: (0,) * len(shape))
    kern = functools.partial(_merge_prompt_kernel, tm=tm, tiles_per_seq=tiles_per_seq)
    return pl.pallas_call(
        kern,
        grid=(m // tm,),
        in_specs=[rows(D_MODEL), rows(D_NSA), rows(D_NSA), rows(D_POOL),
                  pl.BlockSpec((POOL_MAX, D_POOL), lambda i: (jnp.maximum(i * per - 1, 0), 0)),
                  rows(D_POOL), rows(D_GMLP), rows(D_GMLP), rows(D_GMLP),
                  const((D_POOL, D_POOL)), const((1, D_POOL)), const((1, D_GMLP)),
                  const((D_GMLP, D_GMLP)), const((4, CHUNK, CHUNK)), const((CHUNK, D_GMLP)),
                  const((D_MODEL, D_MODEL)), const((1, D_MODEL))],
        out_specs=[rows(D_MODEL), rows(D_GMLP)],
        out_shape=[jax.ShapeDtypeStruct((m, D_MODEL), F32), jax.ShapeDtypeStruct((m, D_GMLP), F32)],
        scratch_shapes=[pltpu.VMEM((tm + 2 * POOL_MAX, D_POOL), F32), pltpu.VMEM((tm + 2 * POOL_MAX, D_POOL), F32)],
        compiler_params=_params("parallel"),
        name="merge_prompt",
    )(x2, ya, pr["za"], pr["pin"], pr["pin"], pr["zb"], pr["u"], pr["v"], pr["zc"],
      wts["pool_w"], wts["pool_scale"], wts["gmlp_norm"], wts["ones_bd"], wts["ws_tril"], wts["bs_full"],
      wts["w_out"], wts["norm_post"])


def _merge_sample_kernel(x_ref, ya_ref, za_ref, pin_ref, st_ref, zb_ref, u_ref, v_ref, zc_ref,
                         pw_ref, ps_ref, gn_ref, ones_ref, w0_ref, b0_ref, wo_ref, gp_ref,
                         xo_ref, vn_ref):
    pin = pin_ref[...]
    w_lane = _pool_window_lane(pin.shape)
    acc = pin
    for k in range(1, POOL_MAX):
        acc = acc + jnp.where(w_lane > k, st_ref[POOL_MAX - 1 - k], 0.0)
    diff = acc / w_lane.astype(F32) - pin
    yb = _dot(diff.astype(BF16), pw_ref[...]) * ps_ref[...]

    v = v_ref[...]
    vn = v * lax.rsqrt(_group_rms(v, ones_ref[...]) + EPS) * gn_ref[...]
    vn_ref[...] = vn
    yc = u_ref[...] * (w0_ref[...] * vn + b0_ref[...])

    xo_ref[...] = _mix_out(x_ref[...], ya_ref[...], za_ref[...], yb, zb_ref[...], yc, zc_ref[...],
                           wo_ref, gp_ref[...])


def _merge_sample(x2, ya, pr, state_t, wts):
    db = x2.shape[0]
    full = lambda shape: pl.BlockSpec(shape, lambda i: (0,) * len(shape))
    args = (x2, ya, pr["za"], pr["pin"], state_t, pr["zb"], pr["u"], pr["v"], pr["zc"],
            wts["pool_w"], wts["pool_scale"], wts["gmlp_norm"], wts["ones_bd"], wts["w0"], wts["b0"],
            wts["w_out"], wts["norm_post"])
    return pl.pallas_call(
        _merge_sample_kernel,
        grid=(1,),
        in_specs=[full(a.shape) for a in args],
        out_specs=[full((db, D_MODEL)), full((db, D_GMLP))],
        out_shape=[jax.ShapeDtypeStruct((db, D_MODEL), F32), jax.ShapeDtypeStruct((db, D_GMLP), F32)],
        compiler_params=_params("arbitrary"),
        name="merge_sample",
    )(*args)


def _layer_weights(l, norm_pre, w_in, cmp_pe, cmp_w1, cmp_w2, pool_w, pool_scale, gmlp_norm, gmlp_ws, gmlp_bs,
                   w_out, norm_post):
    n_g = D_GMLP // GMLP_GROUP_DIM
    ones_bd = jnp.kron(jnp.eye(n_g, dtype=F32), jnp.ones((GMLP_GROUP_DIM, GMLP_GROUP_DIM), F32)).astype(BF16)
    pw = pool_w[l]
    zp = jnp.zeros_like(pw[0])
    pool_bd = jnp.concatenate(
        [jnp.concatenate([pw[i] if j == i else zp for j in range(4)], axis=1) for i in range(4)], axis=0)
    tril = jnp.tril(jnp.ones((CHUNK, CHUNK), F32))
    return dict(
        norm_pre=norm_pre[l].reshape(1, D_MODEL),
        w_in=_prep_w_in(w_in[l]),
        pe_t=_pe_t(cmp_pe[l]),
        w1=_blockdiag4(cmp_w1[l]).astype(BF16),
        w2=_blockdiag4(cmp_w2[l]).astype(BF16),
        pool_w=pool_bd.astype(BF16),
        pool_scale=pool_scale[l].reshape(1, D_POOL),
        gmlp_norm=gmlp_norm[l].reshape(1, D_GMLP),
        ones_bd=ones_bd,
        ws_tril=(gmlp_ws[l] * tril).astype(BF16),
        bs_full=jnp.repeat(gmlp_bs[l].T, GMLP_GROUP_DIM, axis=1),
        w0=jnp.repeat(gmlp_ws[l][:, 0, 0], GMLP_GROUP_DIM).reshape(1, D_GMLP),
        b0=jnp.repeat(gmlp_bs[l][:, 0], GMLP_GROUP_DIM).reshape(1, D_GMLP),
        w_out=w_out[l].astype(BF16),
        norm_post=norm_post[l].reshape(1, D_MODEL),
    )


def _largest_tile(n, cap):
    t = cap
    while n % t:
        t //= 2
    return t


def _feature_major(cache):
    d, n, t = cache.shape[:3]
    return jnp.transpose(cache, (0, 1, 3, 4, 5, 2)).reshape(d, n, D_KV, t)


def _token_major(x_t):
    d, n, _, t = x_t.shape
    return jnp.transpose(x_t.reshape(d, n, 2, N_KV, HEAD_DIM, t), (0, 1, 5, 2, 3, 4))


def kernel(x_prompt, x_sample, cache_kv_cmp, cache_kv_sel, cache_kv_win, state_pool, page_table, norm_pre, w_in,
           cmp_pe, cmp_w1, cmp_w2, pool_w, pool_scale, gmlp_norm, gmlp_ws, gmlp_bs, w_out, norm_post):
    bp, tp, _ = x_prompt.shape
    db, ts, _ = x_sample.shape
    depth = w_in.shape[0]
    n_pages = page_table.shape[1]
    past_len = n_pages * PAGE_SIZE
    win_keep = cache_kv_win.shape[2]
    assert ts == 1 and tp % CHUNK == 0 and tp <= CMP_BLOCK * CMP_BLOCK and tp >= WINDOW
    assert past_len // CMP_BLOCK + 1 <= CMP_BLOCK and win_keep == WINDOW and db % LANES == 0

    mp = bp * tp
    tm_proj = _largest_tile(tp, 512)
    tile_cmp = _largest_tile(tp, 1024)
    tm_merge = _largest_tile(tp, 512)
    tq, tk = 128, _largest_tile(tp, 512)
    assert tp >= WINDOW + tq

    cmp_t = _feature_major(cache_kv_cmp)
    sel_t = _feature_major(cache_kv_sel)
    win_t = _feature_major(cache_kv_win)
    pt_flat = page_table.reshape(-1).astype(jnp.int32)

    ext_prompt = _prompt_key_lanes(jnp.arange(tp, dtype=jnp.int32))
    ext_s = jnp.transpose(
        _extra_rows(jnp.arange(past_len, dtype=jnp.int32), True).reshape(LANES, n_pages, PAGE_SIZE), (1, 0, 2))
    ext_w = _extra_rows(past_len - win_keep + jnp.arange(win_keep, dtype=jnp.int32), False)

    xp = x_prompt.reshape(mp, D_MODEL)
    xs = x_sample.reshape(db, D_MODEL)
    names = ("kvc_p", "kvc_s", "kvs_p", "kvs_s", "kvw_p", "kw_new", "pool_p", "pool_s", "gv_p", "gv_s")
    outs = {k: [] for k in names}
    for l in range(depth):
        wts = _layer_weights(l, norm_pre, w_in, cmp_pe, cmp_w1, cmp_w2, pool_w, pool_scale, gmlp_norm, gmlp_ws,
                             gmlp_bs, w_out, norm_post)
        pr = _project(xp, wts["norm_pre"], wts["w_in"], bp, tp, tm_proj, False)
        kvc = _compress_prompt(pr["kc_t"], wts, tile_cmp)
        ya = _attn_prompt(pr["q"], pr["gl"], kvc, pr["ks_t"], pr["kw_t"], ext_prompt, bp, tp, tq, tk)
        xp, vn = _merge_prompt(xp, ya, pr, wts, tp, tm_merge)
        outs["kvc_p"].append(pr["kc_t"])
        outs["kvs_p"].append(pr["ks_t"])
        outs["kvw_p"].append(pr["kw_t"][:, :, tp - WINDOW:])
        outs["pool_p"].append(pr["pin"].reshape(bp, tp, D_POOL)[:, tp - (POOL_MAX - 1):])
        outs["gv_p"].append(vn.reshape(bp, tp, D_GMLP)[:, tp - CHUNK:])
        pr = _project(xs, wts["norm_pre"], wts["w_in"], 1, db, db, True)
        ya = _attn_sample(cmp_t, sel_t, win_t, pt_flat, l, pr, wts, ext_s, ext_w, db, n_pages, win_keep)
        state_t = jnp.swapaxes(state_pool[l], 0, 1)
        xs, vn = _merge_sample(xs, ya, pr, state_t, wts)
        outs["kvc_s"].append(pr["kc_t"])
        outs["kvs_s"].append(pr["ks_t"])
        outs["kw_new"].append(pr["kw"].reshape(db, 1, D_KV))
        outs["pool_s"].append(jnp.concatenate([state_t[1:], pr["pin"][None]], axis=0))
        outs["gv_s"].append(vn.reshape(db, 1, D_GMLP))
    st = {k: jnp.stack(v) for k, v in outs.items()}
    kvw_s = _win_update(win_t, st["kw_new"], _largest_tile(db, 8))
    sample_kv = lambda x_t: jnp.transpose(_token_major(x_t), (0, 2, 1, 3, 4, 5))
    return (xp.reshape(bp, tp, D_MODEL), xs.reshape(db, ts, D_MODEL),
            _token_major(st["kvc_p"]), sample_kv(st["kvc_s"]),
            _token_major(st["kvs_p"]), sample_kv(st["kvs_s"]),
            _token_major(st["kvw_p"]), _token_major(kvw_s),
            st["pool_p"], jnp.swapaxes(st["pool_s"], 1, 2), st["gv_p"], st["gv_s"])
```

```python
import functools

import jax
import jax.numpy as jnp
import ml_dtypes
import numpy as np
from jax import lax
from jax.experimental import pallas as pl
from jax.experimental.pallas import tpu as pltpu

F32 = jnp.float32
BF16 = jnp.bfloat16

D_MODEL = 1024
HEAD_DIM = 64
N_HEADS = 8
N_KV = 2
GROUP = N_HEADS // N_KV
D_NSA = N_HEADS * HEAD_DIM
D_KV = 2 * N_KV * HEAD_DIM
CMP_BLOCK = 64
N_SELECT = 16
WINDOW = 512
D_POOL = 256
POOL_WINDOWS = (2, 4, 8, 16)
POOL_GROUP_DIM = 64
POOL_MAX = 16
D_GMLP = 256
GMLP_GROUP_DIM = 64
CHUNK = 128
PAGE_SIZE = 128
EPS = 1e-6
NEG_INF = -1e30
SEL_FORCED = 1e4
SEL_INVALID = -1e4
Q_SCALE = HEAD_DIM ** -0.5

WIN_BLOCK = 128
SUB_KEYS = 256
LANES = 128
VMEM_LIMIT = 56 * 1024 * 1024

ROW_ONE_A = 64
ROW_ONE_B = 65
ROW_POS_HI = 66
ROW_POS_LO = 67

LOG2E = float(np.log2(np.e))
LOG2E_HI = float(np.asarray(LOG2E, ml_dtypes.bfloat16))
LOG2E_LO = float(np.asarray(LOG2E - LOG2E_HI, ml_dtypes.bfloat16))
P_POS_HI = (64, 65)
P_POS_LO = (66, 67)
P_ONE = (68, 69, 70)
ONES_ROW = HEAD_DIM
V_ROWS = HEAD_DIM + 16

_SEG = dict(q=(0, 512), kc=(512, 256), ks=(768, 256), kw=(1024, 256), gl=(1280, 24), za=(1304, 512),
            pin=(1816, 256), zb=(2072, 256), u=(2328, 256), v=(2584, 256), zc=(2840, 256))
_PROJ_OUT = (("q", D_NSA), ("kc", 256), ("ks", 256), ("kw", 256), ("za", 512), ("pin", 256),
             ("zb", 256), ("u", 256), ("v", 256), ("zc", 256), ("gl", LANES))
_KV_NAMES = ("kc", "ks", "kw")


def _nt_dot(a, b):
    return lax.dot_general(a, b, (((1,), (1,)), ((), ())), preferred_element_type=F32)


def _tn_dot(a, b):
    return lax.dot_general(a, b, (((0,), (0,)), ((), ())), preferred_element_type=F32)


def _dot(a, b):
    return jnp.dot(a, b, preferred_element_type=F32)


def _sigmoid(x):
    return 0.5 * jnp.tanh(0.5 * x) + 0.5


def _silu(x):
    return x * _sigmoid(x)


def _params(*sem):
    return pltpu.CompilerParams(dimension_semantics=sem, vmem_limit_bytes=VMEM_LIMIT)


def _row_to_col(row):
    eye = (lax.broadcasted_iota(jnp.int32, (LANES, LANES), 0)
           == lax.broadcasted_iota(jnp.int32, (LANES, LANES), 1))
    return jnp.sum(jnp.where(eye, row, 0.0), axis=1, keepdims=True)


def _proj_kernel(x_ref, g_ref, w_ref, *out_refs, emit_rows):
    x = x_ref[...]
    ms = jnp.mean(x * x, axis=-1, keepdims=True)
    xn = (x * lax.rsqrt(ms + EPS) * g_ref[...]).astype(BF16)
    refs = iter(out_refs)
    off = 0
    for name, width in _PROJ_OUT:
        res = _dot(xn, w_ref[:, off:off + width])
        off += width
        if name in _KV_NAMES:
            next(refs)[0] = res.T
            if not emit_rows:
                continue
        next(refs)[...] = res


def _project(x2, g, w, batch, seq, tm, emit_rows):
    m = x2.shape[0]
    n_tot = w.shape[1]
    tiles = seq // tm
    names, shapes, specs = [], [], []
    for name, width in _PROJ_OUT:
        if name in _KV_NAMES:
            names.append(name + "_t")
            shapes.append(jax.ShapeDtypeStruct((batch, width, seq), F32))
            specs.append(pl.BlockSpec((1, width, tm), lambda i: (i // tiles, 0, i % tiles)))
            if not emit_rows:
                continue
        names.append(name)
        shapes.append(jax.ShapeDtypeStruct((m, width), F32))
        specs.append(pl.BlockSpec((tm, width), lambda i: (i, 0)))
    res = pl.pallas_call(
        functools.partial(_proj_kernel, emit_rows=emit_rows),
        grid=(m // tm,),
        in_specs=[pl.BlockSpec((tm, D_MODEL), lambda i: (i, 0)),
                  pl.BlockSpec((1, D_MODEL), lambda i: (0, 0)),
                  pl.BlockSpec((D_MODEL, n_tot), lambda i: (0, 0))],
        out_specs=specs,
        out_shape=shapes,
        compiler_params=_params("parallel"),
        name="norm_project",
    )(x2, g, w)
    return dict(zip(names, res))


def _prep_w_in(w):
    def seg(name):
        o, n = _SEG[name]
        return w[:, o:o + n]
    gl = jnp.pad(seg("gl"), ((0, 0), (0, LANES - 3 * N_HEADS)))
    cols = [seg(n) for n, _ in _PROJ_OUT[:-1]] + [gl]
    return jnp.concatenate(cols, axis=1).astype(BF16)


def _blockdiag4(w):
    z = jnp.zeros((HEAD_DIM, HEAD_DIM), w.dtype)
    blocks = [w[0], w[0], w[1], w[1]]
    rows = [jnp.concatenate([blocks[i] if j == i else z for j in range(4)], axis=1) for i in range(4)]
    return jnp.concatenate(rows, axis=0)


def _pe_t(pe):
    one = jnp.concatenate([pe[0].T, pe[0].T, pe[1].T, pe[1].T], axis=0)
    return jnp.concatenate([one, one], axis=1)


def _compress_mean(xt, pe128, w1):
    n = xt.shape[1]
    pet = jnp.concatenate([pe128] * (n // LANES), axis=1)
    xb = (xt + pet).T.astype(BF16)
    hid = _silu(_dot(xb, w1))
    return jnp.sum(hid.reshape(n // CMP_BLOCK, CMP_BLOCK, D_KV), axis=1) * (1.0 / CMP_BLOCK)


def _compress_t(xt, pe128, w1, w2):
    return _dot(_compress_mean(xt, pe128, w1).astype(BF16), w2)


def _compress_kernel(x_ref, pe_ref, w1_ref, w2_ref, o_ref):
    o_ref[...] = _compress_t(x_ref[0], pe_ref[...], w1_ref[...], w2_ref[...])


def _compress_prompt(kc_t, wts, tile):
    batch, _, seq = kc_t.shape
    tiles = seq // tile
    const = lambda shape: pl.BlockSpec(shape, lambda b, j: (0,) * len(shape))
    return pl.pallas_call(
        _compress_kernel,
        grid=(batch, tiles),
        in_specs=[pl.BlockSpec((1, D_KV, tile), lambda b, j: (b, 0, j)),
                  const((D_KV, LANES)), const((D_KV, D_KV)), const((D_KV, D_KV))],
        out_specs=pl.BlockSpec((tile // CMP_BLOCK, D_KV), lambda b, j: (b * tiles + j, 0)),
        out_shape=jax.ShapeDtypeStruct((batch * seq // CMP_BLOCK, D_KV), F32),
        compiler_params=_params("parallel", "parallel"),
        name="compress_prompt",
    )(kc_t, wts["pe_t"], wts["w1"], wts["w2"])


def _extra_rows(pos, with_onehot):
    row = jnp.arange(LANES)[:, None]
    blk = (pos // CMP_BLOCK)[None, :]
    hi = ((pos // CMP_BLOCK) * CMP_BLOCK).astype(F32)[None, :]
    lo = (pos % CMP_BLOCK).astype(F32)[None, :]
    out = jnp.where((row == ROW_ONE_A) | (row == ROW_ONE_B), 1.0, 0.0) + jnp.zeros_like(hi)
    out = jnp.where(row == ROW_POS_HI, hi, out)
    out = jnp.where(row == ROW_POS_LO, lo, out)
    if with_onehot:
        out = jnp.where((row < CMP_BLOCK) & (row == blk), 1.0, out)
    return out.astype(BF16)


def _prompt_key_lanes(pos):
    lane = jnp.arange(LANES)[None, :]
    hi = ((pos // CMP_BLOCK) * CMP_BLOCK).astype(F32)[:, None]
    lo = (pos % CMP_BLOCK).astype(F32)[:, None]
    out = jnp.where((lane < CMP_BLOCK) & (lane == (pos // CMP_BLOCK)[:, None]), 1.0, 0.0)
    out = jnp.where((lane == P_POS_HI[0]) | (lane == P_POS_HI[1]), hi, out)
    out = jnp.where((lane == P_POS_LO[0]) | (lane == P_POS_LO[1]), lo, out)
    out = jnp.where((lane >= P_ONE[0]) & (lane <= P_ONE[-1]), 1.0, out)
    return out.astype(BF16)


def _alibi_lanes(slope, t_hi, t_lo, shape):
    lane = lax.broadcasted_iota(jnp.int32, shape, 1)
    out = jnp.where(lane == ROW_ONE_A, -slope * t_hi, 0.0)
    out = jnp.where(lane == ROW_ONE_B, -slope * t_lo, out)
    return jnp.where((lane == ROW_POS_HI) | (lane == ROW_POS_LO), slope, out)


def _masked_softmax(s, mask, axis):
    s = jnp.where(mask, s, NEG_INF)
    p = jnp.exp(s - jnp.max(s, axis=axis, keepdims=True)) * mask.astype(F32)
    return p / jnp.maximum(jnp.sum(p, axis=axis, keepdims=True), 1e-30)


def _attn_prompt_kernel(q_ref, gl_ref, kvc_ref, ksk_ref, ksv_ref, kwk_ref, kwv_ref, ext_ref, o_ref,
                        kaug_s, kaug_w, v_s, v_w, s_buf, *, seq, tq, tk):
    i = pl.program_id(1)
    m = GROUP * tq
    n_cmp = seq // CMP_BLOCK

    n_wblk = WINDOW // WIN_BLOCK + tq // WIN_BLOCK

    def value_tile(v_t):
        n = v_t.shape[1]
        ones_row = (lax.broadcasted_iota(jnp.int32, (V_ROWS - HEAD_DIM, n), 0) == 0).astype(F32)
        return jnp.concatenate([v_t, ones_row], axis=0).astype(BF16)

    @pl.when(i == 0)
    def _fill():
        for c in range(seq // tk):
            cols = slice(c * tk, (c + 1) * tk)
            kaug_s[c] = jnp.concatenate([ksk_ref[0, :, cols].T.astype(BF16), ext_ref[cols, :]], axis=1)
            for kv in range(N_KV):
                v_s[c, kv] = value_tile(ksv_ref[0, kv * HEAD_DIM:(kv + 1) * HEAD_DIM, cols])
        for c in range(seq // WIN_BLOCK):
            cols = slice(c * WIN_BLOCK, (c + 1) * WIN_BLOCK)
            kaug_w[c] = jnp.concatenate([kwk_ref[0, :, cols].T.astype(BF16), ext_ref[cols, :]], axis=1)
            for kv in range(N_KV):
                v_w[c, kv] = value_tile(kwv_ref[0, kv * HEAD_DIM:(kv + 1) * HEAD_DIM, cols])

    q0 = i * tq
    lane_t = q0 + lax.broadcasted_iota(jnp.int32, (1, m), 1) % tq
    ext_row = lax.broadcasted_iota(jnp.int32, (LANES, m), 0)
    key_row = lax.broadcasted_iota(jnp.int32, (tk, 1), 0)
    sig_t = _sigmoid(gl_ref[...]).T

    blk_row = lax.broadcasted_iota(jnp.int32, (n_cmp, m), 0)
    dist_c = lane_t - ((blk_row + 1) * CMP_BLOCK - 1)
    mask_c = dist_c >= 0
    blk2 = lax.broadcasted_iota(jnp.int32, (n_cmp, tq), 0)
    cur = (q0 + lax.broadcasted_iota(jnp.int32, (1, tq), 1)) // CMP_BLOCK
    forced = (blk2 == cur) | (blk2 == 0)
    started = blk2 <= cur

    def pieces_of(n):
        return [(a, min(a + SUB_KEYS, n)) for a in range(0, n, SUB_KEYS)]

    def qk_scores(qaugs, k_tile):
        return [[_dot(k_tile[a:b], qaug_t) for qaug_t in qaugs] for a, b in pieces_of(k_tile.shape[0])]

    def softmax_update(s, v_piece, carry, mask):
        m_i, acc = carry
        if mask is not None:
            s = jnp.where(mask, s, NEG_INF)
        m_new = jnp.maximum(m_i, jnp.max(s, axis=0, keepdims=True))
        p = jnp.exp2(s - m_new).astype(BF16)
        return m_new, jnp.exp2(m_i - m_new) * acc + _dot(v_piece, p)

    def softmax_pv(scores, v_tiles, carries, mask_fn):
        carries = list(carries)
        for (a, b), piece_scores in zip(pieces_of(v_tiles[0].shape[1]), scores):
            for kv, s in enumerate(piece_scores):
                carries[kv] = softmax_update(s, v_tiles[kv][:, a:b], carries[kv], mask_fn(a, b))
        return tuple(carries)

    def normalized(carries):
        return [acc[:HEAD_DIM] / acc[ONES_ROW:ONES_ROW + 1] for _, acc in carries]

    def sel_fill(buf, j):
        for p, piece_scores in enumerate(qk_scores(qaugs_s, kaug_s[j])):
            for kv, s in enumerate(piece_scores):
                s_buf[buf, p * N_KV + kv] = s

    def sel_tile(buf, j, carries, causal):
        carries = list(carries)
        for p, (a, b) in enumerate(pieces_of(tk)):
            mask = (j * tk + key_row[a:b] <= lane_t) if causal else None
            for kv in range(N_KV):
                carries[kv] = softmax_update(s_buf[buf, p * N_KV + kv], v_s[j, kv, :, a:b], carries[kv], mask)
        return tuple(carries)

    def split3(x):
        hi = x.astype(BF16).astype(F32)
        mid = (x - hi).astype(BF16).astype(F32)
        return hi, mid, x - hi - mid

    init = (jnp.full((1, m), NEG_INF, F32), jnp.zeros((V_ROWS, m), F32))
    n_chunk = n_cmp // 8
    q4_ts, slope_rows, ali_ts, qaugs_w = [], [], [], []
    pair_t = [q_ref[:, j * LANES:(j + 1) * LANES].T for j in range(N_HEADS // 2)]
    zero_half = jnp.zeros((HEAD_DIM, tq), F32)
    for kv in range(N_KV):
        q_heads = []
        for g in range(GROUP):
            h = kv * GROUP + g
            feats = pair_t[h // 2][(h % 2) * HEAD_DIM:(h % 2 + 1) * HEAD_DIM]
            q_heads.append(jnp.concatenate([feats, zero_half] if kv == 0 else [zero_half, feats], axis=0))
        q4_t = jnp.concatenate([q * Q_SCALE for q in q_heads], axis=1).astype(BF16)
        q4l_t = jnp.concatenate([q * (Q_SCALE * LOG2E) for q in q_heads], axis=1).astype(BF16)
        slopes = [2.0 ** -(kv * GROUP + g + 1) for g in range(GROUP)]
        slope = jnp.concatenate([jnp.full((1, tq), s, F32) for s in slopes], axis=1)
        c_hi, c_lo = slope * LOG2E_HI, slope * LOG2E_LO
        d_hi, d_mid, d_lo = split3(-(c_hi + c_lo) * lane_t.astype(F32))
        ali_t = jnp.zeros((LANES, m), F32)
        for rows, val in ((P_POS_HI, (c_hi, c_lo)), (P_POS_LO, (c_hi, c_lo)), (P_ONE, (d_hi, d_mid, d_lo))):
            for r, v in zip(rows, val):
                ali_t = jnp.where(ext_row == r, v, ali_t)
        q4_ts.append(q4_t)
        slope_rows.append(slope)
        ali_ts.append((q4l_t, ali_t))
        qaugs_w.append(jnp.concatenate([q4l_t, ali_t.astype(BF16)], axis=0))

    c0 = jnp.maximum(q0 // WIN_BLOCK - WINDOW // WIN_BLOCK, 0)
    k_win = jnp.concatenate([kaug_w[c0 + t] for t in range(n_wblk)], axis=0)
    v_wins = [jnp.concatenate([v_w[c0 + t, kv] for t in range(n_wblk)], axis=1) for kv in range(N_KV)]
    scores_w = qk_scores(qaugs_w, k_win)

    o_cs, qaugs_s = [], []
    for kv in range(N_KV):
        q4_t, slope, (q4l_t, ali_t) = q4_ts[kv], slope_rows[kv], ali_ts[kv]
        kvc = kvc_ref[...]
        s_ct = _dot(kvc[:, :LANES].astype(BF16), q4_t) - slope * dist_c.astype(F32)
        p_ct = _masked_softmax(s_ct, mask_c, 0)
        o_c = _tn_dot(kvc[:, LANES:].astype(BF16), p_ct.astype(BF16))

        imp = p_ct[:, 0:tq]
        for g in range(1, GROUP):
            imp = imp + p_ct[:, g * tq:(g + 1) * tq]
        score = jnp.where(forced, SEL_FORCED, jnp.where(started, imp, SEL_INVALID))
        chunks = [score[8 * c:8 * c + 8] for c in range(n_chunk)]
        blk8 = lax.broadcasted_iota(jnp.int32, (8, tq), 0)
        ranks = [jnp.zeros((8, tq), F32) for _ in range(n_chunk)]
        for b in range(n_cmp):
            row = score[b:b + 1, :]
            for c in range(n_chunk):
                if 8 * c > b:
                    beats = row >= chunks[c]
                elif 8 * c + 7 < b:
                    beats = row > chunks[c]
                else:
                    beats = (row > chunks[c]) | ((row == chunks[c]) & (blk8 > b - 8 * c))
                ranks[c] = ranks[c] + jnp.where(beats, 1.0, 0.0)
        rank = jnp.concatenate(ranks, axis=0)
        sel_t = ((rank < N_SELECT) & (score > SEL_INVALID * 0.5)).astype(F32)
        sel_bias = (sel_t - 1.0) * -NEG_INF
        if n_cmp < LANES:
            sel_bias = jnp.concatenate([sel_bias, jnp.zeros((LANES - n_cmp, tq), F32)], axis=0)
        sel_bias = jnp.concatenate([sel_bias] * GROUP, axis=1)

        o_cs.append(o_c)
        qaugs_s.append(
            jnp.concatenate([q4l_t, jnp.where(ext_row < CMP_BLOCK, sel_bias, ali_t).astype(BF16)], axis=0))

    n_kt = (q0 + tq + tk - 1) // tk
    sel_fill(0, 0)

    key_w = c0 * WIN_BLOCK + lax.broadcasted_iota(jnp.int32, (n_wblk * WIN_BLOCK, 1), 0)

    def win_mask(a, b):
        causal = key_w[a:b] <= lane_t
        if a >= WIN_BLOCK:
            return causal
        return causal & (lane_t - key_w[a:b] < WINDOW)
    o_ws = normalized(softmax_pv(scores_w, v_wins, (init,) * N_KV, win_mask))

    n_full = n_kt - 1

    def sel_pair(jj, carries):
        j = 2 * jj
        sel_fill(1, j + 1)
        carries = sel_tile(0, j, carries, False)
        sel_fill(0, j + 2)
        return sel_tile(1, j + 1, carries, False)
    carries = lax.fori_loop(0, n_full // 2, sel_pair, (init,) * N_KV)
    j_a = 2 * (n_full // 2)
    odd = n_full - j_a
    sel_fill(1, n_kt - 1)
    carries = sel_tile(0, j_a, carries, True)
    carries = lax.cond(odd == 1, lambda c: sel_tile(1, n_kt - 1, c, True), lambda c: c, carries)
    o_ss = normalized(carries)

    heads = []
    for kv in range(N_KV):
        rows = slice(kv * HEAD_DIM, (kv + 1) * HEAD_DIM)
        for g in range(GROUP):
            h = kv * GROUP + g
            cols = slice(g * tq, (g + 1) * tq)
            heads.append(sig_t[3 * h:3 * h + 1] * o_cs[kv][rows, cols]
                         + sig_t[3 * h + 1:3 * h + 2] * o_ss[kv][:, cols]
                         + sig_t[3 * h + 2:3 * h + 3] * o_ws[kv][:, cols])
    o_ref[...] = jnp.concatenate(heads, axis=0).T


def _attn_prompt(qw, gl, kvc, ks_t, kw_t, ext, batch, seq, tq, tk):
    nq = seq // tq
    n_cmp = seq // CMP_BLOCK
    n_kt = seq // tk
    kern = functools.partial(_attn_prompt_kernel, seq=seq, tq=tq, tk=tk)
    kv_spec = lambda half: pl.BlockSpec((1, LANES, seq), lambda b, i: (b, half, 0))
    return pl.pallas_call(
        kern,
        grid=(batch, nq),
        in_specs=[pl.BlockSpec((tq, D_NSA), lambda b, i: (b * nq + i, 0)),
                  pl.BlockSpec((tq, LANES), lambda b, i: (b * nq + i, 0)),
                  pl.BlockSpec((n_cmp, D_KV), lambda b, i: (b, 0)),
                  kv_spec(0), kv_spec(1), kv_spec(0), kv_spec(1),
                  pl.BlockSpec((seq, LANES), lambda b, i: (0, 0))],
        out_specs=pl.BlockSpec((tq, D_NSA), lambda b, i: (b * nq + i, 0)),
        out_shape=jax.ShapeDtypeStruct((batch * seq, D_NSA), F32),
        scratch_shapes=[pltpu.VMEM((n_kt, tk, 2 * LANES), BF16),
                        pltpu.VMEM((seq // WIN_BLOCK, WIN_BLOCK, 2 * LANES), BF16),
                        pltpu.VMEM((n_kt, N_KV, V_ROWS, tk), BF16),
                        pltpu.VMEM((seq // WIN_BLOCK, N_KV, V_ROWS, WIN_BLOCK), BF16),
                        pltpu.VMEM((2, N_KV * tk // SUB_KEYS, SUB_KEYS, GROUP * tq), F32)],
        compiler_params=_params("arbitrary", "arbitrary"),
        name="attn_prompt",
    )(qw, gl, kvc, ks_t, ks_t, kw_t, kw_t, ext)


def _attn_sample_kernel(pt_ref, *refs, n_pages, past_len, win_keep):
    del pt_ref
    cmp_refs = refs[:n_pages]
    page_refs = refs[n_pages:2 * n_pages]
    (q_ref, gl_ref, ksn_ref, kwn_ref, win_ref, exts_ref, extw_ref, pe_ref, w1_ref, w2_ref,
     o_ref, kvc_buf, mean_buf) = refs[2 * n_pages:]
    n_cmp = past_len // CMP_BLOCK
    per_page = PAGE_SIZE // CMP_BLOCK

    @pl.when(pl.program_id(0) == 0)
    def _first():
        kvc_buf[...] = jnp.zeros(kvc_buf.shape, F32)

    kvc = kvc_buf[...]

    todo = list(range(n_pages))

    def compress_pages(count):
        for _ in range(min(count, len(todo))):
            p = todo.pop(0)
            mean_buf[p * per_page:(p + 1) * per_page, :] = _compress_mean(
                cmp_refs[p][0, 0], pe_ref[...], w1_ref[...])
    cur = past_len // CMP_BLOCK
    t_hi = float(cur * CMP_BLOCK)
    t_lo = float(past_len % CMP_BLOCK)

    qrow = q_ref[0]
    lane1 = lax.broadcasted_iota(jnp.int32, (1, LANES), 1)
    q_rows = []
    for h in range(N_HEADS):
        pair = qrow[:, (h // 2) * LANES:(h // 2 + 1) * LANES]
        if h % 2 != h // GROUP:
            pair = pltpu.roll(pair, HEAD_DIM, axis=1)
        q_rows.append(jnp.where(lane1 // HEAD_DIM == h // GROUP, pair, 0.0))
    q8 = (jnp.concatenate(q_rows, axis=0) * Q_SCALE).astype(BF16)
    head = lax.broadcasted_iota(jnp.int32, (N_HEADS, 1), 0)
    slope = jnp.exp2(-(head + 1).astype(F32))
    lane8 = lax.broadcasted_iota(jnp.int32, (N_HEADS, LANES), 1)
    ali = _alibi_lanes(slope, t_hi, t_lo, (N_HEADS, LANES))

    kc = kvc[:, :LANES]
    if n_cmp < LANES:
        kc = jnp.concatenate([kc, jnp.zeros((LANES - n_cmp, LANES), F32)], axis=0)
    blk = lax.broadcasted_iota(jnp.int32, (1, LANES), 1)
    dist_c = past_len - ((blk + 1) * CMP_BLOCK - 1)
    mask_c = (dist_c >= 0) & (blk < n_cmp)
    s_c = _nt_dot(q8, kc.astype(BF16)) - slope * dist_c.astype(F32)
    compress_pages(2)
    p_c = _masked_softmax(s_c, mask_c, 1)
    o_c = _dot(p_c[:, :n_cmp].astype(BF16), kvc[:, LANES:].astype(BF16))

    forced = (blk == cur) | (blk == 0)
    started = blk <= cur
    below = (lax.broadcasted_iota(jnp.int32, (LANES, LANES), 0)
             < lax.broadcasted_iota(jnp.int32, (LANES, LANES), 1))
    sel_rows = []
    for kv in range(N_KV):
        imp = jnp.sum(p_c[kv * GROUP:(kv + 1) * GROUP], axis=0, keepdims=True)
        score = jnp.where(forced, SEL_FORCED, jnp.where(started, imp, SEL_INVALID))
        col = _row_to_col(score)
        compress_pages(1)
        beats = (col > score) | ((col == score) & below)
        rank = jnp.sum(beats.astype(F32), axis=0, keepdims=True)
        sel = ((rank < N_SELECT) & (score > SEL_INVALID * 0.5)).astype(F32)
        sel_rows += [sel] * GROUP
    sel_bias = (jnp.concatenate(sel_rows, axis=0) - 1.0) * -NEG_INF
    qaug_s = jnp.concatenate([q8, jnp.where(lane8 < CMP_BLOCK, sel_bias, ali).astype(BF16)], axis=1)
    qaug_w = jnp.concatenate([q8, ali.astype(BF16)], axis=1)

    def new_key_score(row_ref):
        k_new = row_ref[0][:, :LANES].astype(BF16).astype(F32)
        return jnp.sum(q8.astype(F32) * k_new, axis=1, keepdims=True)

    def finish(scores, values, s_new, v_new):
        m = s_new
        for s in scores:
            m = jnp.maximum(m, jnp.max(s, axis=1, keepdims=True))
        p_new = jnp.exp(s_new - m)
        l = p_new
        acc = p_new * v_new.astype(BF16).astype(F32)
        for i, (s, v) in enumerate(zip(scores, values)):
            p = jnp.exp(s - m)
            l = l + jnp.sum(p, axis=1, keepdims=True)
            acc = acc + _nt_dot(p.astype(BF16), v)
            if i % 4 == 3:
                compress_pages(1)
        return acc / l

    scores, values = [], []
    for p, ref in enumerate(page_refs):
        if p % 2 == 0:
            compress_pages(1)
        page = ref[0, 0]
        kaug = jnp.concatenate([page[:LANES].astype(BF16), exts_ref[p]], axis=0)
        scores.append(_dot(qaug_s, kaug))
        values.append(page[LANES:].astype(BF16))
    o_s = finish(scores, values, new_key_score(ksn_ref), ksn_ref[0][:, LANES:])

    win = win_ref[0, 0]
    kaug = jnp.concatenate([win[:LANES].astype(BF16), extw_ref[...]], axis=0)
    s_w = _dot(qaug_w, kaug)
    dist_w = win_keep - lax.broadcasted_iota(jnp.int32, (1, win_keep), 1)
    s_w = jnp.where((dist_w >= 0) & (dist_w < WINDOW), s_w, NEG_INF)
    o_w = finish([s_w], [win[LANES:].astype(BF16)], new_key_score(kwn_ref), kwn_ref[0][:, LANES:])

    sig = _sigmoid(gl_ref[0])
    gate = [jnp.sum(jnp.where(lane8 == 3 * head + j, sig, 0.0), axis=1, keepdims=True) for j in range(3)]
    out = gate[0] * o_c + gate[1] * o_s + gate[2] * o_w
    out = jnp.where(head < GROUP, out, pltpu.roll(out, HEAD_DIM, axis=1))
    o_ref[0] = out[:, :HEAD_DIM]
    compress_pages(len(todo))
    kvc_buf[...] = _dot(mean_buf[...].astype(BF16), w2_ref[...])


def _attn_sample(cmp_t, sel_t, win_t, pt_flat, layer, pr, wts, ext_s, ext_w, db, n_pages, win_keep):
    past_len = n_pages * PAGE_SIZE
    n_cmp = past_len // CMP_BLOCK
    attn_row = lambda s: jnp.maximum(s - 1, 0)
    cmp_row = lambda s: jnp.minimum(s, db - 1)
    page = lambda row_of, p: pl.BlockSpec(
        (1, 1, D_KV, PAGE_SIZE), lambda s, pt: (layer, pt[row_of(s) * n_pages + p], 0, 0))
    row = lambda width: pl.BlockSpec((1, 1, width), lambda s, pt: (attn_row(s), 0, 0))
    const = lambda shape: pl.BlockSpec(shape, lambda s, pt: (0,) * len(shape))
    grid_spec = pltpu.PrefetchScalarGridSpec(
        num_scalar_prefetch=1,
        grid=(db + 1,),
        in_specs=[page(cmp_row, p) for p in range(n_pages)] + [page(attn_row, p) for p in range(n_pages)]
        + [row(D_NSA), row(LANES), row(D_KV), row(D_KV),
           pl.BlockSpec((1, 1, D_KV, win_keep), lambda s, pt: (layer, attn_row(s), 0, 0)),
           const((n_pages, LANES, PAGE_SIZE)), const((LANES, win_keep)),
           const((D_KV, LANES)), const((D_KV, D_KV)), const((D_KV, D_KV))],
        out_specs=pl.BlockSpec((1, N_HEADS, HEAD_DIM), lambda s, pt: (attn_row(s), 0, 0)),
        scratch_shapes=[pltpu.VMEM((n_cmp, D_KV), F32), pltpu.VMEM((n_cmp, D_KV), F32)],
    )
    kern = functools.partial(_attn_sample_kernel, n_pages=n_pages, past_len=past_len, win_keep=win_keep)
    out = pl.pallas_call(
        kern,
        grid_spec=grid_spec,
        out_shape=jax.ShapeDtypeStruct((db, N_HEADS, HEAD_DIM), F32),
        compiler_params=_params("arbitrary"),
        name="attn_sample",
    )(pt_flat, *([cmp_t] * n_pages), *([sel_t] * n_pages), pr["q"].reshape(db, 1, D_NSA),
      pr["gl"].reshape(db, 1, LANES), pr["ks"].reshape(db, 1, D_KV), pr["kw"].reshape(db, 1, D_KV),
      win_t, ext_s, ext_w, wts["pe_t"], wts["w1"], wts["w2"])
    return out.reshape(db, D_NSA)


def _win_update_kernel(win_ref, new_ref, o_ref, *, bb, win_keep):
    lane = lax.broadcasted_iota(jnp.int32, (D_KV, win_keep), 1)
    for k in range(bb):
        row = new_ref[0, k]
        col = jnp.concatenate([_row_to_col(row[:, h * LANES:(h + 1) * LANES]) for h in range(D_KV // LANES)],
                              axis=0)
        shifted = pltpu.roll(win_ref[0, k], win_keep - 1, axis=1)
        o_ref[0, k] = jnp.where(lane == win_keep - 1, col, shifted)


def _win_update(win_t, kw_new, bb):
    depth, db, _, win_keep = win_t.shape
    return pl.pallas_call(
        functools.partial(_win_update_kernel, bb=bb, win_keep=win_keep),
        grid=(depth, db // bb),
        in_specs=[pl.BlockSpec((1, bb, D_KV, win_keep), lambda l, b: (l, b, 0, 0)),
                  pl.BlockSpec((1, bb, 1, D_KV), lambda l, b: (l, b, 0, 0))],
        out_specs=pl.BlockSpec((1, bb, D_KV, win_keep), lambda l, b: (l, b, 0, 0)),
        out_shape=jax.ShapeDtypeStruct(win_t.shape, F32),
        compiler_params=_params("parallel", "parallel"),
        name="win_update",
    )(win_t, kw_new)


def _pool_window_lane(shape):
    lane = lax.broadcasted_iota(jnp.int32, shape, len(shape) - 1)
    w = jnp.full(shape, POOL_WINDOWS[0], jnp.int32)
    for gi in range(1, len(POOL_WINDOWS)):
        w = jnp.where(lane >= gi * POOL_GROUP_DIM, POOL_WINDOWS[gi], w)
    return w


def _group_rms(v, ones_bd):
    sq = v * v
    hi = sq.astype(BF16)
    lo = (sq - hi.astype(F32)).astype(BF16)
    return (_dot(hi, ones_bd) + _dot(lo, ones_bd)) * (1.0 / GMLP_GROUP_DIM)


def _mix_out(x, ya, za, yb, zb, yc, zc, wo_ref, gpost):
    out = _dot((ya * _silu(za)).astype(BF16), wo_ref[0:D_NSA, :])
    out += _dot((yb * _silu(zb)).astype(BF16), wo_ref[D_NSA:D_NSA + D_POOL, :])
    out += _dot((yc * _silu(zc)).astype(BF16), wo_ref[D_NSA + D_POOL:, :])
    ms = jnp.mean(out * out, axis=-1, keepdims=True)
    return x + out * lax.rsqrt(ms + EPS) * gpost


def _merge_prompt_kernel(x_ref, ya_ref, za_ref, pin_ref, prev_ref, zb_ref, u_ref, v_ref, zc_ref,
                         pw_ref, ps_ref, gn_ref, ones_ref, ws_ref, bs_ref, wo_ref, gp_ref,
                         xo_ref, vn_ref, buf_a, buf_b, *, tm, tiles_per_seq):
    i = pl.program_id(0)
    first = (i % tiles_per_seq) == 0
    pin = pin_ref[...]
    lo, n = POOL_MAX, tm + POOL_MAX
    buf_a[0:lo, :] = jnp.zeros((lo, D_POOL), F32)
    buf_b[0:lo, :] = jnp.zeros((lo, D_POOL), F32)
    buf_a[lo:2 * lo, :] = jnp.where(first, 0.0, prev_ref[...])
    buf_a[2 * lo:, :] = pin
    s2 = buf_a[pl.ds(lo, n), :] + buf_a[pl.ds(lo - 1, n), :]
    buf_b[lo:, :] = s2
    s4 = s2 + buf_b[pl.ds(lo - 2, n), :]
    buf_a[lo:, :] = s4
    s8 = s4 + buf_a[pl.ds(lo - 4, n), :]
    buf_b[lo:, :] = s8
    s16 = s8[lo:] + buf_b[pl.ds(2 * lo - 8, tm), :]
    w_lane = _pool_window_lane((tm, D_POOL))
    acc = jnp.where(w_lane == POOL_WINDOWS[0], s2[lo:],
                    jnp.where(w_lane == POOL_WINDOWS[1], s4[lo:], jnp.where(w_lane == POOL_WINDOWS[2], s8[lo:], s16)))
    pos = (i % tiles_per_seq) * tm + lax.broadcasted_iota(jnp.int32, (tm, D_POOL), 0)
    cnt = jnp.minimum(w_lane, pos + 1).astype(F32)
    diff = acc / cnt - pin
    yb = _dot(diff.astype(BF16), pw_ref[...]) * ps_ref[...]

    v = v_ref[...]
    vn = v * lax.rsqrt(_group_rms(v, ones_ref[...]) + EPS) * gn_ref[...]
    vn_ref[...] = vn
    vnb = vn.astype(BF16)
    lane = lax.broadcasted_iota(jnp.int32, (CHUNK, D_GMLP), 1)
    chunks = []
    for c in range(tm // CHUNK):
        vc = vnb[c * CHUNK:(c + 1) * CHUNK]
        s = _dot(ws_ref[0], vc)
        for g in range(1, D_GMLP // GMLP_GROUP_DIM):
            s = jnp.where(lane >= g * GMLP_GROUP_DIM, _dot(ws_ref[g], vc), s)
        chunks.append(s + bs_ref[...])
    yc = u_ref[...] * jnp.concatenate(chunks, axis=0)

    xo_ref[...] = _mix_out(x_ref[...], ya_ref[...], za_ref[...], yb, zb_ref[...], yc, zc_ref[...],
                           wo_ref, gp_ref[...])


def _merge_prompt(x2, ya, pr, wts, seq, tm):
    m = x2.shape[0]
    tiles_per_seq = seq // tm
    per = tm // POOL_MAX
    rows = lambda width: pl.BlockSpec((tm, width), lambda i: (i, 0))
    const = lambda shape: pl.BlockSpec(shape, lambda i: (0,) * len(shape))
    kern = functools.partial(_merge_prompt_kernel, tm=tm, tiles_per_seq=tiles_per_seq)
    return pl.pallas_call(
        kern,
        grid=(m // tm,),
        in_specs=[rows(D_MODEL), rows(D_NSA), rows(D_NSA), rows(D_POOL),
                  pl.BlockSpec((POOL_MAX, D_POOL), lambda i: (jnp.maximum(i * per - 1, 0), 0)),
                  rows(D_POOL), rows(D_GMLP), rows(D_GMLP), rows(D_GMLP),
                  const((D_POOL, D_POOL)), const((1, D_POOL)), const((1, D_GMLP)),
                  const((D_GMLP, D_GMLP)), const((4, CHUNK, CHUNK)), const((CHUNK, D_GMLP)),
                  const((D_MODEL, D_MODEL)), const((1, D_MODEL))],
        out_specs=[rows(D_MODEL), rows(D_GMLP)],
        out_shape=[jax.ShapeDtypeStruct((m, D_MODEL), F32), jax.ShapeDtypeStruct((m, D_GMLP), F32)],
        scratch_shapes=[pltpu.VMEM((tm + 2 * POOL_MAX, D_POOL), F32), pltpu.VMEM((tm + 2 * POOL_MAX, D_POOL), F32)],
        compiler_params=_params("parallel"),
        name="merge_prompt",
    )(x2, ya, pr["za"], pr["pin"], pr["pin"], pr["zb"], pr["u"], pr["v"], pr["zc"],
      wts["pool_w"], wts["pool_scale"], wts["gmlp_norm"], wts["ones_bd"], wts["ws_tril"], wts["bs_full"],
      wts["w_out"], wts["norm_post"])


def _merge_sample_kernel(x_ref, ya_ref, za_ref, pin_ref, st_ref, zb_ref, u_ref, v_ref, zc_ref,
                         pw_ref, ps_ref, gn_ref, ones_ref, w0_ref, b0_ref, wo_ref, gp_ref,
                         xo_ref, vn_ref):
    pin = pin_ref[...]
    w_lane = _pool_window_lane(pin.shape)
    acc = pin
    for k in range(1, POOL_MAX):
        acc = acc + jnp.where(w_lane > k, st_ref[POOL_MAX - 1 - k], 0.0)
    diff = acc / w_lane.astype(F32) - pin
    yb = _dot(diff.astype(BF16), pw_ref[...]) * ps_ref[...]

    v = v_ref[...]
    vn = v * lax.rsqrt(_group_rms(v, ones_ref[...]) + EPS) * gn_ref[...]
    vn_ref[...] = vn
    yc = u_ref[...] * (w0_ref[...] * vn + b0_ref[...])

    xo_ref[...] = _mix_out(x_ref[...], ya_ref[...], za_ref[...], yb, zb_ref[...], yc, zc_ref[...],
                           wo_ref, gp_ref[...])


def _merge_sample(x2, ya, pr, state_t, wts):
    db = x2.shape[0]
    full = lambda shape: pl.BlockSpec(shape, lambda i: (0,) * len(shape))
    args = (x2, ya, pr["za"], pr["pin"], state_t, pr["zb"], pr["u"], pr["v"], pr["zc"],
            wts["pool_w"], wts["pool_scale"], wts["gmlp_norm"], wts["ones_bd"], wts["w0"], wts["b0"],
            wts["w_out"], wts["norm_post"])
    return pl.pallas_call(
        _merge_sample_kernel,
        grid=(1,),
        in_specs=[full(a.shape) for a in args],
        out_specs=[full((db, D_MODEL)), full((db, D_GMLP))],
        out_shape=[jax.ShapeDtypeStruct((db, D_MODEL), F32), jax.ShapeDtypeStruct((db, D_GMLP), F32)],
        compiler_params=_params("arbitrary"),
        name="merge_sample",
    )(*args)


def _layer_weights(l, norm_pre, w_in, cmp_pe, cmp_w1, cmp_w2, pool_w, pool_scale, gmlp_norm, gmlp_ws, gmlp_bs,
                   w_out, norm_post):
    n_g = D_GMLP // GMLP_GROUP_DIM
    ones_bd = jnp.kron(jnp.eye(n_g, dtype=F32), jnp.ones((GMLP_GROUP_DIM, GMLP_GROUP_DIM), F32)).astype(BF16)
    pw = pool_w[l]
    zp = jnp.zeros_like(pw[0])
    pool_bd = jnp.concatenate(
        [jnp.concatenate([pw[i] if j == i else zp for j in range(4)], axis=1) for i in range(4)], axis=0)
    tril = jnp.tril(jnp.ones((CHUNK, CHUNK), F32))
    return dict(
        norm_pre=norm_pre[l].reshape(1, D_MODEL),
        w_in=_prep_w_in(w_in[l]),
        pe_t=_pe_t(cmp_pe[l]),
        w1=_blockdiag4(cmp_w1[l]).astype(BF16),
        w2=_blockdiag4(cmp_w2[l]).astype(BF16),
        pool_w=pool_bd.astype(BF16),
        pool_scale=pool_scale[l].reshape(1, D_POOL),
        gmlp_norm=gmlp_norm[l].reshape(1, D_GMLP),
        ones_bd=ones_bd,
        ws_tril=(gmlp_ws[l] * tril).astype(BF16),
        bs_full=jnp.repeat(gmlp_bs[l].T, GMLP_GROUP_DIM, axis=1),
        w0=jnp.repeat(gmlp_ws[l][:, 0, 0], GMLP_GROUP_DIM).reshape(1, D_GMLP),
        b0=jnp.repeat(gmlp_bs[l][:, 0], GMLP_GROUP_DIM).reshape(1, D_GMLP),
        w_out=w_out[l].astype(BF16),
        norm_post=norm_post[l].reshape(1, D_MODEL),
    )


def _largest_tile(n, cap):
    t = cap
    while n % t:
        t //= 2
    return t


def _feature_major(cache):
    d, n, t = cache.shape[:3]
    return jnp.transpose(cache, (0, 1, 3, 4, 5, 2)).reshape(d, n, D_KV, t)


def _token_major(x_t):
    d, n, _, t = x_t.shape
    return jnp.transpose(x_t.reshape(d, n, 2, N_KV, HEAD_DIM, t), (0, 1, 5, 2, 3, 4))


def kernel(x_prompt, x_sample, cache_kv_cmp, cache_kv_sel, cache_kv_win, state_pool, page_table, norm_pre, w_in,
           cmp_pe, cmp_w1, cmp_w2, pool_w, pool_scale, gmlp_norm, gmlp_ws, gmlp_bs, w_out, norm_post):
    bp, tp, _ = x_prompt.shape
    db, ts, _ = x_sample.shape
    depth = w_in.shape[0]
    n_pages = page_table.shape[1]
    past_len = n_pages * PAGE_SIZE
    win_keep = cache_kv_win.shape[2]
    assert ts == 1 and tp % CHUNK == 0 and tp <= CMP_BLOCK * CMP_BLOCK and tp >= WINDOW
    assert past_len // CMP_BLOCK + 1 <= CMP_BLOCK and win_keep == WINDOW and db % LANES == 0

    mp = bp * tp
    tm_proj = _largest_tile(tp, 512)
    tile_cmp = _largest_tile(tp, 1024)
    tm_merge = _largest_tile(tp, 512)
    tq, tk = 128, _largest_tile(tp, 512)
    assert tp >= WINDOW + tq

    cmp_t = _feature_major(cache_kv_cmp)
    sel_t = _feature_major(cache_kv_sel)
    win_t = _feature_major(cache_kv_win)
    pt_flat = page_table.reshape(-1).astype(jnp.int32)

    ext_prompt = _prompt_key_lanes(jnp.arange(tp, dtype=jnp.int32))
    ext_s = jnp.transpose(
        _extra_rows(jnp.arange(past_len, dtype=jnp.int32), True).reshape(LANES, n_pages, PAGE_SIZE), (1, 0, 2))
    ext_w = _extra_rows(past_len - win_keep + jnp.arange(win_keep, dtype=jnp.int32), False)

    xp = x_prompt.reshape(mp, D_MODEL)
    xs = x_sample.reshape(db, D_MODEL)
    names = ("kvc_p", "kvc_s", "kvs_p", "kvs_s", "kvw_p", "kw_new", "pool_p", "pool_s", "gv_p", "gv_s")
    outs = {k: [] for k in names}
    for l in range(depth):
        wts = _layer_weights(l, norm_pre, w_in, cmp_pe, cmp_w1, cmp_w2, pool_w, pool_scale, gmlp_norm, gmlp_ws,
                             gmlp_bs, w_out, norm_post)
        pr = _project(xp, wts["norm_pre"], wts["w_in"], bp, tp, tm_proj, False)
        kvc = _compress_prompt(pr["kc_t"], wts, tile_cmp)
        ya = _attn_prompt(pr["q"], pr["gl"], kvc, pr["ks_t"], pr["kw_t"], ext_prompt, bp, tp, tq, tk)
        xp, vn = _merge_prompt(xp, ya, pr, wts, tp, tm_merge)
        outs["kvc_p"].append(pr["kc_t"])
        outs["kvs_p"].append(pr["ks_t"])
        outs["kvw_p"].append(pr["kw_t"][:, :, tp - WINDOW:])
        outs["pool_p"].append(pr["pin"].reshape(bp, tp, D_POOL)[:, tp - (POOL_MAX - 1):])
        outs["gv_p"].append(vn.reshape(bp, tp, D_GMLP)[:, tp - CHUNK:])
        pr = _project(xs, wts["norm_pre"], wts["w_in"], 1, db, db, True)
        ya = _attn_sample(cmp_t, sel_t, win_t, pt_flat, l, pr, wts, ext_s, ext_w, db, n_pages, win_keep)
        state_t = jnp.swapaxes(state_pool[l], 0, 1)
        xs, vn = _merge_sample(xs, ya, pr, state_t, wts)
        outs["kvc_s"].append(pr["kc_t"])
        outs["kvs_s"].append(pr["ks_t"])
        outs["kw_new"].append(pr["kw"].reshape(db, 1, D_KV))
        outs["pool_s"].append(jnp.concatenate([state_t[1:], pr["pin"][None]], axis=0))
        outs["gv_s"].append(vn.reshape(db, 1, D_GMLP))
    st = {k: jnp.stack(v) for k, v in outs.items()}
    kvw_s = _win_update(win_t, st["kw_new"], _largest_tile(db, 8))
    sample_kv = lambda x_t: jnp.transpose(_token_major(x_t), (0, 2, 1, 3, 4, 5))
    return (xp.reshape(bp, tp, D_MODEL), xs.reshape(db, ts, D_MODEL),
            _token_major(st["kvc_p"]), sample_kv(st["kvc_s"]),
            _token_major(st["kvs_p"]), sample_kv(st["kvs_s"]),
            _token_major(st["kvw_p"]), _token_major(kvw_s),
            st["pool_p"], jnp.swapaxes(st["pool_s"], 1, 2), st["gv_p"], st["gv_s"])
```

```python
import functools

import jax
import jax.numpy as jnp
import ml_dtypes
import numpy as np
from jax import lax
from jax.experimental import pallas as pl
from jax.experimental.pallas import tpu as pltpu

F32 = jnp.float32
BF16 = jnp.bfloat16

D_MODEL = 1024
HEAD_DIM = 64
N_HEADS = 8
N_KV = 2
GROUP = N_HEADS // N_KV
D_NSA = N_HEADS * HEAD_DIM
D_KV = 2 * N_KV * HEAD_DIM
CMP_BLOCK = 64
N_SELECT = 16
WINDOW = 512
D_POOL = 256
POOL_WINDOWS = (2, 4, 8, 16)
POOL_GROUP_DIM = 64
POOL_MAX = 16
D_GMLP = 256
GMLP_GROUP_DIM = 64
CHUNK = 128
PAGE_SIZE = 128
EPS = 1e-6
NEG_INF = -1e30
SEL_FORCED = 1e4
SEL_INVALID = -1e4
Q_SCALE = HEAD_DIM ** -0.5

WIN_BLOCK = 128
SUB_KEYS = 256
LANES = 128
VMEM_LIMIT = 56 * 1024 * 1024

ROW_ONE_A = 64
ROW_ONE_B = 65
ROW_POS_HI = 66
ROW_POS_LO = 67

LOG2E = float(np.log2(np.e))
LOG2E_HI = float(np.asarray(LOG2E, ml_dtypes.bfloat16))
LOG2E_LO = float(np.asarray(LOG2E - LOG2E_HI, ml_dtypes.bfloat16))
P_POS_HI = (64, 65)
P_POS_LO = (66, 67)
P_ONE = (68, 69, 70)
ONES_ROW = HEAD_DIM
V_ROWS = HEAD_DIM + 16

_SEG = dict(q=(0, 512), kc=(512, 256), ks=(768, 256), kw=(1024, 256), gl=(1280, 24), za=(1304, 512),
            pin=(1816, 256), zb=(2072, 256), u=(2328, 256), v=(2584, 256), zc=(2840, 256))
_PROJ_OUT = (("q", D_NSA), ("kc", 256), ("ks", 256), ("kw", 256), ("za", 512), ("pin", 256),
             ("zb", 256), ("u", 256), ("v", 256), ("zc", 256), ("gl", LANES))
_KV_NAMES = ("kc", "ks", "kw")


def _nt_dot(a, b):
    return lax.dot_general(a, b, (((1,), (1,)), ((), ())), preferred_element_type=F32)


def _tn_dot(a, b):
    return lax.dot_general(a, b, (((0,), (0,)), ((), ())), preferred_element_type=F32)


def _dot(a, b):
    return jnp.dot(a, b, preferred_element_type=F32)


def _sigmoid(x):
    return 0.5 * jnp.tanh(0.5 * x) + 0.5


def _silu(x):
    return x * _sigmoid(x)


def _params(*sem):
    return pltpu.CompilerParams(dimension_semantics=sem, vmem_limit_bytes=VMEM_LIMIT)


def _row_to_col(row):
    eye = (lax.broadcasted_iota(jnp.int32, (LANES, LANES), 0)
           == lax.broadcasted_iota(jnp.int32, (LANES, LANES), 1))
    return jnp.sum(jnp.where(eye, row, 0.0), axis=1, keepdims=True)


def _proj_kernel(x_ref, g_ref, w_ref, *out_refs, emit_rows):
    x = x_ref[...]
    ms = jnp.mean(x * x, axis=-1, keepdims=True)
    xn = (x * lax.rsqrt(ms + EPS) * g_ref[...]).astype(BF16)
    refs = iter(out_refs)
    off = 0
    for name, width in _PROJ_OUT:
        res = _dot(xn, w_ref[:, off:off + width])
        off += width
        if name in _KV_NAMES:
            next(refs)[0] = res.T
            if not emit_rows:
                continue
        next(refs)[...] = res


def _project(x2, g, w, batch, seq, tm, emit_rows):
    m = x2.shape[0]
    n_tot = w.shape[1]
    tiles = seq // tm
    names, shapes, specs = [], [], []
    for name, width in _PROJ_OUT:
        if name in _KV_NAMES:
            names.append(name + "_t")
            shapes.append(jax.ShapeDtypeStruct((batch, width, seq), F32))
            specs.append(pl.BlockSpec((1, width, tm), lambda i: (i // tiles, 0, i % tiles)))
            if not emit_rows:
                continue
        names.append(name)
        shapes.append(jax.ShapeDtypeStruct((m, width), F32))
        specs.append(pl.BlockSpec((tm, width), lambda i: (i, 0)))
    res = pl.pallas_call(
        functools.partial(_proj_kernel, emit_rows=emit_rows),
        grid=(m // tm,),
        in_specs=[pl.BlockSpec((tm, D_MODEL), lambda i: (i, 0)),
                  pl.BlockSpec((1, D_MODEL), lambda i: (0, 0)),
                  pl.BlockSpec((D_MODEL, n_tot), lambda i: (0, 0))],
        out_specs=specs,
        out_shape=shapes,
        compiler_params=_params("parallel"),
        name="norm_project",
    )(x2, g, w)
    return dict(zip(names, res))


def _prep_w_in(w):
    def seg(name):
        o, n = _SEG[name]
        return w[:, o:o + n]
    gl = jnp.pad(seg("gl"), ((0, 0), (0, LANES - 3 * N_HEADS)))
    cols = [seg(n) for n, _ in _PROJ_OUT[:-1]] + [gl]
    return jnp.concatenate(cols, axis=1).astype(BF16)


def _blockdiag4(w):
    z = jnp.zeros((HEAD_DIM, HEAD_DIM), w.dtype)
    blocks = [w[0], w[0], w[1], w[1]]
    rows = [jnp.concatenate([blocks[i] if j == i else z for j in range(4)], axis=1) for i in range(4)]
    return jnp.concatenate(rows, axis=0)


def _pe_t(pe):
    one = jnp.concatenate([pe[0].T, pe[0].T, pe[1].T, pe[1].T], axis=0)
    return jnp.concatenate([one, one], axis=1)


def _compress_mean(xt, pe128, w1):
    n = xt.shape[1]
    pet = jnp.concatenate([pe128] * (n // LANES), axis=1)
    xb = (xt + pet).T.astype(BF16)
    hid = _silu(_dot(xb, w1))
    return jnp.sum(hid.reshape(n // CMP_BLOCK, CMP_BLOCK, D_KV), axis=1) * (1.0 / CMP_BLOCK)


def _compress_t(xt, pe128, w1, w2):
    return _dot(_compress_mean(xt, pe128, w1).astype(BF16), w2)


def _compress_kernel(x_ref, pe_ref, w1_ref, w2_ref, o_ref):
    o_ref[...] = _compress_t(x_ref[0], pe_ref[...], w1_ref[...], w2_ref[...])


def _compress_prompt(kc_t, wts, tile):
    batch, _, seq = kc_t.shape
    tiles = seq // tile
    const = lambda shape: pl.BlockSpec(shape, lambda b, j: (0,) * len(shape))
    return pl.pallas_call(
        _compress_kernel,
        grid=(batch, tiles),
        in_specs=[pl.BlockSpec((1, D_KV, tile), lambda b, j: (b, 0, j)),
                  const((D_KV, LANES)), const((D_KV, D_KV)), const((D_KV, D_KV))],
        out_specs=pl.BlockSpec((tile // CMP_BLOCK, D_KV), lambda b, j: (b * tiles + j, 0)),
        out_shape=jax.ShapeDtypeStruct((batch * seq // CMP_BLOCK, D_KV), F32),
        compiler_params=_params("parallel", "parallel"),
        name="compress_prompt",
    )(kc_t, wts["pe_t"], wts["w1"], wts["w2"])


def _extra_rows(pos, with_onehot):
    row = np.arange(LANES)[:, None]
    blk = (pos // CMP_BLOCK)[None, :]
    hi = ((pos // CMP_BLOCK) * CMP_BLOCK).astype(np.float32)[None, :]
    lo = (pos % CMP_BLOCK).astype(np.float32)[None, :]
    out = np.where((row == ROW_ONE_A) | (row == ROW_ONE_B), 1.0, 0.0) + np.zeros_like(hi)
    out = np.where(row == ROW_POS_HI, hi, out)
    out = np.where(row == ROW_POS_LO, lo, out)
    if with_onehot:
        out = np.where((row < CMP_BLOCK) & (row == blk), 1.0, out)
    return out.astype(ml_dtypes.bfloat16)


def _prompt_key_lanes(pos):
    lane = np.arange(LANES)[None, :]
    hi = ((pos // CMP_BLOCK) * CMP_BLOCK).astype(np.float32)[:, None]
    lo = (pos % CMP_BLOCK).astype(np.float32)[:, None]
    out = np.where((lane < CMP_BLOCK) & (lane == (pos // CMP_BLOCK)[:, None]), 1.0, 0.0)
    out = np.where((lane == P_POS_HI[0]) | (lane == P_POS_HI[1]), hi, out)
    out = np.where((lane == P_POS_LO[0]) | (lane == P_POS_LO[1]), lo, out)
    out = np.where((lane >= P_ONE[0]) & (lane <= P_ONE[-1]), 1.0, out)
    return out.astype(ml_dtypes.bfloat16)


def _alibi_lanes(slope, t_hi, t_lo, shape):
    lane = lax.broadcasted_iota(jnp.int32, shape, 1)
    out = jnp.where(lane == ROW_ONE_A, -slope * t_hi, 0.0)
    out = jnp.where(lane == ROW_ONE_B, -slope * t_lo, out)
    return jnp.where((lane == ROW_POS_HI) | (lane == ROW_POS_LO), slope, out)


def _masked_softmax(s, mask, axis):
    s = jnp.where(mask, s, NEG_INF)
    p = jnp.exp(s - jnp.max(s, axis=axis, keepdims=True)) * mask.astype(F32)
    return p / jnp.maximum(jnp.sum(p, axis=axis, keepdims=True), 1e-30)


def _attn_prompt_kernel(q_ref, gl_ref, kvc_ref, ksk_ref, ksv_ref, kwk_ref, kwv_ref, ext_ref, o_ref,
                        kaug_s, kaug_w, v_s, v_w, s_buf, *, seq, tq, tk):
    i = pl.program_id(1)
    m = GROUP * tq
    n_cmp = seq // CMP_BLOCK

    n_wblk = WINDOW // WIN_BLOCK + tq // WIN_BLOCK

    def value_tile(v_t):
        n = v_t.shape[1]
        ones_row = (lax.broadcasted_iota(jnp.int32, (V_ROWS - HEAD_DIM, n), 0) == 0).astype(F32)
        return jnp.concatenate([v_t, ones_row], axis=0).astype(BF16)

    @pl.when(i == 0)
    def _fill():
        for c in range(seq // tk):
            cols = slice(c * tk, (c + 1) * tk)
            kaug_s[c] = jnp.concatenate([ksk_ref[0, :, cols].T.astype(BF16), ext_ref[cols, :]], axis=1)
            for kv in range(N_KV):
                v_s[c, kv] = value_tile(ksv_ref[0, kv * HEAD_DIM:(kv + 1) * HEAD_DIM, cols])
        for c in range(seq // WIN_BLOCK):
            cols = slice(c * WIN_BLOCK, (c + 1) * WIN_BLOCK)
            kaug_w[c] = jnp.concatenate([kwk_ref[0, :, cols].T.astype(BF16), ext_ref[cols, :]], axis=1)
            for kv in range(N_KV):
                v_w[c, kv] = value_tile(kwv_ref[0, kv * HEAD_DIM:(kv + 1) * HEAD_DIM, cols])

    q0 = i * tq
    lane_t = q0 + lax.broadcasted_iota(jnp.int32, (1, m), 1) % tq
    ext_row = lax.broadcasted_iota(jnp.int32, (LANES, m), 0)
    key_row = lax.broadcasted_iota(jnp.int32, (tk, 1), 0)
    sig_t = _sigmoid(gl_ref[...]).T

    blk_row = lax.broadcasted_iota(jnp.int32, (n_cmp, m), 0)
    dist_c = lane_t - ((blk_row + 1) * CMP_BLOCK - 1)
    mask_c = dist_c >= 0
    blk2 = lax.broadcasted_iota(jnp.int32, (n_cmp, tq), 0)
    cur = (q0 + lax.broadcasted_iota(jnp.int32, (1, tq), 1)) // CMP_BLOCK
    forced = (blk2 == cur) | (blk2 == 0)
    started = blk2 <= cur

    def pieces_of(n):
        return [(a, min(a + SUB_KEYS, n)) for a in range(0, n, SUB_KEYS)]

    def qk_scores(qaugs, k_tile):
        return [[_dot(k_tile[a:b], qaug_t) for qaug_t in qaugs] for a, b in pieces_of(k_tile.shape[0])]

    def softmax_update(s, v_piece, carry, mask):
        m_i, acc = carry
        if mask is not None:
            s = jnp.where(mask, s, NEG_INF)
        m_new = jnp.maximum(m_i, jnp.max(s, axis=0, keepdims=True))
        p = jnp.exp2(s - m_new).astype(BF16)
        return m_new, jnp.exp2(m_i - m_new) * acc + _dot(v_piece, p)

    def softmax_pv(scores, v_tiles, carries, mask_fn):
        carries = list(carries)
        for (a, b), piece_scores in zip(pieces_of(v_tiles[0].shape[1]), scores):
            for kv, s in enumerate(piece_scores):
                carries[kv] = softmax_update(s, v_tiles[kv][:, a:b], carries[kv], mask_fn(a, b))
        return tuple(carries)

    def normalized(carries):
        return [acc[:HEAD_DIM] / acc[ONES_ROW:ONES_ROW + 1] for _, acc in carries]

    def sel_fill(buf, j):
        for p, piece_scores in enumerate(qk_scores(qaugs_s, kaug_s[j])):
            for kv, s in enumerate(piece_scores):
                s_buf[buf, p * N_KV + kv] = s

    def sel_tile(buf, j, carries, causal):
        carries = list(carries)
        for p, (a, b) in enumerate(pieces_of(tk)):
            mask = (j * tk + key_row[a:b] <= lane_t) if causal else None
            for kv in range(N_KV):
                carries[kv] = softmax_update(s_buf[buf, p * N_KV + kv], v_s[j, kv, :, a:b], carries[kv], mask)
        return tuple(carries)

    def split3(x):
        hi = x.astype(BF16).astype(F32)
        mid = (x - hi).astype(BF16).astype(F32)
        return hi, mid, x - hi - mid

    init = (jnp.full((1, m), NEG_INF, F32), jnp.zeros((V_ROWS, m), F32))
    n_chunk = n_cmp // 8
    q4_ts, slope_rows, ali_ts, qaugs_w = [], [], [], []
    pair_t = [q_ref[:, j * LANES:(j + 1) * LANES].T for j in range(N_HEADS // 2)]
    zero_half = jnp.zeros((HEAD_DIM, tq), F32)
    for kv in range(N_KV):
        q_heads = []
        for g in range(GROUP):
            h = kv * GROUP + g
            feats = pair_t[h // 2][(h % 2) * HEAD_DIM:(h % 2 + 1) * HEAD_DIM]
            q_heads.append(jnp.concatenate([feats, zero_half] if kv == 0 else [zero_half, feats], axis=0))
        q4_t = jnp.concatenate([q * Q_SCALE for q in q_heads], axis=1).astype(BF16)
        q4l_t = jnp.concatenate([q * (Q_SCALE * LOG2E) for q in q_heads], axis=1).astype(BF16)
        slopes = [2.0 ** -(kv * GROUP + g + 1) for g in range(GROUP)]
        slope = jnp.concatenate([jnp.full((1, tq), s, F32) for s in slopes], axis=1)
        c_hi, c_lo = slope * LOG2E_HI, slope * LOG2E_LO
        d_hi, d_mid, d_lo = split3(-(c_hi + c_lo) * lane_t.astype(F32))
        ali_t = jnp.zeros((LANES, m), F32)
        for rows, val in ((P_POS_HI, (c_hi, c_lo)), (P_POS_LO, (c_hi, c_lo)), (P_ONE, (d_hi, d_mid, d_lo))):
            for r, v in zip(rows, val):
                ali_t = jnp.where(ext_row == r, v, ali_t)
        q4_ts.append(q4_t)
        slope_rows.append(slope)
        ali_ts.append((q4l_t, ali_t))
        qaugs_w.append(jnp.concatenate([q4l_t, ali_t.astype(BF16)], axis=0))

    c0 = jnp.maximum(q0 // WIN_BLOCK - WINDOW // WIN_BLOCK, 0)
    k_win = jnp.concatenate([kaug_w[c0 + t] for t in range(n_wblk)], axis=0)
    v_wins = [jnp.concatenate([v_w[c0 + t, kv] for t in range(n_wblk)], axis=1) for kv in range(N_KV)]
    scores_w = qk_scores(qaugs_w, k_win)

    o_cs, qaugs_s = [], []
    for kv in range(N_KV):
        q4_t, slope, (q4l_t, ali_t) = q4_ts[kv], slope_rows[kv], ali_ts[kv]
        kvc = kvc_ref[...]
        s_ct = _dot(kvc[:, :LANES].astype(BF16), q4_t) - slope * dist_c.astype(F32)
        p_ct = _masked_softmax(s_ct, mask_c, 0)
        o_c = _tn_dot(kvc[:, LANES:].astype(BF16), p_ct.astype(BF16))

        imp = p_ct[:, 0:tq]
        for g in range(1, GROUP):
            imp = imp + p_ct[:, g * tq:(g + 1) * tq]
        score = jnp.where(forced, SEL_FORCED, jnp.where(started, imp, SEL_INVALID))
        chunks = [score[8 * c:8 * c + 8] for c in range(n_chunk)]
        blk8 = lax.broadcasted_iota(jnp.int32, (8, tq), 0)
        ranks = [jnp.zeros((8, tq), F32) for _ in range(n_chunk)]
        for b in range(n_cmp):
            row = score[b:b + 1, :]
            for c in range(n_chunk):
                if 8 * c > b:
                    beats = row >= chunks[c]
                elif 8 * c + 7 < b:
                    beats = row > chunks[c]
                else:
                    beats = (row > chunks[c]) | ((row == chunks[c]) & (blk8 > b - 8 * c))
                ranks[c] = ranks[c] + jnp.where(beats, 1.0, 0.0)
        rank = jnp.concatenate(ranks, axis=0)
        sel_t = ((rank < N_SELECT) & (score > SEL_INVALID * 0.5)).astype(F32)
        sel_bias = (sel_t - 1.0) * -NEG_INF
        if n_cmp < LANES:
            sel_bias = jnp.concatenate([sel_bias, jnp.zeros((LANES - n_cmp, tq), F32)], axis=0)
        sel_bias = jnp.concatenate([sel_bias] * GROUP, axis=1)

        o_cs.append(o_c)
        qaugs_s.append(
            jnp.concatenate([q4l_t, jnp.where(ext_row < CMP_BLOCK, sel_bias, ali_t).astype(BF16)], axis=0))

    n_kt = (q0 + tq + tk - 1) // tk
    sel_fill(0, 0)

    key_w = c0 * WIN_BLOCK + lax.broadcasted_iota(jnp.int32, (n_wblk * WIN_BLOCK, 1), 0)

    def win_mask(a, b):
        causal = key_w[a:b] <= lane_t
        if a >= WIN_BLOCK:
            return causal
        return causal & (lane_t - key_w[a:b] < WINDOW)
    o_ws = normalized(softmax_pv(scores_w, v_wins, (init,) * N_KV, win_mask))

    n_full = n_kt - 1

    def sel_pair(jj, carries):
        j = 2 * jj
        sel_fill(1, j + 1)
        carries = sel_tile(0, j, carries, False)
        sel_fill(0, j + 2)
        return sel_tile(1, j + 1, carries, False)
    carries = lax.fori_loop(0, n_full // 2, sel_pair, (init,) * N_KV)
    j_a = 2 * (n_full // 2)
    odd = n_full - j_a
    sel_fill(1, n_kt - 1)
    carries = sel_tile(0, j_a, carries, True)
    carries = lax.cond(odd == 1, lambda c: sel_tile(1, n_kt - 1, c, True), lambda c: c, carries)
    o_ss = normalized(carries)

    heads = []
    for kv in range(N_KV):
        rows = slice(kv * HEAD_DIM, (kv + 1) * HEAD_DIM)
        for g in range(GROUP):
            h = kv * GROUP + g
            cols = slice(g * tq, (g + 1) * tq)
            heads.append(sig_t[3 * h:3 * h + 1] * o_cs[kv][rows, cols]
                         + sig_t[3 * h + 1:3 * h + 2] * o_ss[kv][:, cols]
                         + sig_t[3 * h + 2:3 * h + 3] * o_ws[kv][:, cols])
    o_ref[...] = jnp.concatenate(heads, axis=0).T


def _attn_prompt(qw, gl, kvc, ks_t, kw_t, ext, batch, seq, tq, tk):
    nq = seq // tq
    n_cmp = seq // CMP_BLOCK
    n_kt = seq // tk
    kern = functools.partial(_attn_prompt_kernel, seq=seq, tq=tq, tk=tk)
    kv_spec = lambda half: pl.BlockSpec((1, LANES, seq), lambda b, i: (b, half, 0))
    return pl.pallas_call(
        kern,
        grid=(batch, nq),
        in_specs=[pl.BlockSpec((tq, D_NSA), lambda b, i: (b * nq + i, 0)),
                  pl.BlockSpec((tq, LANES), lambda b, i: (b * nq + i, 0)),
                  pl.BlockSpec((n_cmp, D_KV), lambda b, i: (b, 0)),
                  kv_spec(0), kv_spec(1), kv_spec(0), kv_spec(1),
                  pl.BlockSpec((seq, LANES), lambda b, i: (0, 0))],
        out_specs=pl.BlockSpec((tq, D_NSA), lambda b, i: (b * nq + i, 0)),
        out_shape=jax.ShapeDtypeStruct((batch * seq, D_NSA), F32),
        scratch_shapes=[pltpu.VMEM((n_kt, tk, 2 * LANES), BF16),
                        pltpu.VMEM((seq // WIN_BLOCK, WIN_BLOCK, 2 * LANES), BF16),
                        pltpu.VMEM((n_kt, N_KV, V_ROWS, tk), BF16),
                        pltpu.VMEM((seq // WIN_BLOCK, N_KV, V_ROWS, WIN_BLOCK), BF16),
                        pltpu.VMEM((2, N_KV * tk // SUB_KEYS, SUB_KEYS, GROUP * tq), F32)],
        compiler_params=_params("arbitrary", "arbitrary"),
        name="attn_prompt",
    )(qw, gl, kvc, ks_t, ks_t, kw_t, kw_t, ext)


def _attn_sample_kernel(pt_ref, *refs, n_pages, past_len, win_keep):
    del pt_ref
    cmp_refs = refs[:n_pages]
    page_refs = refs[n_pages:2 * n_pages]
    (q_ref, gl_ref, ksn_ref, kwn_ref, win_ref, exts_ref, extw_ref, pe_ref, w1_ref, w2_ref,
     o_ref, kvc_buf, mean_buf) = refs[2 * n_pages:]
    n_cmp = past_len // CMP_BLOCK
    per_page = PAGE_SIZE // CMP_BLOCK

    @pl.when(pl.program_id(0) == 0)
    def _first():
        kvc_buf[...] = jnp.zeros(kvc_buf.shape, F32)

    kvc = kvc_buf[...]

    todo = list(range(n_pages))

    def compress_pages(count):
        for _ in range(min(count, len(todo))):
            p = todo.pop(0)
            mean_buf[p * per_page:(p + 1) * per_page, :] = _compress_mean(
                cmp_refs[p][0, 0], pe_ref[...], w1_ref[...])
    cur = past_len // CMP_BLOCK
    t_hi = float(cur * CMP_BLOCK)
    t_lo = float(past_len % CMP_BLOCK)

    qrow = q_ref[0]
    lane1 = lax.broadcasted_iota(jnp.int32, (1, LANES), 1)
    q_rows = []
    for h in range(N_HEADS):
        pair = qrow[:, (h // 2) * LANES:(h // 2 + 1) * LANES]
        if h % 2 != h // GROUP:
            pair = pltpu.roll(pair, HEAD_DIM, axis=1)
        q_rows.append(jnp.where(lane1 // HEAD_DIM == h // GROUP, pair, 0.0))
    q8 = (jnp.concatenate(q_rows, axis=0) * Q_SCALE).astype(BF16)
    head = lax.broadcasted_iota(jnp.int32, (N_HEADS, 1), 0)
    slope = jnp.exp2(-(head + 1).astype(F32))
    lane8 = lax.broadcasted_iota(jnp.int32, (N_HEADS, LANES), 1)
    ali = _alibi_lanes(slope, t_hi, t_lo, (N_HEADS, LANES))

    kc = kvc[:, :LANES]
    if n_cmp < LANES:
        kc = jnp.concatenate([kc, jnp.zeros((LANES - n_cmp, LANES), F32)], axis=0)
    blk = lax.broadcasted_iota(jnp.int32, (1, LANES), 1)
    dist_c = past_len - ((blk + 1) * CMP_BLOCK - 1)
    mask_c = (dist_c >= 0) & (blk < n_cmp)
    s_c = _nt_dot(q8, kc.astype(BF16)) - slope * dist_c.astype(F32)
    compress_pages(4)
    p_c = _masked_softmax(s_c, mask_c, 1)
    o_c = _dot(p_c[:, :n_cmp].astype(BF16), kvc[:, LANES:].astype(BF16))

    forced = (blk == cur) | (blk == 0)
    started = blk <= cur
    below = (lax.broadcasted_iota(jnp.int32, (LANES, LANES), 0)
             < lax.broadcasted_iota(jnp.int32, (LANES, LANES), 1))
    sel_rows = []
    for kv in range(N_KV):
        imp = jnp.sum(p_c[kv * GROUP:(kv + 1) * GROUP], axis=0, keepdims=True)
        score = jnp.where(forced, SEL_FORCED, jnp.where(started, imp, SEL_INVALID))
        col = _row_to_col(score)
        compress_pages(1)
        beats = (col > score) | ((col == score) & below)
        rank = jnp.sum(beats.astype(F32), axis=0, keepdims=True)
        sel = ((rank < N_SELECT) & (score > SEL_INVALID * 0.5)).astype(F32)
        sel_rows += [sel] * GROUP
    sel_bias = (jnp.concatenate(sel_rows, axis=0) - 1.0) * -NEG_INF
    qaug_s = jnp.concatenate([q8, jnp.where(lane8 < CMP_BLOCK, sel_bias, ali).astype(BF16)], axis=1)
    qaug_w = jnp.concatenate([q8, ali.astype(BF16)], axis=1)

    def new_key_score(row_ref):
        k_new = row_ref[0][:, :LANES].astype(BF16).astype(F32)
        return jnp.sum(q8.astype(F32) * k_new, axis=1, keepdims=True)

    def finish(scores, values, s_new, v_new):
        m = s_new
        for s in scores:
            m = jnp.maximum(m, jnp.max(s, axis=1, keepdims=True))
        p_new = jnp.exp(s_new - m)
        l = p_new
        acc = p_new * v_new.astype(BF16).astype(F32)
        for i, (s, v) in enumerate(zip(scores, values)):
            p = jnp.exp(s - m)
            l = l + jnp.sum(p, axis=1, keepdims=True)
            acc = acc + _nt_dot(p.astype(BF16), v)
            if i % 4 == 3:
                compress_pages(1)
        return acc / l

    scores, values = [], []
    for p, ref in enumerate(page_refs):
        if p % 2 == 0:
            compress_pages(1)
        page = ref[0, 0]
        kaug = jnp.concatenate([page[:LANES].astype(BF16), exts_ref[p]], axis=0)
        scores.append(_dot(qaug_s, kaug))
        values.append(page[LANES:].astype(BF16))
    o_s = finish(scores, values, new_key_score(ksn_ref), ksn_ref[0][:, LANES:])

    win = win_ref[0, 0]
    kaug = jnp.concatenate([win[:LANES].astype(BF16), extw_ref[...]], axis=0)
    s_w = _dot(qaug_w, kaug)
    dist_w = win_keep - lax.broadcasted_iota(jnp.int32, (1, win_keep), 1)
    s_w = jnp.where((dist_w >= 0) & (dist_w < WINDOW), s_w, NEG_INF)
    o_w = finish([s_w], [win[LANES:].astype(BF16)], new_key_score(kwn_ref), kwn_ref[0][:, LANES:])

    sig = _sigmoid(gl_ref[0])
    gate = [jnp.sum(jnp.where(lane8 == 3 * head + j, sig, 0.0), axis=1, keepdims=True) for j in range(3)]
    out = gate[0] * o_c + gate[1] * o_s + gate[2] * o_w
    out = jnp.where(head < GROUP, out, pltpu.roll(out, HEAD_DIM, axis=1))
    o_ref[0] = out[:, :HEAD_DIM]
    compress_pages(len(todo))
    kvc_buf[...] = _dot(mean_buf[...].astype(BF16), w2_ref[...])


def _attn_sample(cmp_t, sel_t, win_t, pt_flat, layer, pr, wts, ext_s, ext_w, db, n_pages, win_keep):
    past_len = n_pages * PAGE_SIZE
    n_cmp = past_len // CMP_BLOCK
    attn_row = lambda s: jnp.maximum(s - 1, 0)
    cmp_row = lambda s: jnp.minimum(s, db - 1)
    page = lambda row_of, p: pl.BlockSpec(
        (1, 1, D_KV, PAGE_SIZE), lambda s, pt: (layer, pt[row_of(s) * n_pages + p], 0, 0))
    row = lambda width: pl.BlockSpec((1, 1, width), lambda s, pt: (attn_row(s), 0, 0))
    const = lambda shape: pl.BlockSpec(shape, lambda s, pt: (0,) * len(shape))
    grid_spec = pltpu.PrefetchScalarGridSpec(
        num_scalar_prefetch=1,
        grid=(db + 1,),
        in_specs=[page(cmp_row, p) for p in range(n_pages)] + [page(attn_row, p) for p in range(n_pages)]
        + [row(D_NSA), row(LANES), row(D_KV), row(D_KV),
           pl.BlockSpec((1, 1, D_KV, win_keep), lambda s, pt: (layer, attn_row(s), 0, 0)),
           const((n_pages, LANES, PAGE_SIZE)), const((LANES, win_keep)),
           const((D_KV, LANES)), const((D_KV, D_KV)), const((D_KV, D_KV))],
        out_specs=pl.BlockSpec((1, N_HEADS, HEAD_DIM), lambda s, pt: (attn_row(s), 0, 0)),
        scratch_shapes=[pltpu.VMEM((n_cmp, D_KV), F32), pltpu.VMEM((n_cmp, D_KV), F32)],
    )
    kern = functools.partial(_attn_sample_kernel, n_pages=n_pages, past_len=past_len, win_keep=win_keep)
    out = pl.pallas_call(
        kern,
        grid_spec=grid_spec,
        out_shape=jax.ShapeDtypeStruct((db, N_HEADS, HEAD_DIM), F32),
        compiler_params=_params("arbitrary"),
        name="attn_sample",
    )(pt_flat, *([cmp_t] * n_pages), *([sel_t] * n_pages), pr["q"].reshape(db, 1, D_NSA),
      pr["gl"].reshape(db, 1, LANES), pr["ks"].reshape(db, 1, D_KV), pr["kw"].reshape(db, 1, D_KV),
      win_t, ext_s, ext_w, wts["pe_t"], wts["w1"], wts["w2"])
    return out.reshape(db, D_NSA)


def _win_update_kernel(win_ref, new_ref, o_ref, *, bb, win_keep):
    lane = lax.broadcasted_iota(jnp.int32, (D_KV, win_keep), 1)
    for k in range(bb):
        row = new_ref[0, k]
        col = jnp.concatenate([_row_to_col(row[:, h * LANES:(h + 1) * LANES]) for h in range(D_KV // LANES)],
                              axis=0)
        shifted = pltpu.roll(win_ref[0, k], win_keep - 1, axis=1)
        o_ref[0, k] = jnp.where(lane == win_keep - 1, col, shifted)


def _win_update(win_t, kw_new, bb):
    depth, db, _, win_keep = win_t.shape
    return pl.pallas_call(
        functools.partial(_win_update_kernel, bb=bb, win_keep=win_keep),
        grid=(depth, db // bb),
        in_specs=[pl.BlockSpec((1, bb, D_KV, win_keep), lambda l, b: (l, b, 0, 0)),
                  pl.BlockSpec((1, bb, 1, D_KV), lambda l, b: (l, b, 0, 0))],
        out_specs=pl.BlockSpec((1, bb, D_KV, win_keep), lambda l, b: (l, b, 0, 0)),
        out_shape=jax.ShapeDtypeStruct(win_t.shape, F32),
        compiler_params=_params("parallel", "parallel"),
        name="win_update",
    )(win_t, kw_new)


def _pool_window_lane(shape):
    lane = lax.broadcasted_iota(jnp.int32, shape, len(shape) - 1)
    w = jnp.full(shape, POOL_WINDOWS[0], jnp.int32)
    for gi in range(1, len(POOL_WINDOWS)):
        w = jnp.where(lane >= gi * POOL_GROUP_DIM, POOL_WINDOWS[gi], w)
    return w


def _group_rms(v, ones_bd):
    sq = v * v
    hi = sq.astype(BF16)
    lo = (sq - hi.astype(F32)).astype(BF16)
    return (_dot(hi, ones_bd) + _dot(lo, ones_bd)) * (1.0 / GMLP_GROUP_DIM)


def _mix_out(x, ya, za, yb, zb, yc, zc, wo_ref, gpost):
    out = _dot((ya * _silu(za)).astype(BF16), wo_ref[0:D_NSA, :])
    out += _dot((yb * _silu(zb)).astype(BF16), wo_ref[D_NSA:D_NSA + D_POOL, :])
    out += _dot((yc * _silu(zc)).astype(BF16), wo_ref[D_NSA + D_POOL:, :])
    ms = jnp.mean(out * out, axis=-1, keepdims=True)
    return x + out * lax.rsqrt(ms + EPS) * gpost


def _merge_prompt_kernel(x_ref, ya_ref, za_ref, pin_ref, prev_ref, zb_ref, u_ref, v_ref, zc_ref,
                         pw_ref, ps_ref, gn_ref, ones_ref, ws_ref, bs_ref, wo_ref, gp_ref,
                         xo_ref, vn_ref, buf_a, buf_b, *, tm, tiles_per_seq):
    i = pl.program_id(0)
    first = (i % tiles_per_seq) == 0
    pin = pin_ref[...]
    lo, n = POOL_MAX, tm + POOL_MAX
    buf_a[0:lo, :] = jnp.zeros((lo, D_POOL), F32)
    buf_b[0:lo, :] = jnp.zeros((lo, D_POOL), F32)
    buf_a[lo:2 * lo, :] = jnp.where(first, 0.0, prev_ref[...])
    buf_a[2 * lo:, :] = pin
    s2 = buf_a[pl.ds(lo, n), :] + buf_a[pl.ds(lo - 1, n), :]
    buf_b[lo:, :] = s2
    s4 = s2 + buf_b[pl.ds(lo - 2, n), :]
    buf_a[lo:, :] = s4
    s8 = s4 + buf_a[pl.ds(lo - 4, n), :]
    buf_b[lo:, :] = s8
    s16 = s8[lo:] + buf_b[pl.ds(2 * lo - 8, tm), :]
    w_lane = _pool_window_lane((tm, D_POOL))
    acc = jnp.where(w_lane == POOL_WINDOWS[0], s2[lo:],
                    jnp.where(w_lane == POOL_WINDOWS[1], s4[lo:], jnp.where(w_lane == POOL_WINDOWS[2], s8[lo:], s16)))
    pos = (i % tiles_per_seq) * tm + lax.broadcasted_iota(jnp.int32, (tm, D_POOL), 0)
    cnt = jnp.minimum(w_lane, pos + 1).astype(F32)
    diff = acc / cnt - pin
    yb = _dot(diff.astype(BF16), pw_ref[...]) * ps_ref[...]

    v = v_ref[...]
    vn = v * lax.rsqrt(_group_rms(v, ones_ref[...]) + EPS) * gn_ref[...]
    vn_ref[...] = vn
    vnb = vn.astype(BF16)
    lane = lax.broadcasted_iota(jnp.int32, (CHUNK, D_GMLP), 1)
    chunks = []
    for c in range(tm // CHUNK):
        vc = vnb[c * CHUNK:(c + 1) * CHUNK]
        s = _dot(ws_ref[0], vc)
        for g in range(1, D_GMLP // GMLP_GROUP_DIM):
            s = jnp.where(lane >= g * GMLP_GROUP_DIM, _dot(ws_ref[g], vc), s)
        chunks.append(s + bs_ref[...])
    yc = u_ref[...] * jnp.concatenate(chunks, axis=0)

    xo_ref[...] = _mix_out(x_ref[...], ya_ref[...], za_ref[...], yb, zb_ref[...], yc, zc_ref[...],
                           wo_ref, gp_ref[...])


def _merge_prompt(x2, ya, pr, wts, seq, tm):
    m = x2.shape[0]
    tiles_per_seq = seq // tm
    per = tm // POOL_MAX
    rows = lambda width: pl.BlockSpec((tm, width), lambda i: (i, 0))
    const = lambda shape: pl.BlockSpec(shape, lambda i: (0,) * len(shape))
    kern = functools.partial(_merge_prompt_kernel, tm=tm, tiles_per_seq=tiles_per_seq)
    return pl.pallas_call(
        kern,
        grid=(m // tm,),
        in_specs=[rows(D_MODEL), rows(D_NSA), rows(D_NSA), rows(D_POOL),
                  pl.BlockSpec((POOL_MAX, D_POOL), lambda i: (jnp.maximum(i * per - 1, 0), 0)),
                  rows(D_POOL), rows(D_GMLP), rows(D_GMLP), rows(D_GMLP),
                  const((D_POOL, D_POOL)), const((1, D_POOL)), const((1, D_GMLP)),
                  const((D_GMLP, D_GMLP)), const((4, CHUNK, CHUNK)), const((CHUNK, D_GMLP)),
                  const((D_MODEL, D_MODEL)), const((1, D_MODEL))],
        out_specs=[rows(D_MODEL), rows(D_GMLP)],
        out_shape=[jax.ShapeDtypeStruct((m, D_MODEL), F32), jax.ShapeDtypeStruct((m, D_GMLP), F32)],
        scratch_shapes=[pltpu.VMEM((tm + 2 * POOL_MAX, D_POOL), F32), pltpu.VMEM((tm + 2 * POOL_MAX, D_POOL), F32)],
        compiler_params=_params("parallel"),
        name="merge_prompt",
    )(x2, ya, pr["za"], pr["pin"], pr["pin"], pr["zb"], pr["u"], pr["v"], pr["zc"],
      wts["pool_w"], wts["pool_scale"], wts["gmlp_norm"], wts["ones_bd"], wts["ws_tril"], wts["bs_full"],
      wts["w_out"], wts["norm_post"])


def _merge_sample_kernel(x_ref, ya_ref, za_ref, pin_ref, st_ref, zb_ref, u_ref, v_ref, zc_ref,
                         pw_ref, ps_ref, gn_ref, ones_ref, w0_ref, b0_ref, wo_ref, gp_ref,
                         xo_ref, vn_ref):
    pin = pin_ref[...]
    w_lane = _pool_window_lane(pin.shape)
    acc = pin
    for k in range(1, POOL_MAX):
        acc = acc + jnp.where(w_lane > k, st_ref[POOL_MAX - 1 - k], 0.0)
    diff = acc / w_lane.astype(F32) - pin
    yb = _dot(diff.astype(BF16), pw_ref[...]) * ps_ref[...]

    v = v_ref[...]
    vn = v * lax.rsqrt(_group_rms(v, ones_ref[...]) + EPS) * gn_ref[...]
    vn_ref[...] = vn
    yc = u_ref[...] * (w0_ref[...] * vn + b0_ref[...])

    xo_ref[...] = _mix_out(x_ref[...], ya_ref[...], za_ref[...], yb, zb_ref[...], yc, zc_ref[...],
                           wo_ref, gp_ref[...])


def _merge_sample(x2, ya, pr, state_t, wts):
    db = x2.shape[0]
    full = lambda shape: pl.BlockSpec(shape, lambda i: (0,) * len(shape))
    args = (x2, ya, pr["za"], pr["pin"], state_t, pr["zb"], pr["u"], pr["v"], pr["zc"],
            wts["pool_w"], wts["pool_scale"], wts["gmlp_norm"], wts["ones_bd"], wts["w0"], wts["b0"],
            wts["w_out"], wts["norm_post"])
    return pl.pallas_call(
        _merge_sample_kernel,
        grid=(1,),
        in_specs=[full(a.shape) for a in args],
        out_specs=[full((db, D_MODEL)), full((db, D_GMLP))],
        out_shape=[jax.ShapeDtypeStruct((db, D_MODEL), F32), jax.ShapeDtypeStruct((db, D_GMLP), F32)],
        compiler_params=_params("arbitrary"),
        name="merge_sample",
    )(*args)


def _layer_weights(l, norm_pre, w_in, cmp_pe, cmp_w1, cmp_w2, pool_w, pool_scale, gmlp_norm, gmlp_ws, gmlp_bs,
                   w_out, norm_post):
    n_g = D_GMLP // GMLP_GROUP_DIM
    ones_bd = jnp.kron(jnp.eye(n_g, dtype=F32), jnp.ones((GMLP_GROUP_DIM, GMLP_GROUP_DIM), F32)).astype(BF16)
    pw = pool_w[l]
    zp = jnp.zeros_like(pw[0])
    pool_bd = jnp.concatenate(
        [jnp.concatenate([pw[i] if j == i else zp for j in range(4)], axis=1) for i in range(4)], axis=0)
    tril = jnp.tril(jnp.ones((CHUNK, CHUNK), F32))
    return dict(
        norm_pre=norm_pre[l].reshape(1, D_MODEL),
        w_in=_prep_w_in(w_in[l]),
        pe_t=_pe_t(cmp_pe[l]),
        w1=_blockdiag4(cmp_w1[l]).astype(BF16),
        w2=_blockdiag4(cmp_w2[l]).astype(BF16),
        pool_w=pool_bd.astype(BF16),
        pool_scale=pool_scale[l].reshape(1, D_POOL),
        gmlp_norm=gmlp_norm[l].reshape(1, D_GMLP),
        ones_bd=ones_bd,
        ws_tril=(gmlp_ws[l] * tril).astype(BF16),
        bs_full=jnp.repeat(gmlp_bs[l].T, GMLP_GROUP_DIM, axis=1),
        w0=jnp.repeat(gmlp_ws[l][:, 0, 0], GMLP_GROUP_DIM).reshape(1, D_GMLP),
        b0=jnp.repeat(gmlp_bs[l][:, 0], GMLP_GROUP_DIM).reshape(1, D_GMLP),
        w_out=w_out[l].astype(BF16),
        norm_post=norm_post[l].reshape(1, D_MODEL),
    )


def _largest_tile(n, cap):
    t = cap
    while n % t:
        t //= 2
    return t


def _feature_major(cache):
    d, n, t = cache.shape[:3]
    return jnp.transpose(cache, (0, 1, 3, 4, 5, 2)).reshape(d, n, D_KV, t)


def _token_major(x_t):
    d, n, _, t = x_t.shape
    return jnp.transpose(x_t.reshape(d, n, 2, N_KV, HEAD_DIM, t), (0, 1, 5, 2, 3, 4))


def kernel(x_prompt, x_sample, cache_kv_cmp, cache_kv_sel, cache_kv_win, state_pool, page_table, norm_pre, w_in,
           cmp_pe, cmp_w1, cmp_w2, pool_w, pool_scale, gmlp_norm, gmlp_ws, gmlp_bs, w_out, norm_post):
    bp, tp, _ = x_prompt.shape
    db, ts, _ = x_sample.shape
    depth = w_in.shape[0]
    n_pages = page_table.shape[1]
    past_len = n_pages * PAGE_SIZE
    win_keep = cache_kv_win.shape[2]
    assert ts == 1 and tp % CHUNK == 0 and tp <= CMP_BLOCK * CMP_BLOCK and tp >= WINDOW
    assert past_len // CMP_BLOCK + 1 <= CMP_BLOCK and win_keep == WINDOW and db % LANES == 0

    mp = bp * tp
    tm_proj = _largest_tile(tp, 512)
    tile_cmp = _largest_tile(tp, 1024)
    tm_merge = _largest_tile(tp, 512)
    tq, tk = 128, _largest_tile(tp, 512)
    assert tp >= WINDOW + tq

    cmp_t = _feature_major(cache_kv_cmp)
    sel_t = _feature_major(cache_kv_sel)
    win_t = _feature_major(cache_kv_win)
    pt_flat = page_table.reshape(-1).astype(jnp.int32)

    ext_prompt = _prompt_key_lanes(np.arange(tp))
    ext_s = np.ascontiguousarray(np.transpose(
        _extra_rows(np.arange(past_len), True).reshape(LANES, n_pages, PAGE_SIZE), (1, 0, 2)))
    ext_w = _extra_rows(past_len - win_keep + np.arange(win_keep), False)

    xp = x_prompt.reshape(mp, D_MODEL)
    xs = x_sample.reshape(db, D_MODEL)
    names = ("kvc_p", "kvc_s", "kvs_p", "kvs_s", "kvw_p", "kw_new", "pool_p", "pool_s", "gv_p", "gv_s")
    outs = {k: [] for k in names}
    for l in range(depth):
        wts = _layer_weights(l, norm_pre, w_in, cmp_pe, cmp_w1, cmp_w2, pool_w, pool_scale, gmlp_norm, gmlp_ws,
                             gmlp_bs, w_out, norm_post)
        pr = _project(xp, wts["norm_pre"], wts["w_in"], bp, tp, tm_proj, False)
        kvc = _compress_prompt(pr["kc_t"], wts, tile_cmp)
        ya = _attn_prompt(pr["q"], pr["gl"], kvc, pr["ks_t"], pr["kw_t"], ext_prompt, bp, tp, tq, tk)
        xp, vn = _merge_prompt(xp, ya, pr, wts, tp, tm_merge)
        outs["kvc_p"].append(pr["kc_t"])
        outs["kvs_p"].append(pr["ks_t"])
        outs["kvw_p"].append(pr["kw_t"][:, :, tp - WINDOW:])
        outs["pool_p"].append(pr["pin"].reshape(bp, tp, D_POOL)[:, tp - (POOL_MAX - 1):])
        outs["gv_p"].append(vn.reshape(bp, tp, D_GMLP)[:, tp - CHUNK:])
        pr = _project(xs, wts["norm_pre"], wts["w_in"], 1, db, db, True)
        ya = _attn_sample(cmp_t, sel_t, win_t, pt_flat, l, pr, wts, ext_s, ext_w, db, n_pages, win_keep)
        state_t = jnp.swapaxes(state_pool[l], 0, 1)
        xs, vn = _merge_sample(xs, ya, pr, state_t, wts)
        outs["kvc_s"].append(pr["kc_t"])
        outs["kvs_s"].append(pr["ks_t"])
        outs["kw_new"].append(pr["kw"].reshape(db, 1, D_KV))
        outs["pool_s"].append(jnp.concatenate([state_t[1:], pr["pin"][None]], axis=0))
        outs["gv_s"].append(vn.reshape(db, 1, D_GMLP))
    st = {k: jnp.stack(v) for k, v in outs.items()}
    kvw_s = _win_update(win_t, st["kw_new"], _largest_tile(db, 16))
    sample_kv = lambda x_t: jnp.transpose(_token_major(x_t), (0, 2, 1, 3, 4, 5))
    return (xp.reshape(bp, tp, D_MODEL), xs.reshape(db, ts, D_MODEL),
            _token_major(st["kvc_p"]), sample_kv(st["kvc_s"]),
            _token_major(st["kvs_p"]), sample_kv(st["kvs_s"]),
            _token_major(st["kvw_p"]), _token_major(kvw_s),
            st["pool_p"], jnp.swapaxes(st["pool_s"], 1, 2), st["gv_p"], st["gv_s"])
```
